```python
import jax
import jax.numpy as jnp
from jax import lax
import numpy as np

D_MODEL = 1024
BATCH = 4
SEQ = 8192
DEPTH = 1
DEC_BATCH = 32
DEC_SEQ = 1
PAST_LEN = 16384
PAGE_SIZE = 128

CONV_CH = D_MODEL // 2
CONV_W = 3
N_HEADS = 8
HEAD_DIM = (D_MODEL - CONV_CH) // N_HEADS
ATTN_W = N_HEADS * HEAD_DIM
N_KV_HEADS = 2
GROUP = N_HEADS // N_KV_HEADS
N_BRANCH = 3
CMP_BLK = 32
SEL_BLK = 64
SEL_PER_CMP = SEL_BLK // CMP_BLK
N_SEL = 16
WINDOW = 512
Q_BLK = 128
ROPE_THETA = 10000.0
NORM_EPS = 1e-6
FORCE_BONUS = 1e4
KV_W = N_BRANCH * 2 * N_KV_HEADS * HEAD_DIM
IN_SPLITS = (CONV_CH, CONV_CH, CONV_CH, CONV_CH, ATTN_W, KV_W, N_HEADS * N_BRANCH, ATTN_W)
IN_W = sum(IN_SPLITS)
MIX_W = CONV_CH + ATTN_W

kernel_name = "hymba_shortconv_nsa_decode_step"


def rms_norm(x, g):
    xf = x.astype(jnp.float32)
    xf = xf * lax.rsqrt(jnp.mean(xf * xf, axis=-1, keepdims=True) + NORM_EPS)
    return (xf * g.astype(jnp.float32)).astype(x.dtype)


def rope(x, pos):
    half = HEAD_DIM // 2
    inv = ROPE_THETA ** (-jnp.arange(half, dtype=jnp.float32) / half)
    ang = pos.astype(jnp.float32)[:, None] * inv[None, :]
    shape = (1, pos.shape[0]) + (1,) * (x.ndim - 3) + (half,)
    cos = jnp.cos(ang).reshape(shape)
    sin = jnp.sin(ang).reshape(shape)
    xf = x.astype(jnp.float32)
    x1, x2 = xf[..., :half], xf[..., half:]
    return jnp.concatenate([x1 * cos - x2 * sin, x2 * cos + x1 * sin], axis=-1).astype(x.dtype)


def masked_softmax(s, mask):
    s = jnp.where(mask, s.astype(jnp.float32), -jnp.inf)
    m = jnp.max(s, axis=-1, keepdims=True)
    m = jnp.where(jnp.isfinite(m), m, 0.0)
    p = jnp.exp(s - m)
    return p / jnp.maximum(jnp.sum(p, axis=-1, keepdims=True), 1e-30)


def short_conv(u_ext, w, bias):
    t = u_ext.shape[1] - (CONV_W - 1)
    out = w[0] * u_ext[:, 0:t]
    for k in range(1, CONV_W):
        out = out + w[k] * u_ext[:, k:k + t]
    return out + bias


def compress_blocks(kv, cmp_pe, cmp_w):
    b, l = kv.shape[:2]
    nc = l // CMP_BLK
    blocks = kv[:, :nc * CMP_BLK].reshape(b, nc, CMP_BLK, 2, N_KV_HEADS, HEAD_DIM)
    blocks = blocks + cmp_pe[None, None, :, :, None, :]
    return jnp.einsum('bcrjgd,rjde->bcjge', blocks, cmp_w)


def to_sel_blocks(kv):
    b, l = kv.shape[:2]
    ns = -(-l // SEL_BLK)
    kv = jnp.pad(kv, ((0, 0), (0, ns * SEL_BLK - l), (0, 0), (0, 0), (0, 0)))
    return kv.reshape(b, ns, SEL_BLK, 2, N_KV_HEADS, HEAD_DIM).transpose(0, 4, 1, 2, 3, 5)


def nsa_block(q, q_pos, gate, kv_cmp, sel_blocks, kv_win, win_pos):
    b, tq = q.shape[:2]
    nc = kv_cmp.shape[1]
    ns = sel_blocks.shape[2]
    qp = q_pos[:, None]
    s_cmp = jnp.einsum('bqgnd,bcgd->bqgnc', q, kv_cmp[:, :, 0])
    cmp_end = (jnp.arange(nc, dtype=jnp.int32) + 1) * CMP_BLK - 1
    p_cmp = masked_softmax(s_cmp, (cmp_end[None, :] <= qp)[None, :, None, None, :])
    o_cmp = jnp.einsum('bqgnc,bcgd->bqgnd', p_cmp.astype(q.dtype), kv_cmp[:, :, 1])
    imp = jnp.sum(p_cmp, axis=3)
    imp = jnp.pad(imp, ((0, 0), (0, 0), (0, 0), (0, ns * SEL_PER_CMP - nc)))
    imp = imp.reshape(b, tq, N_KV_HEADS, ns, SEL_PER_CMP).sum(-1)
    blk = jnp.arange(ns, dtype=jnp.int32)[None, :]
    q_blk = (q_pos // SEL_BLK)[:, None]
    valid = blk * SEL_BLK <= qp
    forced = (blk == 0) | (blk == q_blk) | (blk == q_blk - 1)
    bonus = jnp.where(forced, FORCE_BONUS, 0.0).astype(jnp.float32)
    score = jnp.where(valid[None, :, None, :], imp + bonus[None, :, None, :], -jnp.inf)
    _, idx = lax.top_k(score, min(N_SEL, ns))
    bi = jnp.arange(b)[:, None, None, None]
    gi = jnp.arange(N_KV_HEADS)[None, None, :, None]
    sel = sel_blocks[bi, gi, idx]
    n_keys = idx.shape[-1] * SEL_BLK
    sel = sel.reshape(b, tq, N_KV_HEADS, n_keys, 2, HEAD_DIM)
    sel_pos = (idx[..., None] * SEL_BLK + jnp.arange(SEL_BLK, dtype=jnp.int32)).reshape(b, tq, N_KV_HEADS, n_keys)
    s_slc = jnp.einsum('bqgnd,bqgkd->bqgnk', q, sel[..., 0, :])
    m_slc = (sel_pos <= q_pos[None, :, None, None])[:, :, :, None, :]
    p_slc = masked_softmax(s_slc, m_slc)
    o_slc = jnp.einsum('bqgnk,bqgkd->bqgnd', p_slc.astype(q.dtype), sel[..., 1, :])
    s_win = jnp.einsum('bqgnd,bkgd->bqgnk', q, kv_win[:, :, 0])
    wp = win_pos[None, :]
    m_win = (wp <= qp) & (wp > qp - WINDOW) & (wp >= 0)
    p_win = masked_softmax(s_win, m_win[None, :, None, None, :])
    o_win = jnp.einsum('bqgnk,bkgd->bqgnd', p_win.astype(q.dtype), kv_win[:, :, 1])
    o = gate[..., 0:1] * o_cmp + gate[..., 1:2] * o_slc + gate[..., 2:3] * o_win
    return o.reshape(b, tq, N_HEADS, HEAD_DIM)


def mixer_inputs(x, pos, norm_g, w_in, q_gain, k_gain):
    bsz, t = x.shape[:2]
    h = rms_norm(x, norm_g)
    proj = h @ w_in
    offs = np.cumsum(IN_SPLITS)[:-1].tolist()
    b_gate, c_gate, conv_in, z_conv, q, kv, gate_logits, z_attn = jnp.split(proj, offs, axis=-1)
    q = q.reshape(bsz, t, N_HEADS, HEAD_DIM)
    q = rope(rms_norm(q, q_gain), pos) * (HEAD_DIM ** -0.5)
    q = q.reshape(bsz, t, N_KV_HEADS, GROUP, HEAD_DIM)
    kv = kv.reshape(bsz, t, N_BRANCH, 2, N_KV_HEADS, HEAD_DIM)
    k = rope(rms_norm(kv[:, :, :, 0], k_gain[:, None, :]), pos)
    kv = jnp.stack([k, kv[:, :, :, 1]], axis=3)
    gate = jax.nn.sigmoid(gate_logits.reshape(bsz, t, N_KV_HEADS, GROUP, N_BRANCH))
    u = c_gate * conv_in
    return u, b_gate, z_conv, q, kv, gate, z_attn


def mixer_output(x, conv_y, b_gate, z_conv, attn_o, z_attn, w_out):
    bsz, t = x.shape[:2]
    conv_branch = b_gate * conv_y * jax.nn.silu(z_conv)
    attn_branch = attn_o.reshape(bsz, t, ATTN_W) * jax.nn.silu(z_attn)
    return x + jnp.concatenate([conv_branch, attn_branch], axis=-1) @ w_out


def prompt_layer(x, norm_g, w_in, conv_w, conv_b, q_gain, k_gain, cmp_pe, cmp_w, w_out):
    bsz, t = x.shape[:2]
    pos = jnp.arange(t, dtype=jnp.int32)
    u, b_gate, z_conv, q, kv, gate, z_attn = mixer_inputs(x, pos, norm_g, w_in, q_gain, k_gain)
    conv_y = short_conv(jnp.pad(u, ((0, 0), (CONV_W - 1, 0), (0, 0))), conv_w, conv_b)
    kv_cmp_rows, kv_slc_rows, kv_win_rows = kv[:, :, 0], kv[:, :, 1], kv[:, :, 2]
    kv_cmp = compress_blocks(kv_cmp_rows, cmp_pe, cmp_w)
    sel_blocks = to_sel_blocks(kv_slc_rows)
    win_pad = jnp.pad(kv_win_rows, ((0, 0), (WINDOW, 0), (0, 0), (0, 0), (0, 0)))

    def q_block(i):
        s = i * Q_BLK
        q_b = lax.dynamic_slice_in_dim(q, s, Q_BLK, axis=1)
        g_b = lax.dynamic_slice_in_dim(gate, s, Q_BLK, axis=1)
        kv_w = lax.dynamic_slice_in_dim(win_pad, s, WINDOW + Q_BLK, axis=1)
        q_pos = s + jnp.arange(Q_BLK, dtype=jnp.int32)
        w_pos = s - WINDOW + jnp.arange(WINDOW + Q_BLK, dtype=jnp.int32)
        return nsa_block(q_b, q_pos, g_b, kv_cmp, sel_blocks, kv_w, w_pos)

    o = lax.map(q_block, jnp.arange(t // Q_BLK, dtype=jnp.int32))
    o = jnp.moveaxis(o, 0, 1).reshape(bsz, t, N_HEADS, HEAD_DIM)
    y = mixer_output(x, conv_y, b_gate, z_conv, o, z_attn, w_out)
    w_keep = min(WINDOW, t)
    return y, kv_cmp_rows, kv_slc_rows, kv_win_rows[:, t - w_keep:], u[:, t - (CONV_W - 1):]


def sample_layer(x, cache_cmp, cache_slc, win_buf, conv_buf, page_table,
                 norm_g, w_in, conv_w, conv_b, q_gain, k_gain, cmp_pe, cmp_w, w_out):
    bsz, t = x.shape[:2]
    past = page_table.shape[1] * PAGE_SIZE
    pos = past + jnp.arange(t, dtype=jnp.int32)
    u, b_gate, z_conv, q, kv, gate, z_attn = mixer_inputs(x, pos, norm_g, w_in, q_gain, k_gain)
    u_ext = jnp.concatenate([conv_buf.astype(u.dtype), u], axis=1)
    conv_y = short_conv(u_ext, conv_w, conv_b)

    def gather_past(pool):
        return pool[page_table].reshape(bsz, past, 2, N_KV_HEADS, HEAD_DIM)

    full_cmp = jnp.concatenate([gather_past(cache_cmp).astype(kv.dtype), kv[:, :, 0]], axis=1)
    full_slc = jnp.concatenate([gather_past(cache_slc).astype(kv.dtype), kv[:, :, 1]], axis=1)
    kv_cmp = compress_blocks(full_cmp, cmp_pe, cmp_w)
    sel_blocks = to_sel_blocks(full_slc)
    w_buf = win_buf.shape[1]
    kv_w = jnp.concatenate([win_buf.astype(kv.dtype), kv[:, :, 2]], axis=1)
    w_pos = past - w_buf + jnp.arange(w_buf + t, dtype=jnp.int32)
    o = nsa_block(q, pos, gate, kv_cmp, sel_blocks, kv_w, w_pos)
    y = mixer_output(x, conv_y, b_gate, z_conv, o, z_attn, w_out)
    return y, kv[:, :, 0], kv[:, :, 1], kv_w[:, t:], u_ext[:, t:]


def setup_inputs(seed: int = 0) -> dict:
    key = jax.random.key(seed)
    ks = jax.random.split(key, 16)
    n_pages = PAST_LEN // PAGE_SIZE
    n_used = DEC_BATCH * n_pages
    n_pool = n_used + max(1, n_used // 4)
    w_buf = min(WINDOW, PAST_LEN)
    f32 = jnp.float32
    nrm = lambda k, s: jax.random.normal(k, s, f32)
    page_table = jax.random.permutation(ks[6], n_pool)[:n_used].reshape(DEC_BATCH, n_pages).astype(jnp.int32)
    return {
        "x_prompt": nrm(ks[0], (BATCH, SEQ, D_MODEL)),
        "x_sample": nrm(ks[1], (DEC_BATCH, DEC_SEQ, D_MODEL)),
        "cache_cmp_kv": nrm(ks[2], (DEPTH, n_pool, PAGE_SIZE, 2, N_KV_HEADS, HEAD_DIM)),
        "cache_slc_kv": nrm(ks[3], (DEPTH, n_pool, PAGE_SIZE, 2, N_KV_HEADS, HEAD_DIM)),
        "state_win_kv": nrm(ks[4], (DEPTH, DEC_BATCH, w_buf, 2, N_KV_HEADS, HEAD_DIM)),
        "state_conv": nrm(ks[5], (DEPTH, DEC_BATCH, CONV_W - 1, CONV_CH)),
        "page_table": page_table,
        "norm_g": 1.0 + 0.02 * nrm(ks[7], (DEPTH, D_MODEL)),
        "w_in": nrm(ks[8], (DEPTH, D_MODEL, IN_W)) * D_MODEL ** -0.5,
        "conv_w": nrm(ks[9], (DEPTH, CONV_W, CONV_CH)) * CONV_W ** -0.5,
        "conv_b": 0.02 * nrm(ks[10], (DEPTH, CONV_CH)),
        "q_gain": 1.0 + 0.02 * nrm(ks[11], (DEPTH, HEAD_DIM)),
        "k_gain": 1.0 + 0.02 * nrm(ks[12], (DEPTH, N_BRANCH, HEAD_DIM)),
        "cmp_pe": 0.1 * nrm(ks[13], (DEPTH, CMP_BLK, 2, HEAD_DIM)),
        "cmp_w": nrm(ks[14], (DEPTH, CMP_BLK, 2, HEAD_DIM, HEAD_DIM)) * (CMP_BLK * HEAD_DIM) ** -0.5,
        "w_out": nrm(ks[15], (DEPTH, MIX_W, D_MODEL)) * MIX_W ** -0.5,
    }


def reference(x_prompt, x_sample, cache_cmp_kv, cache_slc_kv, state_win_kv, state_conv, page_table,
              norm_g, w_in, conv_w, conv_b, q_gain, k_gain, cmp_pe, cmp_w, w_out):
    yp, ys = x_prompt, x_sample
    p_cmp, p_slc, p_win, p_conv = [], [], [], []
    s_cmp, s_slc, s_win, s_conv = [], [], [], []
    for layer in range(DEPTH):
        params = (norm_g[layer], w_in[layer], conv_w[layer], conv_b[layer], q_gain[layer],
                  k_gain[layer], cmp_pe[layer], cmp_w[layer], w_out[layer])
        yp, a, b, c, d = prompt_layer(yp, *params)
        p_cmp.append(a); p_slc.append(b); p_win.append(c); p_conv.append(d)
        ys, a, b, c, d = sample_layer(ys, cache_cmp_kv[layer], cache_slc_kv[layer], state_win_kv[layer],
                                      state_conv[layer], page_table, *params)
        s_cmp.append(a); s_slc.append(b); s_win.append(c); s_conv.append(d)
    return (yp, ys, jnp.stack(p_cmp), jnp.stack(p_slc), jnp.stack(p_win), jnp.stack(p_conv),
            jnp.stack(s_cmp), jnp.stack(s_slc), jnp.stack(s_win), jnp.stack(s_conv))
```

```python
import functools

import jax
import jax.numpy as jnp
from jax import lax
from jax.experimental import pallas as pl
from jax.experimental.pallas import tpu as pltpu

F32 = jnp.float32
BF16 = jnp.bfloat16

D_MODEL = 1024
CONV_CH = 512
CONV_W = 3
N_HEADS = 8
HEAD_DIM = 64
ATTN_W = N_HEADS * HEAD_DIM
N_KV = 2
GROUP = N_HEADS // N_KV
N_BRANCH = 3
CMP_BLK = 32
SEL_BLK = 64
N_SEL = 16
WINDOW = 512
PAGE = 128
ROPE_THETA = 10000.0
NORM_EPS = 1e-6
FORCE_BONUS = 1e4
KV_LANES = 2 * N_KV * HEAD_DIM
LANES = 128
SEL_CHUNK = 64
NEG_BIAS = -1e9
NEG_MASK = -1e30

C_CONV = 0
C_Q = 4 * CONV_CH
C_KV = C_Q + ATTN_W
C_Z = C_KV + N_BRANCH * KV_LANES
C_G = C_Z + ATTN_W
W_COLS = C_G + LANES

VMEM_LIMIT = 48 * 1024 * 1024


def _dot(a, b):
    return jnp.dot(a, b, preferred_element_type=F32)


def _dot_nt(a, b):
    return lax.dot_general(a, b, (((1,), (1,)), ((), ())), preferred_element_type=F32)


def _iota(shape, dim):
    return lax.broadcasted_iota(jnp.int32, shape, dim)


def _masked_softmax(s, mask):
    s = jnp.where(mask, s, -jnp.inf)
    m = jnp.max(s, axis=-1, keepdims=True)
    m = jnp.where(m > -jnp.inf, m, 0.0)
    p = jnp.exp(s - m)
    return p * (1.0 / jnp.maximum(jnp.sum(p, axis=-1, keepdims=True), 1e-30))


def _group_mean_sq(x):
    x2 = x * x
    hi = x2.astype(BF16)
    lo = (x2 - hi.astype(F32)).astype(BF16)
    same = (_iota((LANES, LANES), 0) >> 6) == (_iota((LANES, LANES), 1) >> 6)
    ones = jnp.where(same, 1.0, 0.0).astype(BF16)
    return (_dot(hi, ones) + _dot(lo, ones)) * (1.0 / HEAD_DIM)


def _head_norm_rope(x, gain, cos2, sin2):
    xn = x * lax.rsqrt(_group_mean_sq(x) + NORM_EPS) * gain
    lane = _iota(x.shape, 1)
    swapped = jnp.where((lane & (HEAD_DIM - 1)) < HEAD_DIM // 2,
                        pltpu.roll(xn, LANES - HEAD_DIM // 2, 1), pltpu.roll(xn, HEAD_DIM // 2, 1))
    return xn * cos2 + swapped * sin2


def _low_half(x, other=0.0):
    lane = _iota(x.shape, 1)
    return jnp.where(lane < HEAD_DIM, x, other)


def _swap_halves(x):
    return pltpu.roll(x, HEAD_DIM, 1)


def _normed_input(x, norm_g):
    ms = jnp.mean(x * x, axis=-1, keepdims=True)
    return (x * lax.rsqrt(ms + NORM_EPS) * norm_g).astype(BF16)


def _silu(z):
    return z * jax.nn.sigmoid(z)


def _prompt_in_kernel(x_ref, ng_ref, w_ref, cw_ref, cb_ref, qg_ref, kg_ref, cos_ref, sin_ref,
                      convb_ref, sz_ref, q_ref, kvc_ref, kvs_ref, kvw_ref, ksa_ref, vsa_ref, kw_ref, vw_ref,
                      gate_ref, utail_ref, carry_ref, *, tm):
    ti = pl.program_id(1)
    h = _normed_input(x_ref[...], ng_ref[...])
    cos2 = cos_ref[...]
    sin2 = sin_ref[...]

    pc = _dot(h, w_ref[:, C_CONV:C_Q])
    b_gate = pc[:, 0:CONV_CH]
    u = pc[:, CONV_CH:2 * CONV_CH] * pc[:, 2 * CONV_CH:3 * CONV_CH]
    z_conv = pc[:, 3 * CONV_CH:4 * CONV_CH]

    @pl.when(ti == 0)
    def _():
        carry_ref[...] = jnp.zeros_like(carry_ref)

    prev1 = carry_ref[7:8, :]
    prev2 = carry_ref[6:7, :]
    row = _iota(u.shape, 0)
    u1 = jnp.where(row == 0, prev1, pltpu.roll(u, 1, 0))
    u2 = jnp.where(row == 0, prev2, jnp.where(row == 1, prev1, pltpu.roll(u, 2, 0)))
    conv_y = cw_ref[0:1, :] * u2 + cw_ref[1:2, :] * u1 + cw_ref[2:3, :] * u + cb_ref[...]
    convb_ref[...] = (b_gate * conv_y * _silu(z_conv)).astype(BF16)
    carry_ref[...] = u[tm - 8:tm, :]
    utail_ref[...] = u[tm - 8:tm, :]

    pq = _dot(h, w_ref[:, C_Q:C_KV])
    for c in range(N_HEADS // 2):
        qc = _head_norm_rope(pq[:, c * LANES:(c + 1) * LANES], qg_ref[...], cos2, sin2) * (HEAD_DIM ** -0.5)
        for half in range(2):
            head = 2 * c + half
            src = qc if half == 0 else _swap_halves(qc)
            q_ref[head // GROUP, head % GROUP] = _low_half(src).astype(BF16)

    pk = _dot(h, w_ref[:, C_KV:C_Z])
    lane = _iota((tm, LANES), 1)
    tpos = ti * tm + _iota((tm, LANES), 0)
    onehot = jnp.where(((tpos >> 6) & (SEL_CHUNK - 1)) == lane - HEAD_DIM, 1.0, 0.0)
    for br, out_ref in enumerate((kvc_ref, kvs_ref, kvw_ref)):
        kp = _head_norm_rope(pk[:, br * KV_LANES:br * KV_LANES + LANES], kg_ref[br:br + 1, :], cos2, sin2)
        vv = pk[:, br * KV_LANES + LANES:(br + 1) * KV_LANES]
        out_ref[:, 0:LANES] = kp
        out_ref[:, LANES:KV_LANES] = vv
        if br == 1:
            for g in range(N_KV):
                ksa_ref[g] = _low_half(kp if g == 0 else _swap_halves(kp), onehot).astype(BF16)
                vsa_ref[g] = _low_half(vv if g == 0 else _swap_halves(vv)).astype(BF16)
        if br == 2:
            for g in range(N_KV):
                kw_ref[g] = _low_half(kp if g == 0 else _swap_halves(kp)).astype(BF16)
                vw_ref[g] = _low_half(vv if g == 0 else _swap_halves(vv)).astype(BF16)

    sz_ref[...] = _silu(_dot(h, w_ref[:, C_Z:C_G]))
    sg = jax.nn.sigmoid(_dot(h, w_ref[:, C_G:W_COLS]))
    gate_ref[0] = sg
    gate_ref[1] = pltpu.roll(sg, LANES - GROUP * N_BRANCH, 1)


def _prompt_in(x, ng, w_all, cw, cb, qg, kg, cos2, sin2, tm):
    b, t, _ = x.shape
    grid = (b, t // tm)
    row_blk = lambda w: pl.BlockSpec((None, tm, w), lambda bi, ti: (bi, ti, 0))
    full = lambda a: pl.BlockSpec(a.shape, lambda bi, ti: (0,) * a.ndim)
    head_blk = pl.BlockSpec((None, N_KV, tm, LANES), lambda bi, ti: (bi, 0, ti, 0))
    out_shape = (
        jax.ShapeDtypeStruct((b, t, CONV_CH), BF16),
        jax.ShapeDtypeStruct((b, t, ATTN_W), F32),
        jax.ShapeDtypeStruct((b, N_KV, GROUP, t, LANES), BF16),
        jax.ShapeDtypeStruct((b, t, KV_LANES), F32),
        jax.ShapeDtypeStruct((b, t, KV_LANES), F32),
        jax.ShapeDtypeStruct((b, t, KV_LANES), F32),
        jax.ShapeDtypeStruct((b, N_KV, t, LANES), BF16),
        jax.ShapeDtypeStruct((b, N_KV, t, LANES), BF16),
        jax.ShapeDtypeStruct((b, N_KV, t, LANES), BF16),
        jax.ShapeDtypeStruct((b, N_KV, t, LANES), BF16),
        jax.ShapeDtypeStruct((b, N_KV, t, LANES), F32),
        jax.ShapeDtypeStruct((b, 8, CONV_CH), F32),
    )
    out_specs = (
        row_blk(CONV_CH), row_blk(ATTN_W),
        pl.BlockSpec((None, N_KV, GROUP, tm, LANES), lambda bi, ti: (bi, 0, 0, ti, 0)),
        row_blk(KV_LANES), row_blk(KV_LANES), row_blk(KV_LANES),
        head_blk, head_blk, head_blk, head_blk, head_blk,
        pl.BlockSpec((None, 8, CONV_CH), lambda bi, ti: (bi, 0, 0)),
    )
    tab = pl.BlockSpec((tm, LANES), lambda bi, ti: (ti, 0))
    return pl.pallas_call(
        functools.partial(_prompt_in_kernel, tm=tm),
        grid=grid,
        in_specs=[row_blk(D_MODEL), full(ng), full(w_all), full(cw), full(cb), full(qg), full(kg), tab, tab],
        out_specs=out_specs,
        out_shape=out_shape,
        scratch_shapes=[pltpu.VMEM((8, CONV_CH), F32)],
        compiler_params=pltpu.CompilerParams(dimension_semantics=("arbitrary", "arbitrary"),
                                             vmem_limit_bytes=VMEM_LIMIT),
        name="prompt_in_proj",
    )(x, ng, w_all, cw, cb, qg, kg, cos2, sin2)


def _compress_rows(xk_ref, xv_ref, pe_ref, wbd_ref, acc_ref, n_even):
    acc_ref[...] = jnp.zeros_like(acc_ref)

    def rows(ref, r):
        even = ref[pl.ds(r, n_even, stride=2 * CMP_BLK), :]
        odd = ref[pl.ds(CMP_BLK + r, n_even, stride=2 * CMP_BLK), :]
        return jnp.concatenate([even, odd], axis=0)

    def body(r, carry):
        xr = jnp.concatenate([rows(xk_ref, r), rows(xv_ref, r)], axis=1) + pe_ref[pl.ds(r, 1), :]
        acc_ref[...] += _dot(xr.astype(BF16), wbd_ref[r])
        return carry

    lax.fori_loop(0, CMP_BLK, body, 0)


def _prompt_compress_kernel(xk_ref, xv_ref, pe_ref, wbd_ref, kc_ref, vc_ref, acc_ref, *, n_even):
    _compress_rows(xk_ref, xv_ref, pe_ref, wbd_ref, acc_ref, n_even)
    kk = acc_ref[:, 0:LANES]
    vv = acc_ref[:, LANES:KV_LANES]
    for g in range(N_KV):
        kc_ref[g] = _low_half(kk if g == 0 else _swap_halves(kk)).astype(BF16)
        vc_ref[g] = _low_half(vv if g == 0 else _swap_halves(vv)).astype(BF16)


def _prompt_compress(kvc, pe_row, wbd):
    b, t, _ = kvc.shape
    nc = t // CMP_BLK
    out = jax.ShapeDtypeStruct((b, N_KV, nc, LANES), BF16)
    blk = pl.BlockSpec((None, N_KV, nc, LANES), lambda bi: (bi, 0, 0, 0))
    return pl.pallas_call(
        functools.partial(_prompt_compress_kernel, n_even=nc // 2),
        grid=(b,),
        in_specs=[pl.BlockSpec((None, t, LANES), lambda bi: (bi, 0, 0)),
                  pl.BlockSpec((None, t, LANES), lambda bi: (bi, 0, 1)),
                  pl.BlockSpec(pe_row.shape, lambda bi: (0, 0)),
                  pl.BlockSpec(wbd.shape, lambda bi: (0, 0, 0))],
        out_specs=(blk, blk),
        out_shape=(out, out),
        scratch_shapes=[pltpu.VMEM((nc, KV_LANES), F32)],
        compiler_params=pltpu.CompilerParams(dimension_semantics=("arbitrary",), vmem_limit_bytes=VMEM_LIMIT),
        name="prompt_compress",
    )(kvc, kvc, pe_row, wbd)


def _select_blocks(score, blk_f, n_rounds):
    sel = jnp.zeros(score.shape, F32)
    picks = []
    for _ in range(n_rounds):
        m = jnp.max(score, axis=-1, keepdims=True)
        first = jnp.min(jnp.where(score == m, blk_f, 1e9), axis=-1, keepdims=True)
        hit = blk_f == first
        sel = jnp.where(hit, 1.0, sel)
        score = jnp.where(hit, -jnp.inf, score)
        picks.append(first)
    return sel, picks


def _prompt_attn_kernel(q_ref, kc_ref, vc_ref, ksa_ref, vsa_ref, kw_ref, vw_ref, gate_ref, sz_ref, out_ref,
                        qa_ref, m_ref, l_ref, acc_ref, *, tq, tk, t):
    i = pl.program_id(2)
    qs = i * tq
    rows = GROUP * tq
    nc = t // CMP_BLK
    ns = nc // 2
    q = q_ref[...].reshape(rows, LANES)
    qpos = qs + (_iota((rows, 1), 0) & (tq - 1))

    col = _iota((1, nc), 1)
    cblk = jnp.where(col < ns, 2 * col, 2 * (col - ns) + 1)
    p_cmp = _masked_softmax(_dot_nt(q, kc_ref[...]), (cblk + 1) * CMP_BLK - 1 <= qpos)
    o_cmp = _dot(p_cmp.astype(BF16), vc_ref[...])

    pair = p_cmp[:, 0:ns] + p_cmp[:, ns:nc]
    imp = pair[0:tq]
    for n in range(1, GROUP):
        imp = imp + pair[n * tq:(n + 1) * tq]
    blk = _iota((tq, ns), 1)
    qp = qs + _iota((tq, 1), 0)
    qb = qp >> 6
    forced = (blk == 0) | (blk == qb) | (blk == qb - 1)
    score = jnp.where(blk * SEL_BLK <= qp, imp + jnp.where(forced, FORCE_BONUS, 0.0), -jnp.inf)
    sel, _ = _select_blocks(score, blk.astype(F32), min(N_SEL, ns))
    bias = jnp.where(sel > 0.0, 0.0, NEG_BIAS)

    for c in range(ns // SEL_CHUNK):
        bc = bias[:, (c // 2) * LANES:(c // 2 + 1) * LANES]
        if c % 2 == 0:
            bc = _swap_halves(bc)
        bc = jnp.concatenate([bc] * GROUP, axis=0).astype(BF16)
        qa_ref[c] = _low_half(q, bc)

    m_ref[...] = jnp.full_like(m_ref, NEG_MASK)
    l_ref[...] = jnp.zeros_like(l_ref)
    acc_ref[...] = jnp.zeros_like(acc_ref)

    def kv_step(kt, carry):
        k0 = pl.multiple_of(kt * tk, tk)
        s = _dot_nt(qa_ref[kt // (SEL_CHUNK * SEL_BLK // tk)], ksa_ref[pl.ds(k0, tk), :])
        s = jnp.where(k0 + _iota((1, tk), 1) <= qpos, s, NEG_MASK)
        m_old = m_ref[...]
        m_new = jnp.maximum(m_old, jnp.max(s, axis=-1, keepdims=True))
        alpha = jnp.exp(m_old - m_new)
        p = jnp.exp(s - m_new)
        l_ref[...] = alpha * l_ref[...] + jnp.sum(p, axis=-1, keepdims=True)
        acc_ref[...] = alpha * acc_ref[...] + _dot(p.astype(BF16), vsa_ref[pl.ds(k0, tk), :])
        m_ref[...] = m_new
        return carry

    lax.fori_loop(0, (qs + tq + tk - 1) // tk, kv_step, 0)
    o_slc = acc_ref[...] * (1.0 / l_ref[...])

    wlen = WINDOW + tq
    w0 = pl.multiple_of(jnp.maximum(qs - WINDOW, 0), tq)
    wpos = w0 + _iota((1, wlen), 1)
    p_win = _masked_softmax(_dot_nt(q, kw_ref[pl.ds(w0, wlen), :]), (wpos <= qpos) & (wpos > qpos - WINDOW))
    o_win = _dot(p_win.astype(BF16), vw_ref[pl.ds(w0, wlen), :])

    gate = gate_ref[...]
    mixed = []
    for n in range(GROUP):
        r = slice(n * tq, (n + 1) * tq)
        mixed.append(gate[:, 3 * n:3 * n + 1] * o_cmp[r] + gate[:, 3 * n + 1:3 * n + 2] * o_slc[r]
                     + gate[:, 3 * n + 2:3 * n + 3] * o_win[r])
    for c in range(GROUP // 2):
        o_pair = mixed[2 * c] + _swap_halves(mixed[2 * c + 1])
        out_ref[:, c * LANES:(c + 1) * LANES] = (o_pair * sz_ref[:, c * LANES:(c + 1) * LANES]).astype(BF16)


def _prompt_attn(q, kc, vc, ksa, vsa, kw, vw, gate, sz, tq, tk):
    b, _, _, t, _ = q.shape
    assert t % (SEL_CHUNK * SEL_BLK) == 0 and t >= WINDOW + tq and (SEL_CHUNK * SEL_BLK) % tk == 0
    nc = t // CMP_BLK
    rows = GROUP * tq
    seq = lambda n: pl.BlockSpec((None, None, n, LANES), lambda bi, g, i: (bi, g, 0, 0))
    return pl.pallas_call(
        functools.partial(_prompt_attn_kernel, tq=tq, tk=tk, t=t),
        grid=(b, N_KV, t // tq),
        in_specs=[pl.BlockSpec((None, None, GROUP, tq, LANES), lambda bi, g, i: (bi, g, 0, i, 0)),
                  seq(nc), seq(nc), seq(t), seq(t), seq(t), seq(t),
                  pl.BlockSpec((None, None, tq, LANES), lambda bi, g, i: (bi, g, i, 0)),
                  pl.BlockSpec((None, tq, GROUP * HEAD_DIM), lambda bi, g, i: (bi, i, g))],
        out_specs=pl.BlockSpec((None, tq, GROUP * HEAD_DIM), lambda bi, g, i: (bi, i, g)),
        out_shape=jax.ShapeDtypeStruct((b, t, ATTN_W), BF16),
        scratch_shapes=[pltpu.VMEM((t // (SEL_CHUNK * SEL_BLK), rows, LANES), BF16),
                        pltpu.VMEM((rows, 1), F32), pltpu.VMEM((rows, 1), F32), pltpu.VMEM((rows, LANES), F32)],
        compiler_params=pltpu.CompilerParams(dimension_semantics=("arbitrary", "arbitrary", "arbitrary"),
                                             vmem_limit_bytes=VMEM_LIMIT),
        name="prompt_attention",
    )(q, kc, vc, ksa, vsa, kw, vw, gate, sz)


def _out_proj_kernel(x_ref, convb_ref, attnb_ref, w_ref, y_ref):
    mix = jnp.concatenate([convb_ref[...], attnb_ref[...]], axis=1)
    y_ref[...] = x_ref[...] + _dot(mix, w_ref[...])


def _out_proj(x2d, convb, attnb, w_out, tm):
    m = x2d.shape[0]
    blk = lambda w: pl.BlockSpec((tm, w), lambda i: (i, 0))
    return pl.pallas_call(
        _out_proj_kernel,
        grid=(m // tm,),
        in_specs=[blk(D_MODEL), blk(CONV_CH), blk(ATTN_W), pl.BlockSpec(w_out.shape, lambda i: (0, 0))],
        out_specs=blk(D_MODEL),
        out_shape=jax.ShapeDtypeStruct((m, D_MODEL), F32),
        compiler_params=pltpu.CompilerParams(dimension_semantics=("arbitrary",), vmem_limit_bytes=VMEM_LIMIT),
        name="out_proj",
    )(x2d, convb, attnb, w_out)


def _sample_in_kernel(x_ref, ng_ref, w_ref, cw_ref, cb_ref, qg_ref, kg_ref, cos_ref, sin_ref, c0_ref, c1_ref,
                      convb_ref, sz_ref, qbd_ref, kvn_ref, gate_ref, u_ref):
    h = _normed_input(x_ref[...], ng_ref[...])
    cos2 = cos_ref[...]
    sin2 = sin_ref[...]
    pc = _dot(h, w_ref[:, C_CONV:C_Q])
    b_gate = pc[:, 0:CONV_CH]
    u = pc[:, CONV_CH:2 * CONV_CH] * pc[:, 2 * CONV_CH:3 * CONV_CH]
    z_conv = pc[:, 3 * CONV_CH:4 * CONV_CH]
    conv_y = cw_ref[0:1, :] * c0_ref[...] + cw_ref[1:2, :] * c1_ref[...] + cw_ref[2:3, :] * u + cb_ref[...]
    convb_ref[...] = (b_gate * conv_y * _silu(z_conv)).astype(BF16)
    u_ref[...] = u

    pq = _dot(h, w_ref[:, C_Q:C_KV])
    zeros = jnp.zeros((x_ref.shape[0], LANES), BF16)
    for c in range(N_HEADS // 2):
        qc = _head_norm_rope(pq[:, c * LANES:(c + 1) * LANES], qg_ref[...], cos2, sin2) * (HEAD_DIM ** -0.5)
        lane = _iota(qc.shape, 1)
        for half in range(2):
            head = 2 * c + half
            g = head // GROUP
            src = qc if half == g else _swap_halves(qc)
            keep = (lane >= g * HEAD_DIM) & (lane < (g + 1) * HEAD_DIM)
            qbd_ref[head] = jnp.concatenate([jnp.where(keep, src, 0.0).astype(BF16), zeros], axis=1)

    pk = _dot(h, w_ref[:, C_KV:C_Z])
    for br in range(N_BRANCH):
        kvn_ref[:, br * KV_LANES:br * KV_LANES + LANES] = _head_norm_rope(
            pk[:, br * KV_LANES:br * KV_LANES + LANES], kg_ref[br:br + 1, :], cos2, sin2)
        kvn_ref[:, br * KV_LANES + LANES:(br + 1) * KV_LANES] = pk[:, br * KV_LANES + LANES:(br + 1) * KV_LANES]

    sz_ref[...] = _silu(_dot(h, w_ref[:, C_Z:C_G]))
    gate_ref[...] = jax.nn.sigmoid(_dot(h, w_ref[:, C_G:W_COLS]))


def _sample_in(x, ng, w_all, cw, cb, qg, kg, cos1, sin1, c0, c1):
    nb = x.shape[0]
    out_shape = (
        jax.ShapeDtypeStruct((nb, CONV_CH), BF16),
        jax.ShapeDtypeStruct((nb, ATTN_W), F32),
        jax.ShapeDtypeStruct((N_HEADS, nb, 2 * LANES), BF16),
        jax.ShapeDtypeStruct((nb, N_BRANCH * KV_LANES), F32),
        jax.ShapeDtypeStruct((nb, LANES), F32),
        jax.ShapeDtypeStruct((nb, CONV_CH), F32),
    )
    return pl.pallas_call(
        _sample_in_kernel,
        out_shape=out_shape,
        compiler_params=pltpu.CompilerParams(vmem_limit_bytes=VMEM_LIMIT),
        name="sample_in_proj",
    )(x, ng, w_all, cw, cb, qg, kg, cos1, sin1, c0, c1)


def _sample_cmp_kernel(pt_ref, qbd_ref, cache_ref, pe_ref, wbd_ref, ocmp_ref, idx_ref, bufk_ref, bufv_ref, acc_ref,
                       sem, *, n_pages, past):
    b = pl.program_id(0)

    def page_copies(p):
        page = pt_ref[b, p]
        dst = pl.ds(p * PAGE, PAGE)
        return (pltpu.make_async_copy(cache_ref.at[page, :, pl.ds(0, LANES)], bufk_ref.at[dst, :], sem.at[0]),
                pltpu.make_async_copy(cache_ref.at[page, :, pl.ds(LANES, LANES)], bufv_ref.at[dst, :], sem.at[1]))

    def start(p, carry):
        for copy in page_copies(p):
            copy.start()
        return carry

    def wait(p, carry):
        for copy in page_copies(p):
            copy.wait()
        return carry

    lax.fori_loop(0, n_pages, start, 0)
    lax.fori_loop(0, n_pages, wait, 0)

    nc = past // CMP_BLK
    ns = nc // 2
    _compress_rows(bufk_ref, bufv_ref, pe_ref, wbd_ref, acc_ref, ns)
    kv = acc_ref[...].astype(BF16)
    qbd = qbd_ref[...]
    col = _iota((1, nc), 1)
    cblk = jnp.where(col < ns, 2 * col, 2 * (col - ns) + 1)
    p_cmp = _masked_softmax(_dot_nt(qbd, kv), (cblk + 1) * CMP_BLK - 1 <= past)
    ocmp_ref[...] = _dot(p_cmp.astype(BF16), kv)

    pair = p_cmp[:, 0:ns] + p_cmp[:, ns:nc]
    row = _iota((N_HEADS, ns), 0)
    imp = jnp.where(row == 0, jnp.sum(jnp.where(row < GROUP, pair, 0.0), axis=0, keepdims=True),
                    jnp.sum(jnp.where(row >= GROUP, pair, 0.0), axis=0, keepdims=True))
    blk = _iota((N_HEADS, ns), 1)
    qb = past // SEL_BLK
    forced = (blk == 0) | (blk == qb) | (blk == qb - 1)
    score = jnp.where((blk * SEL_BLK <= past) & (row < N_KV), imp + jnp.where(forced, FORCE_BONUS, 0.0), -jnp.inf)
    _, picks = _select_blocks(score, blk.astype(F32), N_SEL - 1)
    lane = _iota((N_HEADS, LANES), 1)
    idx = jnp.zeros((N_HEADS, LANES), F32)
    for k, pick in enumerate(picks):
        idx = jnp.where(lane == k, pick, idx)
    idx_ref[...] = idx.astype(jnp.int32)


def _sample_cmp(page_table, qbd, cache, pe_row, wbd, past):
    nb, n_pages = page_table.shape
    nc = past // CMP_BLK
    grid_spec = pltpu.PrefetchScalarGridSpec(
        num_scalar_prefetch=1,
        grid=(nb,),
        in_specs=[pl.BlockSpec((None, N_HEADS, 2 * LANES), lambda b, pt: (b, 0, 0)),
                  pl.BlockSpec(memory_space=pl.ANY),
                  pl.BlockSpec(pe_row.shape, lambda b, pt: (0, 0)),
                  pl.BlockSpec(wbd.shape, lambda b, pt: (0, 0, 0))],
        out_specs=(pl.BlockSpec((None, N_HEADS, KV_LANES), lambda b, pt: (b, 0, 0)),
                   pl.BlockSpec((None, N_HEADS, LANES), lambda b, pt: (b, 0, 0))),
        scratch_shapes=[pltpu.VMEM((past, LANES), F32), pltpu.VMEM((past, LANES), F32),
                        pltpu.VMEM((nc, KV_LANES), F32), pltpu.SemaphoreType.DMA((2,))],
    )
    return pl.pallas_call(
        functools.partial(_sample_cmp_kernel, n_pages=n_pages, past=past),
        grid_spec=grid_spec,
        out_shape=(jax.ShapeDtypeStruct((nb, N_HEADS, KV_LANES), F32),
                   jax.ShapeDtypeStruct((nb, N_HEADS, LANES), jnp.int32)),
        compiler_params=pltpu.CompilerParams(dimension_semantics=("arbitrary",), vmem_limit_bytes=VMEM_LIMIT),
        name="sample_compress_select",
    )(page_table, qbd, cache, pe_row, wbd)


def _sample_attn_kernel(pt_ref, sel_ref, qbd_ref, cache_ref, kvn_ref, win_ref, ocmp_ref, gate_ref, out_ref,
                        buf_ref, sem, *, past):
    b = pl.program_id(0)
    n_pick = N_SEL - 1

    def block_copy(j):
        blk = sel_ref[b, j]
        src = cache_ref.at[pt_ref[b, blk // (PAGE // SEL_BLK)], pl.ds((blk % (PAGE // SEL_BLK)) * SEL_BLK, SEL_BLK), :]
        return pltpu.make_async_copy(src, buf_ref.at[j], sem.at[0])

    for j in range(N_KV * n_pick):
        block_copy(j).start()

    qbd = qbd_ref[...]
    qf = qbd.astype(F32)
    row = _iota((N_HEADS, 1), 0)

    def new_token(offset):
        kv_new = kvn_ref[:, offset:offset + KV_LANES].astype(BF16).astype(F32)
        return jnp.sum(qf * kv_new, axis=-1, keepdims=True), kv_new

    def attend(keys, mask, s_new, kv_new):
        s = _dot_nt(qbd, keys)
        if mask is not None:
            s = jnp.where(mask, s, -jnp.inf)
        m = jnp.maximum(jnp.max(s, axis=-1, keepdims=True), s_new)
        p = jnp.exp(s - m)
        p_new = jnp.exp(s_new - m)
        norm = 1.0 / (jnp.sum(p, axis=-1, keepdims=True) + p_new)
        return (_dot(p.astype(BF16), keys) + p_new.astype(BF16).astype(F32) * kv_new) * norm

    w_buf = win_ref.shape[0]
    s_new, kv_new = new_token(2 * KV_LANES)
    x_win = attend(win_ref[...].astype(BF16), _iota((1, w_buf), 1) > w_buf - WINDOW, s_new, kv_new)

    for j in range(N_KV * n_pick):
        block_copy(j).wait()

    s_new, kv_new = new_token(KV_LANES)
    x_slc = []
    for g in range(N_KV):
        keys = buf_ref[g * n_pick:(g + 1) * n_pick].reshape(n_pick * SEL_BLK, KV_LANES).astype(BF16)
        x_slc.append(attend(keys, None, s_new, kv_new))
    x_slc = jnp.where(row < GROUP, x_slc[0], x_slc[1])

    gate = jnp.broadcast_to(gate_ref[...], (N_HEADS, LANES))
    lane = _iota((N_HEADS, LANES), 1)
    mixed = jnp.zeros((N_HEADS, KV_LANES), F32)
    for br, x_br in enumerate((ocmp_ref[...], x_slc, x_win)):
        g_col = jnp.sum(jnp.where(lane == N_BRANCH * _iota((N_HEADS, LANES), 0) + br, gate, 0.0), axis=-1, keepdims=True)
        mixed = mixed + g_col * x_br
    v_lo = mixed[:, 2 * HEAD_DIM:3 * HEAD_DIM]
    v_hi = mixed[:, 3 * HEAD_DIM:4 * HEAD_DIM]
    out_ref[...] = jnp.where(row < GROUP, v_lo, v_hi)


def _sample_attn(page_table, sel, qbd, cache, kvn, win, ocmp, gate, past):
    nb = page_table.shape[0]
    n_pick = N_SEL - 1
    per_b = lambda *shape: pl.BlockSpec((None,) + shape, lambda b, pt, s: (b,) + (0,) * len(shape))
    grid_spec = pltpu.PrefetchScalarGridSpec(
        num_scalar_prefetch=2,
        grid=(nb,),
        in_specs=[per_b(N_HEADS, 2 * LANES),
                  pl.BlockSpec(memory_space=pl.ANY),
                  per_b(1, N_BRANCH * KV_LANES),
                  per_b(win.shape[1], KV_LANES),
                  per_b(N_HEADS, KV_LANES),
                  per_b(1, LANES)],
        out_specs=per_b(N_HEADS, HEAD_DIM),
        scratch_shapes=[pltpu.VMEM((N_KV * n_pick, SEL_BLK, KV_LANES), F32), pltpu.SemaphoreType.DMA((1,))],
    )
    return pl.pallas_call(
        functools.partial(_sample_attn_kernel, past=past),
        grid_spec=grid_spec,
        out_shape=jax.ShapeDtypeStruct((nb, N_HEADS, HEAD_DIM), F32),
        compiler_params=pltpu.CompilerParams(dimension_semantics=("arbitrary",), vmem_limit_bytes=VMEM_LIMIT),
        name="sample_attention",
    )(page_table, sel, qbd, cache, kvn, win, ocmp, gate)


def _sample_out_kernel(x_ref, convb_ref, attn_ref, sz_ref, w_ref, y_ref):
    mix = jnp.concatenate([convb_ref[...], (attn_ref[...] * sz_ref[...]).astype(BF16)], axis=1)
    y_ref[...] = x_ref[...] + _dot(mix, w_ref[...])


def _sample_out(x, convb, attn, sz, w_out):
    return pl.pallas_call(
        _sample_out_kernel,
        out_shape=jax.ShapeDtypeStruct(x.shape, F32),
        compiler_params=pltpu.CompilerParams(vmem_limit_bytes=VMEM_LIMIT),
        name="sample_out_proj",
    )(x, convb, attn, sz, w_out)


def _rope_tables(pos):
    half = HEAD_DIM // 2
    inv = ROPE_THETA ** (-jnp.arange(half, dtype=F32) / half)
    ang = pos.astype(F32)[:, None] * inv[None, :]
    cos, sin = jnp.cos(ang), jnp.sin(ang)
    return jnp.concatenate([cos, cos, cos, cos], axis=1), jnp.concatenate([-sin, sin, -sin, sin], axis=1)


def _layer_params(norm_g, w_in, conv_w, conv_b, q_gain, k_gain, cmp_pe, cmp_w, w_out):
    c_gates = C_KV + N_BRANCH * KV_LANES
    n_gates = N_HEADS * N_BRANCH
    w_all = jnp.concatenate([w_in[:, :c_gates], w_in[:, c_gates + n_gates:], w_in[:, c_gates:c_gates + n_gates],
                             jnp.zeros((D_MODEL, LANES - n_gates), w_in.dtype)], axis=1).astype(BF16)
    eye_g = jnp.eye(N_KV, dtype=cmp_w.dtype)
    eye_j = jnp.eye(2, dtype=cmp_w.dtype)
    wbd = jnp.einsum('rjde,jk,gh->rjgdkhe', cmp_w, eye_j, eye_g).reshape(CMP_BLK, KV_LANES, KV_LANES).astype(BF16)
    pe_row = jnp.broadcast_to(cmp_pe[:, :, None, :], (CMP_BLK, 2, N_KV, HEAD_DIM)).reshape(CMP_BLK, KV_LANES)
    return dict(
        ng=norm_g.reshape(1, D_MODEL), w_all=w_all, cw=conv_w, cb=conv_b.reshape(1, CONV_CH),
        qg=jnp.tile(q_gain, 2).reshape(1, LANES), kg=jnp.tile(k_gain, (1, 2)),
        pe_row=pe_row, wbd=wbd, w_out=w_out.astype(BF16))


def _kv_rows(a):
    return a.reshape(a.shape[:-1] + (2, N_KV, HEAD_DIM))


def _prompt_layer(x, p):
    b, t, _ = x.shape
    cos2, sin2 = _rope_tables(jnp.arange(t, dtype=jnp.int32))
    (convb, sz, q, kvc, kvs, kvw, ksa, vsa, kw, vw, gate, utail) = _prompt_in(
        x, p["ng"], p["w_all"], p["cw"], p["cb"], p["qg"], p["kg"], cos2, sin2, tm=256)
    kc, vc = _prompt_compress(kvc, p["pe_row"], p["wbd"])
    attnb = _prompt_attn(q, kc, vc, ksa, vsa, kw, vw, gate, sz, tq=128, tk=512)
    y = _out_proj(x.reshape(b * t, D_MODEL), convb.reshape(b * t, CONV_CH), attnb.reshape(b * t, ATTN_W),
                  p["w_out"], tm=512).reshape(b, t, D_MODEL)
    w_keep = min(WINDOW, t)
    return (y, _kv_rows(kvc), _kv_rows(kvs), _kv_rows(kvw[:, t - w_keep:]), utail[:, 8 - (CONV_W - 1):])


def _sample_layer(x, cache_cmp, cache_slc, win_buf, conv_buf, page_table, p):
    nb, t, _ = x.shape
    assert t == 1
    n_pages = page_table.shape[1]
    past = n_pages * PAGE
    assert past % (2 * SEL_BLK) == 0 and past // SEL_BLK >= LANES and past // SEL_BLK >= N_SEL
    n_pool = cache_cmp.shape[0]
    cos1, sin1 = _rope_tables(jnp.full((1,), past, dtype=jnp.int32))
    convb, sz, qbd, kvn, gate, u = _sample_in(
        x.reshape(nb, D_MODEL), p["ng"], p["w_all"], p["cw"], p["cb"], p["qg"], p["kg"], cos1, sin1,
        conv_buf[:, 0], conv_buf[:, 1])
    qbd = jnp.transpose(qbd, (1, 0, 2))
    ocmp, idx = _sample_cmp(page_table, qbd, cache_cmp.reshape(n_pool, PAGE, KV_LANES), p["pe_row"], p["wbd"], past)
    sel = idx[:, :N_KV, :N_SEL - 1].reshape(nb, N_KV * (N_SEL - 1))
    w_buf = win_buf.shape[1]
    attn = _sample_attn(page_table, sel, qbd, cache_slc.reshape(n_pool, PAGE, KV_LANES),
                        kvn.reshape(nb, 1, N_BRANCH * KV_LANES), win_buf.reshape(nb, w_buf, KV_LANES), ocmp,
                        gate.reshape(nb, 1, LANES), past)
    y = _sample_out(x.reshape(nb, D_MODEL), convb, attn.reshape(nb, ATTN_W), sz, p["w_out"]).reshape(nb, 1, D_MODEL)
    kv_new = _kv_rows(kvn.reshape(nb, 1, N_BRANCH, KV_LANES))
    win_new = jnp.concatenate([win_buf, kv_new[:, :, 2]], axis=1)[:, t:]
    conv_new = jnp.concatenate([conv_buf, u.reshape(nb, 1, CONV_CH)], axis=1)[:, t:]
    return y, kv_new[:, :, 0], kv_new[:, :, 1], win_new, conv_new


def kernel(x_prompt, x_sample, cache_cmp_kv, cache_slc_kv, state_win_kv, state_conv, page_table, norm_g, w_in,
           conv_w, conv_b, q_gain, k_gain, cmp_pe, cmp_w, w_out):
    yp, ys = x_prompt, x_sample
    outs = [[] for _ in range(8)]
    for layer in range(norm_g.shape[0]):
        p = _layer_params(norm_g[layer], w_in[layer], conv_w[layer], conv_b[layer], q_gain[layer], k_gain[layer],
                          cmp_pe[layer], cmp_w[layer], w_out[layer])
        yp, *prompt_state = _prompt_layer(yp, p)
        ys, *sample_state = _sample_layer(ys, cache_cmp_kv[layer], cache_slc_kv[layer], state_win_kv[layer],
                                          state_conv[layer], page_table, p)
        for acc, a in zip(outs, prompt_state + sample_state):
            acc.append(a)
    return (yp, ys) + tuple(jnp.stack(a) for a in outs)
```

```python
import functools

import jax
import jax.numpy as jnp
from jax import lax
from jax.experimental import pallas as pl
from jax.experimental.pallas import tpu as pltpu

F32 = jnp.float32
BF16 = jnp.bfloat16

D_MODEL = 1024
CONV_CH = 512
CONV_W = 3
N_HEADS = 8
HEAD_DIM = 64
ATTN_W = N_HEADS * HEAD_DIM
N_KV = 2
GROUP = N_HEADS // N_KV
N_BRANCH = 3
CMP_BLK = 32
SEL_BLK = 64
N_SEL = 16
WINDOW = 512
PAGE = 128
ROPE_THETA = 10000.0
NORM_EPS = 1e-6
FORCE_BONUS = 1e4
KV_LANES = 2 * N_KV * HEAD_DIM
LANES = 128
SEL_CHUNK = 64
NEG_BIAS = -1e9
NEG_MASK = -1e30
Q_SCALE = HEAD_DIM ** -0.5 * 1.4426950408889634

C_CONV = 0
C_Q = 4 * CONV_CH
C_KV = C_Q + ATTN_W
C_Z = C_KV + N_BRANCH * KV_LANES
C_G = C_Z + ATTN_W
W_COLS = C_G + LANES

VMEM_LIMIT = 48 * 1024 * 1024


def _dot(a, b):
    return jnp.dot(a, b, preferred_element_type=F32)


def _dot_nt(a, b):
    return lax.dot_general(a, b, (((1,), (1,)), ((), ())), preferred_element_type=F32)


def _iota(shape, dim):
    return lax.broadcasted_iota(jnp.int32, shape, dim)


def _masked_softmax(s, mask):
    s = jnp.where(mask, s, -jnp.inf)
    m = jnp.max(s, axis=-1, keepdims=True)
    m = jnp.where(m > -jnp.inf, m, 0.0)
    p = jnp.exp2(s - m)
    return p * (1.0 / jnp.maximum(jnp.sum(p, axis=-1, keepdims=True), 1e-30))


def _group_mean_sq(x):
    x2 = x * x
    hi = x2.astype(BF16)
    lo = (x2 - hi.astype(F32)).astype(BF16)
    same = (_iota((LANES, LANES), 0) >> 6) == (_iota((LANES, LANES), 1) >> 6)
    ones = jnp.where(same, 1.0, 0.0).astype(BF16)
    return (_dot(hi, ones) + _dot(lo, ones)) * (1.0 / HEAD_DIM)


def _head_norm_rope(x, gain, cos2, sin2):
    xn = x * lax.rsqrt(_group_mean_sq(x) + NORM_EPS) * gain
    lane = _iota(x.shape, 1)
    swapped = jnp.where((lane & (HEAD_DIM - 1)) < HEAD_DIM // 2,
                        pltpu.roll(xn, LANES - HEAD_DIM // 2, 1), pltpu.roll(xn, HEAD_DIM // 2, 1))
    return xn * cos2 + swapped * sin2


def _low_half(x, other=0.0):
    lane = _iota(x.shape, 1)
    return jnp.where(lane < HEAD_DIM, x, other)


def _swap_halves(x):
    return pltpu.roll(x, HEAD_DIM, 1)


def _normed_input(x, norm_g):
    ms = jnp.mean(x * x, axis=-1, keepdims=True)
    return (x * lax.rsqrt(ms + NORM_EPS) * norm_g).astype(BF16)


def _silu(z):
    return z * jax.nn.sigmoid(z)


def _prompt_in_kernel(x_ref, ng_ref, w_ref, cw_ref, cb_ref, qg_ref, kg_ref, cos_ref, sin_ref,
                      convb_ref, sz_ref, q_ref, kvc_ref, kvs_ref, kvw_ref, ksa_ref, vsa_ref, kw_ref, vw_ref,
                      gate_ref, utail_ref, carry_ref, *, tm):
    ti = pl.program_id(1)
    h = _normed_input(x_ref[...], ng_ref[...])
    cos2 = cos_ref[...]
    sin2 = sin_ref[...]

    pc = _dot(h, w_ref[:, C_CONV:C_Q])
    b_gate = pc[:, 0:CONV_CH]
    u = pc[:, CONV_CH:2 * CONV_CH] * pc[:, 2 * CONV_CH:3 * CONV_CH]
    z_conv = pc[:, 3 * CONV_CH:4 * CONV_CH]

    @pl.when(ti == 0)
    def _():
        carry_ref[...] = jnp.zeros_like(carry_ref)

    prev1 = carry_ref[7:8, :]
    prev2 = carry_ref[6:7, :]
    row = _iota(u.shape, 0)
    u1 = jnp.where(row == 0, prev1, pltpu.roll(u, 1, 0))
    u2 = jnp.where(row == 0, prev2, jnp.where(row == 1, prev1, pltpu.roll(u, 2, 0)))
    conv_y = cw_ref[0:1, :] * u2 + cw_ref[1:2, :] * u1 + cw_ref[2:3, :] * u + cb_ref[...]
    convb_ref[...] = (b_gate * conv_y * _silu(z_conv)).astype(BF16)
    carry_ref[...] = u[tm - 8:tm, :]
    utail_ref[...] = u[tm - 8:tm, :]

    pq = _dot(h, w_ref[:, C_Q:C_KV])
    for c in range(N_HEADS // 2):
        qc = _head_norm_rope(pq[:, c * LANES:(c + 1) * LANES], qg_ref[...], cos2, sin2) * Q_SCALE
        for half in range(2):
            head = 2 * c + half
            src = qc if half == 0 else _swap_halves(qc)
            q_ref[head // GROUP, head % GROUP] = _low_half(src).astype(BF16)

    pk = _dot(h, w_ref[:, C_KV:C_Z])
    lane = _iota((tm, LANES), 1)
    tpos = ti * tm + _iota((tm, LANES), 0)
    onehot = jnp.where(((tpos >> 6) & (SEL_CHUNK - 1)) == lane - HEAD_DIM, 1.0, 0.0)
    ones_col = jnp.where(lane == HEAD_DIM, 1.0, 0.0)
    for br, out_ref in enumerate((kvc_ref, kvs_ref, kvw_ref)):
        kp = _head_norm_rope(pk[:, br * KV_LANES:br * KV_LANES + LANES], kg_ref[br:br + 1, :], cos2, sin2)
        vv = pk[:, br * KV_LANES + LANES:(br + 1) * KV_LANES]
        out_ref[:, 0:LANES] = kp
        out_ref[:, LANES:KV_LANES] = vv
        if br == 1:
            for g in range(N_KV):
                ksa_ref[g] = _low_half(kp if g == 0 else _swap_halves(kp), onehot).astype(BF16)
                vsa_ref[g] = _low_half(vv if g == 0 else _swap_halves(vv), ones_col).astype(BF16)
        if br == 2:
            for g in range(N_KV):
                kw_ref[g] = _low_half(kp if g == 0 else _swap_halves(kp)).astype(BF16)
                vw_ref[g] = _low_half(vv if g == 0 else _swap_halves(vv)).astype(BF16)

    sz_ref[...] = _silu(_dot(h, w_ref[:, C_Z:C_G]))
    sg = jax.nn.sigmoid(_dot(h, w_ref[:, C_G:W_COLS]))
    gate_ref[0] = sg
    gate_ref[1] = pltpu.roll(sg, LANES - GROUP * N_BRANCH, 1)


def _prompt_in(x, ng, w_all, cw, cb, qg, kg, cos2, sin2, tm):
    b, t, _ = x.shape
    grid = (b, t // tm)
    row_blk = lambda w: pl.BlockSpec((None, tm, w), lambda bi, ti: (bi, ti, 0))
    full = lambda a: pl.BlockSpec(a.shape, lambda bi, ti: (0,) * a.ndim)
    head_blk = pl.BlockSpec((None, N_KV, tm, LANES), lambda bi, ti: (bi, 0, ti, 0))
    out_shape = (
        jax.ShapeDtypeStruct((b, t, CONV_CH), BF16),
        jax.ShapeDtypeStruct((b, t, ATTN_W), F32),
        jax.ShapeDtypeStruct((b, N_KV, GROUP, t, LANES), BF16),
        jax.ShapeDtypeStruct((b, t, KV_LANES), F32),
        jax.ShapeDtypeStruct((b, t, KV_LANES), F32),
        jax.ShapeDtypeStruct((b, t, KV_LANES), F32),
        jax.ShapeDtypeStruct((b, N_KV, t, LANES), BF16),
        jax.ShapeDtypeStruct((b, N_KV, t, LANES), BF16),
        jax.ShapeDtypeStruct((b, N_KV, t, LANES), BF16),
        jax.ShapeDtypeStruct((b, N_KV, t, LANES), BF16),
        jax.ShapeDtypeStruct((b, N_KV, t, LANES), F32),
        jax.ShapeDtypeStruct((b, 8, CONV_CH), F32),
    )
    out_specs = (
        row_blk(CONV_CH), row_blk(ATTN_W),
        pl.BlockSpec((None, N_KV, GROUP, tm, LANES), lambda bi, ti: (bi, 0, 0, ti, 0)),
        row_blk(KV_LANES), row_blk(KV_LANES), row_blk(KV_LANES),
        head_blk, head_blk, head_blk, head_blk, head_blk,
        pl.BlockSpec((None, 8, CONV_CH), lambda bi, ti: (bi, 0, 0)),
    )
    tab = pl.BlockSpec((tm, LANES), lambda bi, ti: (ti, 0))
    return pl.pallas_call(
        functools.partial(_prompt_in_kernel, tm=tm),
        grid=grid,
        in_specs=[row_blk(D_MODEL), full(ng), full(w_all), full(cw), full(cb), full(qg), full(kg), tab, tab],
        out_specs=out_specs,
        out_shape=out_shape,
        scratch_shapes=[pltpu.VMEM((8, CONV_CH), F32)],
        compiler_params=pltpu.CompilerParams(dimension_semantics=("arbitrary", "arbitrary"),
                                             vmem_limit_bytes=VMEM_LIMIT),
        name="prompt_in_proj",
    )(x, ng, w_all, cw, cb, qg, kg, cos2, sin2)


def _compress_rows(xk_ref, xv_ref, pe_ref, wbd_ref, acc_ref, n_even):
    acc_ref[...] = jnp.zeros_like(acc_ref)

    def rows(ref, r):
        even = ref[pl.ds(r, n_even, stride=2 * CMP_BLK), :]
        odd = ref[pl.ds(CMP_BLK + r, n_even, stride=2 * CMP_BLK), :]
        return jnp.concatenate([even, odd], axis=0)

    def body(r, carry):
        xr = jnp.concatenate([rows(xk_ref, r), rows(xv_ref, r)], axis=1) + pe_ref[pl.ds(r, 1), :]
        acc_ref[...] += _dot(xr.astype(BF16), wbd_ref[r])
        return carry

    lax.fori_loop(0, CMP_BLK, body, 0)


def _prompt_compress_kernel(xk_ref, xv_ref, pe_ref, wbd_ref, kc_ref, vc_ref, acc_ref, *, n_even):
    _compress_rows(xk_ref, xv_ref, pe_ref, wbd_ref, acc_ref, n_even)
    kk = acc_ref[:, 0:LANES]
    vv = acc_ref[:, LANES:KV_LANES]
    for g in range(N_KV):
        kc_ref[g] = _low_half(kk if g == 0 else _swap_halves(kk)).astype(BF16)
        vc_ref[g] = _low_half(vv if g == 0 else _swap_halves(vv)).astype(BF16)


def _prompt_compress(kvc, pe_row, wbd):
    b, t, _ = kvc.shape
    nc = t // CMP_BLK
    out = jax.ShapeDtypeStruct((b, N_KV, nc, LANES), BF16)
    blk = pl.BlockSpec((None, N_KV, nc, LANES), lambda bi: (bi, 0, 0, 0))
    return pl.pallas_call(
        functools.partial(_prompt_compress_kernel, n_even=nc // 2),
        grid=(b,),
        in_specs=[pl.BlockSpec((None, t, LANES), lambda bi: (bi, 0, 0)),
                  pl.BlockSpec((None, t, LANES), lambda bi: (bi, 0, 1)),
                  pl.BlockSpec(pe_row.shape, lambda bi: (0, 0)),
                  pl.BlockSpec(wbd.shape, lambda bi: (0, 0, 0))],
        out_specs=(blk, blk),
        out_shape=(out, out),
        scratch_shapes=[pltpu.VMEM((nc, KV_LANES), F32)],
        compiler_params=pltpu.CompilerParams(dimension_semantics=("arbitrary",), vmem_limit_bytes=VMEM_LIMIT),
        name="prompt_compress",
    )(kvc, kvc, pe_row, wbd)


def _select_blocks(score, blk_f, n_rounds):
    sel = jnp.zeros(score.shape, F32)
    picks = []
    for _ in range(n_rounds):
        m = jnp.max(score, axis=-1, keepdims=True)
        first = jnp.min(jnp.where(score == m, blk_f, 1e9), axis=-1, keepdims=True)
        hit = blk_f == first
        sel = jnp.where(hit, 1.0, sel)
        score = jnp.where(hit, -jnp.inf, score)
        picks.append(first)
    return sel, picks


def _selection_bias(imp, qs):
    tq, ns = imp.shape
    n_chunks = tq // LANES
    imp_t = jnp.concatenate([imp[r * LANES:(r + 1) * LANES, :].T for r in range(n_chunks)], axis=1)
    blk = _iota((ns, tq), 0)
    qp = qs + _iota((1, tq), 1)
    qb = qp >> 6
    valid = blk * SEL_BLK <= qp
    forced = ((blk == 0) | (blk == qb) | (blk == qb - 1)) & valid
    blk_f = blk.astype(F32)
    sel = jnp.where(forced, 1.0, 0.0)
    score = jnp.where(valid & jnp.logical_not(forced), imp_t, -jnp.inf)
    for _ in range(min(N_SEL, ns) - 3):
        m = jnp.max(score, axis=0, keepdims=True)
        first = jnp.min(jnp.where(score == m, blk_f, 1e9), axis=0, keepdims=True)
        hit = blk_f == first
        sel = jnp.where(hit, 1.0, sel)
        score = jnp.where(hit, -jnp.inf, score)
    bias_t = jnp.where(sel > 0.0, 0.0, NEG_BIAS)
    return jnp.concatenate([bias_t[:, r * LANES:(r + 1) * LANES].T for r in range(n_chunks)], axis=0)


def _prompt_attn_kernel(q_ref, kc_ref, vc_ref, ksa_ref, vsa_ref, kw_ref, vw_ref, gate_ref, sz_ref, out_ref,
                        qa_ref, sa_ref, sb_ref, m_ref, acc_ref, *, tq, tk, t):
    i = pl.program_id(2)
    qs = i * tq
    rows = GROUP * tq
    nc = t // CMP_BLK
    ns = nc // 2
    q = q_ref[...].reshape(rows, LANES)
    qpos = qs + (_iota((rows, 1), 0) & (tq - 1))

    col = _iota((1, nc), 1)
    cblk = jnp.where(col < ns, 2 * col, 2 * (col - ns) + 1)
    p_cmp = _masked_softmax(_dot_nt(q, kc_ref[...]), (cblk + 1) * CMP_BLK - 1 <= qpos)
    o_cmp = _dot(p_cmp.astype(BF16), vc_ref[...])

    pair = p_cmp[:, 0:ns] + p_cmp[:, ns:nc]
    imp = pair[0:tq]
    for n in range(1, GROUP):
        imp = imp + pair[n * tq:(n + 1) * tq]
    bias = _selection_bias(imp, qs)

    for c in range(ns // SEL_CHUNK):
        bc = bias[:, (c // 2) * LANES:(c // 2 + 1) * LANES]
        if c % 2 == 0:
            bc = _swap_halves(bc)
        bc = jnp.concatenate([bc] * GROUP, axis=0).astype(BF16)
        qa_ref[c] = _low_half(q, bc)

    wlen = WINDOW + tq
    w0 = pl.multiple_of(jnp.maximum(qs - WINDOW, 0), tq)
    wpos = w0 + _iota((1, wlen), 1)
    p_win = _masked_softmax(_dot_nt(q, kw_ref[pl.ds(w0, wlen), :]), (wpos <= qpos) & (wpos > qpos - WINDOW))
    o_win = _dot(p_win.astype(BF16), vw_ref[pl.ds(w0, wlen), :])

    m_ref[...] = jnp.full_like(m_ref, NEG_MASK)
    acc_ref[...] = jnp.zeros_like(acc_ref)

    def scores(kt):
        k0 = pl.multiple_of(kt * tk, tk)
        return _dot_nt(qa_ref[kt // (SEL_CHUNK * SEL_BLK // tk)], ksa_ref[pl.ds(k0, tk), :])

    def update(s, kt, causal):
        k0 = pl.multiple_of(kt * tk, tk)
        if causal:
            s = jnp.where(k0 + _iota((1, tk), 1) <= qpos, s, NEG_MASK)
        m_old = m_ref[...]
        m_new = jnp.maximum(m_old, jnp.broadcast_to(jnp.max(s, axis=-1, keepdims=True), m_old.shape))
        p = jnp.exp2(s - jnp.concatenate([m_new] * (tk // LANES), axis=1))
        acc_ref[...] = jnp.exp2(m_old - m_new) * acc_ref[...] + _dot(p.astype(BF16), vsa_ref[pl.ds(k0, tk), :])
        m_ref[...] = m_new

    n_kt = (qs + tq + tk - 1) // tk
    n_pairs = (n_kt - 1) // 2
    sa_ref[...] = scores(0)

    def pair_step(j, carry):
        sb_ref[...] = scores(2 * j + 1)
        update(sa_ref[...], 2 * j, False)
        sa_ref[...] = scores(2 * j + 2)
        update(sb_ref[...], 2 * j + 1, False)
        return carry

    lax.fori_loop(0, n_pairs, pair_step, 0)
    last = n_kt - 1

    @pl.when(n_kt - 2 * n_pairs == 1)
    def _():
        update(sa_ref[...], last, True)

    @pl.when(n_kt - 2 * n_pairs == 2)
    def _():
        sb_ref[...] = scores(last)
        update(sa_ref[...], last - 1, False)
        update(sb_ref[...], last, True)

    acc = acc_ref[...]
    o_slc = _low_half(acc * (1.0 / acc[:, HEAD_DIM:HEAD_DIM + 1]))

    gate = gate_ref[...]
    mixed = []
    for n in range(GROUP):
        r = slice(n * tq, (n + 1) * tq)
        mixed.append(gate[:, 3 * n:3 * n + 1] * o_cmp[r] + gate[:, 3 * n + 1:3 * n + 2] * o_slc[r]
                     + gate[:, 3 * n + 2:3 * n + 3] * o_win[r])
    for c in range(GROUP // 2):
        o_pair = mixed[2 * c] + _swap_halves(mixed[2 * c + 1])
        out_ref[:, c * LANES:(c + 1) * LANES] = (o_pair * sz_ref[:, c * LANES:(c + 1) * LANES]).astype(BF16)


def _prompt_attn(q, kc, vc, ksa, vsa, kw, vw, gate, sz, tq, tk):
    b, _, _, t, _ = q.shape
    assert t % (2 * SEL_CHUNK * SEL_BLK) == 0 and t >= WINDOW + tq and (SEL_CHUNK * SEL_BLK) % tk == 0
    assert tq % LANES == 0 and N_SEL > 3
    nc = t // CMP_BLK
    rows = GROUP * tq
    seq = lambda n: pl.BlockSpec((None, None, n, LANES), lambda bi, g, i: (bi, g, 0, 0))
    return pl.pallas_call(
        functools.partial(_prompt_attn_kernel, tq=tq, tk=tk, t=t),
        grid=(b, N_KV, t // tq),
        in_specs=[pl.BlockSpec((None, None, GROUP, tq, LANES), lambda bi, g, i: (bi, g, 0, i, 0)),
                  seq(nc), seq(nc), seq(t), seq(t), seq(t), seq(t),
                  pl.BlockSpec((None, None, tq, LANES), lambda bi, g, i: (bi, g, i, 0)),
                  pl.BlockSpec((None, tq, GROUP * HEAD_DIM), lambda bi, g, i: (bi, i, g))],
        out_specs=pl.BlockSpec((None, tq, GROUP * HEAD_DIM), lambda bi, g, i: (bi, i, g)),
        out_shape=jax.ShapeDtypeStruct((b, t, ATTN_W), BF16),
        scratch_shapes=[pltpu.VMEM((t // (SEL_CHUNK * SEL_BLK), rows, LANES), BF16),
                        pltpu.VMEM((rows, tk), F32), pltpu.VMEM((rows, tk), F32),
                        pltpu.VMEM((rows, LANES), F32), pltpu.VMEM((rows, LANES), F32)],
        compiler_params=pltpu.CompilerParams(dimension_semantics=("arbitrary", "arbitrary", "arbitrary"),
                                             vmem_limit_bytes=VMEM_LIMIT),
        name="prompt_attention",
    )(q, kc, vc, ksa, vsa, kw, vw, gate, sz)


def _out_proj_kernel(x_ref, convb_ref, attnb_ref, w_ref, y_ref):
    mix = jnp.concatenate([convb_ref[...], attnb_ref[...]], axis=1)
    y_ref[...] = x_ref[...] + _dot(mix, w_ref[...])


def _out_proj(x2d, convb, attnb, w_out, tm):
    m = x2d.shape[0]
    blk = lambda w: pl.BlockSpec((tm, w), lambda i: (i, 0))
    return pl.pallas_call(
        _out_proj_kernel,
        grid=(m // tm,),
        in_specs=[blk(D_MODEL), blk(CONV_CH), blk(ATTN_W), pl.BlockSpec(w_out.shape, lambda i: (0, 0))],
        out_specs=blk(D_MODEL),
        out_shape=jax.ShapeDtypeStruct((m, D_MODEL), F32),
        compiler_params=pltpu.CompilerParams(dimension_semantics=("arbitrary",), vmem_limit_bytes=VMEM_LIMIT),
        name="out_proj",
    )(x2d, convb, attnb, w_out)


def _sample_in_kernel(x_ref, ng_ref, w_ref, cw_ref, cb_ref, qg_ref, kg_ref, cos_ref, sin_ref, c0_ref, c1_ref,
                      convb_ref, sz_ref, qbd_ref, kvn_ref, gate_ref, u_ref):
    h = _normed_input(x_ref[...], ng_ref[...])
    cos2 = cos_ref[...]
    sin2 = sin_ref[...]
    pc = _dot(h, w_ref[:, C_CONV:C_Q])
    b_gate = pc[:, 0:CONV_CH]
    u = pc[:, CONV_CH:2 * CONV_CH] * pc[:, 2 * CONV_CH:3 * CONV_CH]
    z_conv = pc[:, 3 * CONV_CH:4 * CONV_CH]
    conv_y = cw_ref[0:1, :] * c0_ref[...] + cw_ref[1:2, :] * c1_ref[...] + cw_ref[2:3, :] * u + cb_ref[...]
    convb_ref[...] = (b_gate * conv_y * _silu(z_conv)).astype(BF16)
    u_ref[...] = u

    pq = _dot(h, w_ref[:, C_Q:C_KV])
    zeros = jnp.zeros((x_ref.shape[0], LANES), BF16)
    for c in range(N_HEADS // 2):
        qc = _head_norm_rope(pq[:, c * LANES:(c + 1) * LANES], qg_ref[...], cos2, sin2) * Q_SCALE
        lane = _iota(qc.shape, 1)
        for half in range(2):
            head = 2 * c + half
            g = head // GROUP
            src = qc if half == g else _swap_halves(qc)
            keep = (lane >= g * HEAD_DIM) & (lane < (g + 1) * HEAD_DIM)
            qbd_ref[head] = jnp.concatenate([jnp.where(keep, src, 0.0).astype(BF16), zeros], axis=1)

    pk = _dot(h, w_ref[:, C_KV:C_Z])
    for br in range(N_BRANCH):
        kvn_ref[:, br * KV_LANES:br * KV_LANES + LANES] = _head_norm_rope(
            pk[:, br * KV_LANES:br * KV_LANES + LANES], kg_ref[br:br + 1, :], cos2, sin2)
        kvn_ref[:, br * KV_LANES + LANES:(br + 1) * KV_LANES] = pk[:, br * KV_LANES + LANES:(br + 1) * KV_LANES]

    sz_ref[...] = _silu(_dot(h, w_ref[:, C_Z:C_G]))
    gate_ref[...] = jax.nn.sigmoid(_dot(h, w_ref[:, C_G:W_COLS]))


def _sample_in(x, ng, w_all, cw, cb, qg, kg, cos1, sin1, c0, c1):
    nb = x.shape[0]
    out_shape = (
        jax.ShapeDtypeStruct((nb, CONV_CH), BF16),
        jax.ShapeDtypeStruct((nb, ATTN_W), F32),
        jax.ShapeDtypeStruct((N_HEADS, nb, 2 * LANES), BF16),
        jax.ShapeDtypeStruct((nb, N_BRANCH * KV_LANES), F32),
        jax.ShapeDtypeStruct((nb, LANES), F32),
        jax.ShapeDtypeStruct((nb, CONV_CH), F32),
    )
    return pl.pallas_call(
        _sample_in_kernel,
        out_shape=out_shape,
        compiler_params=pltpu.CompilerParams(vmem_limit_bytes=VMEM_LIMIT),
        name="sample_in_proj",
    )(x, ng, w_all, cw, cb, qg, kg, cos1, sin1, c0, c1)


def _sample_cmp_kernel(pt_ref, qbd_ref, cache_ref, pe_ref, wbd_ref, ocmp_ref, idx_ref, bufk_ref, bufv_ref, acc_ref,
                       sem, *, n_pages, past):
    b = pl.program_id(0)

    def page_copies(p):
        page = pt_ref[b, p]
        dst = pl.ds(p * PAGE, PAGE)
        return (pltpu.make_async_copy(cache_ref.at[page, :, pl.ds(0, LANES)], bufk_ref.at[dst, :], sem.at[0]),
                pltpu.make_async_copy(cache_ref.at[page, :, pl.ds(LANES, LANES)], bufv_ref.at[dst, :], sem.at[1]))

    def start(p, carry):
        for copy in page_copies(p):
            copy.start()
        return carry

    def wait(p, carry):
        for copy in page_copies(p):
            copy.wait()
        return carry

    lax.fori_loop(0, n_pages, start, 0)
    lax.fori_loop(0, n_pages, wait, 0)

    nc = past // CMP_BLK
    ns = nc // 2
    _compress_rows(bufk_ref, bufv_ref, pe_ref, wbd_ref, acc_ref, ns)
    kv = acc_ref[...].astype(BF16)
    qbd = qbd_ref[...]
    col = _iota((1, nc), 1)
    cblk = jnp.where(col < ns, 2 * col, 2 * (col - ns) + 1)
    p_cmp = _masked_softmax(_dot_nt(qbd, kv), (cblk + 1) * CMP_BLK - 1 <= past)
    ocmp_ref[...] = _dot(p_cmp.astype(BF16), kv)

    pair = p_cmp[:, 0:ns] + p_cmp[:, ns:nc]
    row = _iota((N_HEADS, ns), 0)
    imp = jnp.where(row == 0, jnp.sum(jnp.where(row < GROUP, pair, 0.0), axis=0, keepdims=True),
                    jnp.sum(jnp.where(row >= GROUP, pair, 0.0), axis=0, keepdims=True))
    blk = _iota((N_HEADS, ns), 1)
    qb = past // SEL_BLK
    forced = (blk == 0) | (blk == qb) | (blk == qb - 1)
    score = jnp.where((blk * SEL_BLK <= past) & (row < N_KV), imp + jnp.where(forced, FORCE_BONUS, 0.0), -jnp.inf)
    _, picks = _select_blocks(score, blk.astype(F32), N_SEL - 1)
    lane = _iota((N_HEADS, LANES), 1)
    idx = jnp.zeros((N_HEADS, LANES), F32)
    for k, pick in enumerate(picks):
        idx = jnp.where(lane == k, pick, idx)
    idx_ref[...] = idx.astype(jnp.int32)


def _sample_cmp(page_table, qbd, cache, pe_row, wbd, past):
    nb, n_pages = page_table.shape
    nc = past // CMP_BLK
    grid_spec = pltpu.PrefetchScalarGridSpec(
        num_scalar_prefetch=1,
        grid=(nb,),
        in_specs=[pl.BlockSpec((None, N_HEADS, 2 * LANES), lambda b, pt: (b, 0, 0)),
                  pl.BlockSpec(memory_space=pl.ANY),
                  pl.BlockSpec(pe_row.shape, lambda b, pt: (0, 0)),
                  pl.BlockSpec(wbd.shape, lambda b, pt: (0, 0, 0))],
        out_specs=(pl.BlockSpec((None, N_HEADS, KV_LANES), lambda b, pt: (b, 0, 0)),
                   pl.BlockSpec((None, N_HEADS, LANES), lambda b, pt: (b, 0, 0))),
        scratch_shapes=[pltpu.VMEM((past, LANES), F32), pltpu.VMEM((past, LANES), F32),
                        pltpu.VMEM((nc, KV_LANES), F32), pltpu.SemaphoreType.DMA((2,))],
    )
    return pl.pallas_call(
        functools.partial(_sample_cmp_kernel, n_pages=n_pages, past=past),
        grid_spec=grid_spec,
        out_shape=(jax.ShapeDtypeStruct((nb, N_HEADS, KV_LANES), F32),
                   jax.ShapeDtypeStruct((nb, N_HEADS, LANES), jnp.int32)),
        compiler_params=pltpu.CompilerParams(dimension_semantics=("arbitrary",), vmem_limit_bytes=VMEM_LIMIT),
        name="sample_compress_select",
    )(page_table, qbd, cache, pe_row, wbd)


def _sample_attn_kernel(pt_ref, sel_ref, qbd_ref, cache_ref, kvn_ref, win_ref, ocmp_ref, gate_ref, out_ref,
                        buf_ref, sem, *, past):
    b = pl.program_id(0)
    n_pick = N_SEL - 1

    def block_copy(j):
        blk = sel_ref[b, j]
        src = cache_ref.at[pt_ref[b, blk // (PAGE // SEL_BLK)], pl.ds((blk % (PAGE // SEL_BLK)) * SEL_BLK, SEL_BLK), :]
        return pltpu.make_async_copy(src, buf_ref.at[j], sem.at[0])

    for j in range(N_KV * n_pick):
        block_copy(j).start()

    qbd = qbd_ref[...]
    qf = qbd.astype(F32)
    row = _iota((N_HEADS, 1), 0)

    def new_token(offset):
        kv_new = kvn_ref[:, offset:offset + KV_LANES].astype(BF16).astype(F32)
        return jnp.sum(qf * kv_new, axis=-1, keepdims=True), kv_new

    def attend(keys, mask, s_new, kv_new):
        s = _dot_nt(qbd, keys)
        if mask is not None:
            s = jnp.where(mask, s, -jnp.inf)
        m = jnp.maximum(jnp.max(s, axis=-1, keepdims=True), s_new)
        p = jnp.exp2(s - m)
        p_new = jnp.exp2(s_new - m)
        norm = 1.0 / (jnp.sum(p, axis=-1, keepdims=True) + p_new)
        return (_dot(p.astype(BF16), keys) + p_new.astype(BF16).astype(F32) * kv_new) * norm

    w_buf = win_ref.shape[0]
    s_new, kv_new = new_token(2 * KV_LANES)
    x_win = attend(win_ref[...].astype(BF16), _iota((1, w_buf), 1) > w_buf - WINDOW, s_new, kv_new)

    for j in range(N_KV * n_pick):
        block_copy(j).wait()

    s_new, kv_new = new_token(KV_LANES)
    x_slc = []
    for g in range(N_KV):
        keys = buf_ref[g * n_pick:(g + 1) * n_pick].reshape(n_pick * SEL_BLK, KV_LANES).astype(BF16)
        x_slc.append(attend(keys, None, s_new, kv_new))
    x_slc = jnp.where(row < GROUP, x_slc[0], x_slc[1])

    gate = jnp.broadcast_to(gate_ref[...], (N_HEADS, LANES))
    lane = _iota((N_HEADS, LANES), 1)
    mixed = jnp.zeros((N_HEADS, KV_LANES), F32)
    for br, x_br in enumerate((ocmp_ref[...], x_slc, x_win)):
        g_col = jnp.sum(jnp.where(lane == N_BRANCH * _iota((N_HEADS, LANES), 0) + br, gate, 0.0), axis=-1, keepdims=True)
        mixed = mixed + g_col * x_br
    v_lo = mixed[:, 2 * HEAD_DIM:3 * HEAD_DIM]
    v_hi = mixed[:, 3 * HEAD_DIM:4 * HEAD_DIM]
    out_ref[...] = jnp.where(row < GROUP, v_lo, v_hi)


def _sample_attn(page_table, sel, qbd, cache, kvn, win, ocmp, gate, past):
    nb = page_table.shape[0]
    n_pick = N_SEL - 1
    per_b = lambda *shape: pl.BlockSpec((None,) + shape, lambda b, pt, s: (b,) + (0,) * len(shape))
    grid_spec = pltpu.PrefetchScalarGridSpec(
        num_scalar_prefetch=2,
        grid=(nb,),
        in_specs=[per_b(N_HEADS, 2 * LANES),
                  pl.BlockSpec(memory_space=pl.ANY),
                  per_b(1, N_BRANCH * KV_LANES),
                  per_b(win.shape[1], KV_LANES),
                  per_b(N_HEADS, KV_LANES),
                  per_b(1, LANES)],
        out_specs=per_b(N_HEADS, HEAD_DIM),
        scratch_shapes=[pltpu.VMEM((N_KV * n_pick, SEL_BLK, KV_LANES), F32), pltpu.SemaphoreType.DMA((1,))],
    )
    return pl.pallas_call(
        functools.partial(_sample_attn_kernel, past=past),
        grid_spec=grid_spec,
        out_shape=jax.ShapeDtypeStruct((nb, N_HEADS, HEAD_DIM), F32),
        compiler_params=pltpu.CompilerParams(dimension_semantics=("arbitrary",), vmem_limit_bytes=VMEM_LIMIT),
        name="sample_attention",
    )(page_table, sel, qbd, cache, kvn, win, ocmp, gate)


def _sample_out_kernel(x_ref, convb_ref, attn_ref, sz_ref, w_ref, y_ref):
    mix = jnp.concatenate([convb_ref[...], (attn_ref[...] * sz_ref[...]).astype(BF16)], axis=1)
    y_ref[...] = x_ref[...] + _dot(mix, w_ref[...])


def _sample_out(x, convb, attn, sz, w_out):
    return pl.pallas_call(
        _sample_out_kernel,
        out_shape=jax.ShapeDtypeStruct(x.shape, F32),
        compiler_params=pltpu.CompilerParams(vmem_limit_bytes=VMEM_LIMIT),
        name="sample_out_proj",
    )(x, convb, attn, sz, w_out)


def _rope_tables(pos):
    half = HEAD_DIM // 2
    inv = ROPE_THETA ** (-jnp.arange(half, dtype=F32) / half)
    ang = pos.astype(F32)[:, None] * inv[None, :]
    cos, sin = jnp.cos(ang), jnp.sin(ang)
    return jnp.concatenate([cos, cos, cos, cos], axis=1), jnp.concatenate([-sin, sin, -sin, sin], axis=1)


def _layer_params(norm_g, w_in, conv_w, conv_b, q_gain, k_gain, cmp_pe, cmp_w, w_out):
    c_gates = C_KV + N_BRANCH * KV_LANES
    n_gates = N_HEADS * N_BRANCH
    w_all = jnp.concatenate([w_in[:, :c_gates], w_in[:, c_gates + n_gates:], w_in[:, c_gates:c_gates + n_gates],
                             jnp.zeros((D_MODEL, LANES - n_gates), w_in.dtype)], axis=1).astype(BF16)
    eye_g = jnp.eye(N_KV, dtype=cmp_w.dtype)
    eye_j = jnp.eye(2, dtype=cmp_w.dtype)
    wbd = jnp.einsum('rjde,jk,gh->rjgdkhe', cmp_w, eye_j, eye_g).reshape(CMP_BLK, KV_LANES, KV_LANES).astype(BF16)
    pe_row = jnp.broadcast_to(cmp_pe[:, :, None, :], (CMP_BLK, 2, N_KV, HEAD_DIM)).reshape(CMP_BLK, KV_LANES)
    return dict(
        ng=norm_g.reshape(1, D_MODEL), w_all=w_all, cw=conv_w, cb=conv_b.reshape(1, CONV_CH),
        qg=jnp.tile(q_gain, 2).reshape(1, LANES), kg=jnp.tile(k_gain, (1, 2)),
        pe_row=pe_row, wbd=wbd, w_out=w_out.astype(BF16))


def _kv_rows(a):
    return a.reshape(a.shape[:-1] + (2, N_KV, HEAD_DIM))


def _prompt_layer(x, p):
    b, t, _ = x.shape
    cos2, sin2 = _rope_tables(jnp.arange(t, dtype=jnp.int32))
    (convb, sz, q, kvc, kvs, kvw, ksa, vsa, kw, vw, gate, utail) = _prompt_in(
        x, p["ng"], p["w_all"], p["cw"], p["cb"], p["qg"], p["kg"], cos2, sin2, tm=256)
    kc, vc = _prompt_compress(kvc, p["pe_row"], p["wbd"])
    attnb = _prompt_attn(q, kc, vc, ksa, vsa, kw, vw, gate, sz, tq=128, tk=512)
    y = _out_proj(x.reshape(b * t, D_MODEL), convb.reshape(b * t, CONV_CH), attnb.reshape(b * t, ATTN_W),
                  p["w_out"], tm=512).reshape(b, t, D_MODEL)
    w_keep = min(WINDOW, t)
    return (y, _kv_rows(kvc), _kv_rows(kvs), _kv_rows(kvw[:, t - w_keep:]), utail[:, 8 - (CONV_W - 1):])


def _sample_layer(x, cache_cmp, cache_slc, win_buf, conv_buf, page_table, p):
    nb, t, _ = x.shape
    assert t == 1
    n_pages = page_table.shape[1]
    past = n_pages * PAGE
    assert past % (2 * SEL_BLK) == 0 and past // SEL_BLK >= LANES and past // SEL_BLK >= N_SEL
    n_pool = cache_cmp.shape[0]
    cos1, sin1 = _rope_tables(jnp.full((1,), past, dtype=jnp.int32))
    convb, sz, qbd, kvn, gate, u = _sample_in(
        x.reshape(nb, D_MODEL), p["ng"], p["w_all"], p["cw"], p["cb"], p["qg"], p["kg"], cos1, sin1,
        conv_buf[:, 0], conv_buf[:, 1])
    qbd = jnp.transpose(qbd, (1, 0, 2))
    ocmp, idx = _sample_cmp(page_table, qbd, cache_cmp.reshape(n_pool, PAGE, KV_LANES), p["pe_row"], p["wbd"], past)
    sel = idx[:, :N_KV, :N_SEL - 1].reshape(nb, N_KV * (N_SEL - 1))
    w_buf = win_buf.shape[1]
    attn = _sample_attn(page_table, sel, qbd, cache_slc.reshape(n_pool, PAGE, KV_LANES),
                        kvn.reshape(nb, 1, N_BRANCH * KV_LANES), win_buf.reshape(nb, w_buf, KV_LANES), ocmp,
                        gate.reshape(nb, 1, LANES), past)
    y = _sample_out(x.reshape(nb, D_MODEL), convb, attn.reshape(nb, ATTN_W), sz, p["w_out"]).reshape(nb, 1, D_MODEL)
    kv_new = _kv_rows(kvn.reshape(nb, 1, N_BRANCH, KV_LANES))
    win_new = jnp.concatenate([win_buf, kv_new[:, :, 2]], axis=1)[:, t:]
    conv_new = jnp.concatenate([conv_buf, u.reshape(nb, 1, CONV_CH)], axis=1)[:, t:]
    return y, kv_new[:, :, 0], kv_new[:, :, 1], win_new, conv_new


def kernel(x_prompt, x_sample, cache_cmp_kv, cache_slc_kv, state_win_kv, state_conv, page_table, norm_g, w_in,
           conv_w, conv_b, q_gain, k_gain, cmp_pe, cmp_w, w_out):
    yp, ys = x_prompt, x_sample
    outs = [[] for _ in range(8)]
    for layer in range(norm_g.shape[0]):
        p = _layer_params(norm_g[layer], w_in[layer], conv_w[layer], conv_b[layer], q_gain[layer], k_gain[layer],
                          cmp_pe[layer], cmp_w[layer], w_out[layer])
        yp, *prompt_state = _prompt_layer(yp, p)
        ys, *sample_state = _sample_layer(ys, cache_cmp_kv[layer], cache_slc_kv[layer], state_win_kv[layer],
                                          state_conv[layer], page_table, p)
        for acc, a in zip(outs, prompt_state + sample_state):
            acc.append(a)
    return (yp, ys) + tuple(jnp.stack(a) for a in outs)
```

```python
import functools

import jax
import jax.numpy as jnp
from jax import lax
from jax.experimental import pallas as pl
from jax.experimental.pallas import tpu as pltpu

F32 = jnp.float32
BF16 = jnp.bfloat16

D_MODEL = 1024
CONV_CH = 512
CONV_W = 3
N_HEADS = 8
HEAD_DIM = 64
ATTN_W = N_HEADS * HEAD_DIM
N_KV = 2
GROUP = N_HEADS // N_KV
N_BRANCH = 3
CMP_BLK = 32
SEL_BLK = 64
N_SEL = 16
WINDOW = 512
PAGE = 128
ROPE_THETA = 10000.0
NORM_EPS = 1e-6
FORCE_BONUS = 1e4
KV_LANES = 2 * N_KV * HEAD_DIM
LANES = 128
SEL_CHUNK = 64
NEG_BIAS = -1e9
NEG_MASK = -1e30
Q_SCALE = HEAD_DIM ** -0.5 * 1.4426950408889634

C_CONV = 0
C_Q = 4 * CONV_CH
C_KV = C_Q + ATTN_W
C_Z = C_KV + N_BRANCH * KV_LANES
C_G = C_Z + ATTN_W
W_COLS = C_G + LANES

VMEM_LIMIT = 48 * 1024 * 1024
PAGES_PER_CHUNK = 32
PAIR_UNROLL = 4


def _dot(a, b):
    return jnp.dot(a, b, preferred_element_type=F32)


def _dot_nt(a, b):
    return lax.dot_general(a, b, (((1,), (1,)), ((), ())), preferred_element_type=F32)


def _iota(shape, dim):
    return lax.broadcasted_iota(jnp.int32, shape, dim)


def _masked_softmax(s, mask):
    s = jnp.where(mask, s, -jnp.inf)
    m = jnp.max(s, axis=-1, keepdims=True)
    m = jnp.where(m > -jnp.inf, m, 0.0)
    p = jnp.exp2(s - m)
    return p * (1.0 / jnp.maximum(jnp.sum(p, axis=-1, keepdims=True), 1e-30))


def _group_mean_sq(x):
    x2 = x * x
    hi = x2.astype(BF16)
    lo = (x2 - hi.astype(F32)).astype(BF16)
    same = (_iota((LANES, LANES), 0) >> 6) == (_iota((LANES, LANES), 1) >> 6)
    ones = jnp.where(same, 1.0, 0.0).astype(BF16)
    return (_dot(hi, ones) + _dot(lo, ones)) * (1.0 / HEAD_DIM)


def _head_norm_rope(x, gain, cos2, sin2):
    xn = x * lax.rsqrt(_group_mean_sq(x) + NORM_EPS) * gain
    lane = _iota(x.shape, 1)
    swapped = jnp.where((lane & (HEAD_DIM - 1)) < HEAD_DIM // 2,
                        pltpu.roll(xn, LANES - HEAD_DIM // 2, 1), pltpu.roll(xn, HEAD_DIM // 2, 1))
    return xn * cos2 + swapped * sin2


def _low_half(x, other=0.0):
    lane = _iota(x.shape, 1)
    return jnp.where(lane < HEAD_DIM, x, other)


def _swap_halves(x):
    return pltpu.roll(x, HEAD_DIM, 1)


def _normed_input(x, norm_g):
    ms = jnp.mean(x * x, axis=-1, keepdims=True)
    return (x * lax.rsqrt(ms + NORM_EPS) * norm_g).astype(BF16)


def _silu(z):
    return z * jax.nn.sigmoid(z)


def _prompt_in_kernel(x_ref, ng_ref, w_ref, cw_ref, cb_ref, qg_ref, kg_ref, cos_ref, sin_ref,
                      convb_ref, sz_ref, q_ref, kvc_ref, kvs_ref, kvw_ref, ksa_ref, vsa_ref, kw_ref, vw_ref,
                      gate_ref, utail_ref, carry_ref, *, tm):
    ti = pl.program_id(1)
    h = _normed_input(x_ref[...], ng_ref[...])
    cos2 = cos_ref[...]
    sin2 = sin_ref[...]

    pc = _dot(h, w_ref[:, C_CONV:C_Q])
    b_gate = pc[:, 0:CONV_CH]
    u = pc[:, CONV_CH:2 * CONV_CH] * pc[:, 2 * CONV_CH:3 * CONV_CH]
    z_conv = pc[:, 3 * CONV_CH:4 * CONV_CH]

    @pl.when(ti == 0)
    def _():
        carry_ref[...] = jnp.zeros_like(carry_ref)

    prev1 = carry_ref[7:8, :]
    prev2 = carry_ref[6:7, :]
    row = _iota(u.shape, 0)
    u1 = jnp.where(row == 0, prev1, pltpu.roll(u, 1, 0))
    u2 = jnp.where(row == 0, prev2, jnp.where(row == 1, prev1, pltpu.roll(u, 2, 0)))
    conv_y = cw_ref[0:1, :] * u2 + cw_ref[1:2, :] * u1 + cw_ref[2:3, :] * u + cb_ref[...]
    convb_ref[...] = (b_gate * conv_y * _silu(z_conv)).astype(BF16)
    carry_ref[...] = u[tm - 8:tm, :]
    utail_ref[...] = u[tm - 8:tm, :]

    pq = _dot(h, w_ref[:, C_Q:C_KV])
    for c in range(N_HEADS // 2):
        qc = _head_norm_rope(pq[:, c * LANES:(c + 1) * LANES], qg_ref[...], cos2, sin2) * Q_SCALE
        for half in range(2):
            head = 2 * c + half
            src = qc if half == 0 else _swap_halves(qc)
            q_ref[head // GROUP, head % GROUP] = _low_half(src).astype(BF16)

    pk = _dot(h, w_ref[:, C_KV:C_Z])
    lane = _iota((tm, LANES), 1)
    tpos = ti * tm + _iota((tm, LANES), 0)
    onehot = jnp.where(((tpos >> 6) & (SEL_CHUNK - 1)) == lane - HEAD_DIM, 1.0, 0.0)
    ones_col = jnp.where(lane == HEAD_DIM, 1.0, 0.0)
    for br, out_ref in enumerate((kvc_ref, kvs_ref, kvw_ref)):
        kp = _head_norm_rope(pk[:, br * KV_LANES:br * KV_LANES + LANES], kg_ref[br:br + 1, :], cos2, sin2)
        vv = pk[:, br * KV_LANES + LANES:(br + 1) * KV_LANES]
        out_ref[:, 0:LANES] = kp
        out_ref[:, LANES:KV_LANES] = vv
        if br == 1:
            for g in range(N_KV):
                ksa_ref[g] = _low_half(kp if g == 0 else _swap_halves(kp), onehot).astype(BF16)
                vsa_ref[g] = _low_half(vv if g == 0 else _swap_halves(vv), ones_col).astype(BF16)
        if br == 2:
            for g in range(N_KV):
                kw_ref[g] = _low_half(kp if g == 0 else _swap_halves(kp)).astype(BF16)
                vw_ref[g] = _low_half(vv if g == 0 else _swap_halves(vv), ones_col).astype(BF16)

    sz_ref[...] = _silu(_dot(h, w_ref[:, C_Z:C_G]))
    sg = jax.nn.sigmoid(_dot(h, w_ref[:, C_G:W_COLS]))
    gate_ref[0] = sg
    gate_ref[1] = pltpu.roll(sg, LANES - GROUP * N_BRANCH, 1)


def _prompt_in(x, ng, w_all, cw, cb, qg, kg, cos2, sin2, tm):
    b, t, _ = x.shape
    grid = (b, t // tm)
    row_blk = lambda w: pl.BlockSpec((None, tm, w), lambda bi, ti: (bi, ti, 0))
    full = lambda a: pl.BlockSpec(a.shape, lambda bi, ti: (0,) * a.ndim)
    head_blk = pl.BlockSpec((None, N_KV, tm, LANES), lambda bi, ti: (bi, 0, ti, 0))
    out_shape = (
        jax.ShapeDtypeStruct((b, t, CONV_CH), BF16),
        jax.ShapeDtypeStruct((b, t, ATTN_W), F32),
        jax.ShapeDtypeStruct((b, N_KV, GROUP, t, LANES), BF16),
        jax.ShapeDtypeStruct((b, t, KV_LANES), F32),
        jax.ShapeDtypeStruct((b, t, KV_LANES), F32),
        jax.ShapeDtypeStruct((b, t, KV_LANES), F32),
        jax.ShapeDtypeStruct((b, N_KV, t, LANES), BF16),
        jax.ShapeDtypeStruct((b, N_KV, t, LANES), BF16),
        jax.ShapeDtypeStruct((b, N_KV, t, LANES), BF16),
        jax.ShapeDtypeStruct((b, N_KV, t, LANES), BF16),
        jax.ShapeDtypeStruct((b, N_KV, t, LANES), F32),
        jax.ShapeDtypeStruct((b, 8, CONV_CH), F32),
    )
    out_specs = (
        row_blk(CONV_CH), row_blk(ATTN_W),
        pl.BlockSpec((None, N_KV, GROUP, tm, LANES), lambda bi, ti: (bi, 0, 0, ti, 0)),
        row_blk(KV_LANES), row_blk(KV_LANES), row_blk(KV_LANES),
        head_blk, head_blk, head_blk, head_blk, head_blk,
        pl.BlockSpec((None, 8, CONV_CH), lambda bi, ti: (bi, 0, 0)),
    )
    tab = pl.BlockSpec((tm, LANES), lambda bi, ti: (ti, 0))
    return pl.pallas_call(
        functools.partial(_prompt_in_kernel, tm=tm),
        grid=grid,
        in_specs=[row_blk(D_MODEL), full(ng), full(w_all), full(cw), full(cb), full(qg), full(kg), tab, tab],
        out_specs=out_specs,
        out_shape=out_shape,
        scratch_shapes=[pltpu.VMEM((8, CONV_CH), F32)],
        compiler_params=pltpu.CompilerParams(dimension_semantics=("arbitrary", "arbitrary"),
                                             vmem_limit_bytes=VMEM_LIMIT),
        name="prompt_in_proj",
    )(x, ng, w_all, cw, cb, qg, kg, cos2, sin2)


def _compress_rows(xk_ref, xv_ref, pe_ref, wbd_ref, acc_ref, n_even):
    acc_ref[...] = jnp.zeros_like(acc_ref)

    def rows(ref, r):
        even = ref[pl.ds(r, n_even, stride=2 * CMP_BLK), :]
        odd = ref[pl.ds(CMP_BLK + r, n_even, stride=2 * CMP_BLK), :]
        return jnp.concatenate([even, odd], axis=0)

    def body(r, carry):
        xr = jnp.concatenate([rows(xk_ref, r), rows(xv_ref, r)], axis=1) + pe_ref[pl.ds(r, 1), :]
        acc_ref[...] += _dot(xr.astype(BF16), wbd_ref[r])
        return carry

    lax.fori_loop(0, CMP_BLK, body, 0)


def _prompt_compress_kernel(xk_ref, xv_ref, pe_ref, wbd_ref, kc_ref, vc_ref, acc_ref, *, n_even):
    _compress_rows(xk_ref, xv_ref, pe_ref, wbd_ref, acc_ref, n_even)
    kk = acc_ref[:, 0:LANES]
    vv = acc_ref[:, LANES:KV_LANES]
    for g in range(N_KV):
        kc_ref[g] = _low_half(kk if g == 0 else _swap_halves(kk)).astype(BF16)
        vc_ref[g] = _low_half(vv if g == 0 else _swap_halves(vv)).astype(BF16)


def _prompt_compress(kvc, pe_row, wbd):
    b, t, _ = kvc.shape
    nc = t // CMP_BLK
    out = jax.ShapeDtypeStruct((b, N_KV, nc, LANES), BF16)
    blk = pl.BlockSpec((None, N_KV, nc, LANES), lambda bi: (bi, 0, 0, 0))
    return pl.pallas_call(
        functools.partial(_prompt_compress_kernel, n_even=nc // 2),
        grid=(b,),
        in_specs=[pl.BlockSpec((None, t, LANES), lambda bi: (bi, 0, 0)),
                  pl.BlockSpec((None, t, LANES), lambda bi: (bi, 0, 1)),
                  pl.BlockSpec(pe_row.shape, lambda bi: (0, 0)),
                  pl.BlockSpec(wbd.shape, lambda bi: (0, 0, 0))],
        out_specs=(blk, blk),
        out_shape=(out, out),
        scratch_shapes=[pltpu.VMEM((nc, KV_LANES), F32)],
        compiler_params=pltpu.CompilerParams(dimension_semantics=("arbitrary",), vmem_limit_bytes=VMEM_LIMIT),
        name="prompt_compress",
    )(kvc, kvc, pe_row, wbd)


def _select_blocks(score, blk_f, n_rounds):
    sel = jnp.zeros(score.shape, F32)
    picks = []
    for _ in range(n_rounds):
        m = jnp.max(score, axis=-1, keepdims=True)
        first = jnp.min(jnp.where(score == m, blk_f, 1e9), axis=-1, keepdims=True)
        hit = blk_f == first
        sel = jnp.where(hit, 1.0, sel)
        score = jnp.where(hit, -jnp.inf, score)
        picks.append(first)
    return sel, picks


def _selection_bias(imp, qs):
    tq, ns = imp.shape
    n_chunks = tq // LANES
    imp_t = jnp.concatenate([imp[r * LANES:(r + 1) * LANES, :].T for r in range(n_chunks)], axis=1)
    blk = _iota((ns, tq), 0)
    qp = qs + _iota((1, tq), 1)
    qb = qp >> 6
    valid = blk * SEL_BLK <= qp
    forced = ((blk == 0) | (blk == qb) | (blk == qb - 1)) & valid
    blk_f = blk.astype(F32)
    sel = jnp.where(forced, 1.0, 0.0)
    score = jnp.where(valid & jnp.logical_not(forced), imp_t, -jnp.inf)
    for _ in range(min(N_SEL, ns) - 3):
        m = jnp.max(score, axis=0, keepdims=True)
        first = jnp.min(jnp.where(score == m, blk_f, 1e9), axis=0, keepdims=True)
        hit = blk_f == first
        sel = jnp.where(hit, 1.0, sel)
        score = jnp.where(hit, -jnp.inf, score)
    bias_t = jnp.where(sel > 0.0, 0.0, NEG_BIAS)
    return jnp.concatenate([bias_t[:, r * LANES:(r + 1) * LANES].T for r in range(n_chunks)], axis=0)


def _prompt_attn_kernel(q_ref, kc_ref, vc_ref, ksa_ref, vsa_ref, kw_ref, vw_ref, gate_ref, sz_ref, out_ref,
                        qa_ref, sa_ref, sb_ref, m_ref, acc_ref, *, tq, tk, t):
    i = pl.program_id(2)
    qs = i * tq
    rows = GROUP * tq
    nc = t // CMP_BLK
    ns = nc // 2
    q = q_ref[...].reshape(rows, LANES)
    qpos = qs + (_iota((rows, 1), 0) & (tq - 1))

    col = _iota((1, nc), 1)
    cblk = jnp.where(col < ns, 2 * col, 2 * (col - ns) + 1)
    s_cmp = jnp.where((cblk + 1) * CMP_BLK - 1 <= qpos, _dot_nt(q, kc_ref[...]), -jnp.inf)
    m_cmp = jnp.max(s_cmp, axis=-1, keepdims=True)
    e_cmp = jnp.exp2(s_cmp - jnp.where(m_cmp > -jnp.inf, m_cmp, 0.0))
    inv_cmp = 1.0 / jnp.maximum(jnp.sum(e_cmp, axis=-1, keepdims=True), 1e-30)
    o_cmp = _dot(e_cmp.astype(BF16), vc_ref[...]) * inv_cmp

    pair = (e_cmp[:, 0:ns] + e_cmp[:, ns:nc]) * inv_cmp
    imp = pair[0:tq]
    for n in range(1, GROUP):
        imp = imp + pair[n * tq:(n + 1) * tq]
    bias = _selection_bias(imp, qs)

    for c in range(ns // SEL_CHUNK):
        bc = bias[:, (c // 2) * LANES:(c // 2 + 1) * LANES]
        if c % 2 == 0:
            bc = _swap_halves(bc)
        bc = jnp.concatenate([bc] * GROUP, axis=0).astype(BF16)
        qa_ref[c] = _low_half(q, bc)

    wlen = WINDOW + tq
    w0 = pl.multiple_of(jnp.maximum(qs - WINDOW, 0), tq)
    wpos = w0 + _iota((1, wlen), 1)
    s_win = jnp.where((wpos <= qpos) & (wpos > qpos - WINDOW), _dot_nt(q, kw_ref[pl.ds(w0, wlen), :]), NEG_MASK)
    e_win = jnp.exp2(s_win - jnp.max(s_win, axis=-1, keepdims=True))
    a_win = _dot(e_win.astype(BF16), vw_ref[pl.ds(w0, wlen), :])
    o_win = _low_half(a_win * (1.0 / a_win[:, HEAD_DIM:HEAD_DIM + 1]))

    m_ref[...] = jnp.full_like(m_ref, NEG_MASK)
    acc_ref[...] = jnp.zeros_like(acc_ref)

    def scores(kt):
        k0 = pl.multiple_of(kt * tk, tk)
        return _dot_nt(qa_ref[kt // (SEL_CHUNK * SEL_BLK // tk)], ksa_ref[pl.ds(k0, tk), :])

    def update(s, kt, causal):
        k0 = pl.multiple_of(kt * tk, tk)
        if causal:
            s = jnp.where(k0 + _iota((1, tk), 1) <= qpos, s, NEG_MASK)
        m_old = m_ref[...]
        m_new = jnp.maximum(m_old, jnp.broadcast_to(jnp.max(s, axis=-1, keepdims=True), m_old.shape))
        p = jnp.exp2(s - jnp.concatenate([m_new] * (tk // LANES), axis=1))
        acc_ref[...] = jnp.exp2(m_old - m_new) * acc_ref[...] + _dot(p.astype(BF16), vsa_ref[pl.ds(k0, tk), :])
        m_ref[...] = m_new

    n_kt = (qs + tq + tk - 1) // tk
    n_pairs = (n_kt - 1) // 2
    sa_ref[...] = scores(0)

    def pair_step(j, carry):
        sb_ref[...] = scores(2 * j + 1)
        update(sa_ref[...], 2 * j, False)
        sa_ref[...] = scores(2 * j + 2)
        update(sb_ref[...], 2 * j + 1, False)
        return carry

    lax.fori_loop(0, n_pairs, pair_step, 0)
    last = n_kt - 1

    @pl.when(n_kt - 2 * n_pairs == 1)
    def _():
        update(sa_ref[...], last, True)

    @pl.when(n_kt - 2 * n_pairs == 2)
    def _():
        sb_ref[...] = scores(last)
        update(sa_ref[...], last - 1, False)
        update(sb_ref[...], last, True)

    acc = acc_ref[...]
    o_slc = _low_half(acc * (1.0 / acc[:, HEAD_DIM:HEAD_DIM + 1]))

    gate = gate_ref[...]
    mixed = []
    for n in range(GROUP):
        r = slice(n * tq, (n + 1) * tq)
        mixed.append(gate[:, 3 * n:3 * n + 1] * o_cmp[r] + gate[:, 3 * n + 1:3 * n + 2] * o_slc[r]
                     + gate[:, 3 * n + 2:3 * n + 3] * o_win[r])
    for c in range(GROUP // 2):
        o_pair = mixed[2 * c] + _swap_halves(mixed[2 * c + 1])
        out_ref[:, c * LANES:(c + 1) * LANES] = (o_pair * sz_ref[:, c * LANES:(c + 1) * LANES]).astype(BF16)


def _prompt_attn(q, kc, vc, ksa, vsa, kw, vw, gate, sz, tq, tk):
    b, _, _, t, _ = q.shape
    assert t % (2 * SEL_CHUNK * SEL_BLK) == 0 and t >= WINDOW + tq and (SEL_CHUNK * SEL_BLK) % tk == 0
    assert tq % LANES == 0 and N_SEL > 3
    nc = t // CMP_BLK
    rows = GROUP * tq
    seq = lambda n: pl.BlockSpec((None, None, n, LANES), lambda bi, g, i: (bi, g, 0, 0))
    return pl.pallas_call(
        functools.partial(_prompt_attn_kernel, tq=tq, tk=tk, t=t),
        grid=(b, N_KV, t // tq),
        in_specs=[pl.BlockSpec((None, None, GROUP, tq, LANES), lambda bi, g, i: (bi, g, 0, i, 0)),
                  seq(nc), seq(nc), seq(t), seq(t), seq(t), seq(t),
                  pl.BlockSpec((None, None, tq, LANES), lambda bi, g, i: (bi, g, i, 0)),
                  pl.BlockSpec((None, tq, GROUP * HEAD_DIM), lambda bi, g, i: (bi, i, g))],
        out_specs=pl.BlockSpec((None, tq, GROUP * HEAD_DIM), lambda bi, g, i: (bi, i, g)),
        out_shape=jax.ShapeDtypeStruct((b, t, ATTN_W), BF16),
        scratch_shapes=[pltpu.VMEM((t // (SEL_CHUNK * SEL_BLK), rows, LANES), BF16),
                        pltpu.VMEM((rows, tk), F32), pltpu.VMEM((rows, tk), F32),
                        pltpu.VMEM((rows, LANES), F32), pltpu.VMEM((rows, LANES), F32)],
        compiler_params=pltpu.CompilerParams(dimension_semantics=("arbitrary", "arbitrary", "arbitrary"),
                                             vmem_limit_bytes=VMEM_LIMIT),
        name="prompt_attention",
    )(q, kc, vc, ksa, vsa, kw, vw, gate, sz)


def _out_proj_kernel(x_ref, convb_ref, attnb_ref, w_ref, y_ref):
    mix = jnp.concatenate([convb_ref[...], attnb_ref[...]], axis=1)
    y_ref[...] = x_ref[...] + _dot(mix, w_ref[...])


def _out_proj(x2d, convb, attnb, w_out, tm):
    m = x2d.shape[0]
    blk = lambda w: pl.BlockSpec((tm, w), lambda i: (i, 0))
    return pl.pallas_call(
        _out_proj_kernel,
        grid=(m // tm,),
        in_specs=[blk(D_MODEL), blk(CONV_CH), blk(ATTN_W), pl.BlockSpec(w_out.shape, lambda i: (0, 0))],
        out_specs=blk(D_MODEL),
        out_shape=jax.ShapeDtypeStruct((m, D_MODEL), F32),
        compiler_params=pltpu.CompilerParams(dimension_semantics=("arbitrary",), vmem_limit_bytes=VMEM_LIMIT),
        name="out_proj",
    )(x2d, convb, attnb, w_out)


def _sample_in_kernel(x_ref, ng_ref, w_ref, cw_ref, cb_ref, qg_ref, kg_ref, cos_ref, sin_ref, c0_ref, c1_ref,
                      convb_ref, sz_ref, qbd_ref, kvn_ref, gate_ref, u_ref):
    h = _normed_input(x_ref[...], ng_ref[...])
    cos2 = cos_ref[...]
    sin2 = sin_ref[...]
    pc = _dot(h, w_ref[:, C_CONV:C_Q])
    b_gate = pc[:, 0:CONV_CH]
    u = pc[:, CONV_CH:2 * CONV_CH] * pc[:, 2 * CONV_CH:3 * CONV_CH]
    z_conv = pc[:, 3 * CONV_CH:4 * CONV_CH]
    conv_y = cw_ref[0:1, :] * c0_ref[...] + cw_ref[1:2, :] * c1_ref[...] + cw_ref[2:3, :] * u + cb_ref[...]
    convb_ref[...] = (b_gate * conv_y * _silu(z_conv)).astype(BF16)
    u_ref[...] = u

    pq = _dot(h, w_ref[:, C_Q:C_KV])
    zeros = jnp.zeros((x_ref.shape[0], LANES), BF16)
    for c in range(N_HEADS // 2):
        qc = _head_norm_rope(pq[:, c * LANES:(c + 1) * LANES], qg_ref[...], cos2, sin2) * Q_SCALE
        lane = _iota(qc.shape, 1)
        for half in range(2):
            head = 2 * c + half
            g = head // GROUP
            src = qc if half == g else _swap_halves(qc)
            keep = (lane >= g * HEAD_DIM) & (lane < (g + 1) * HEAD_DIM)
            qbd_ref[head] = jnp.concatenate([jnp.where(keep, src, 0.0).astype(BF16), zeros], axis=1)

    pk = _dot(h, w_ref[:, C_KV:C_Z])
    for br in range(N_BRANCH):
        kvn_ref[:, br * KV_LANES:br * KV_LANES + LANES] = _head_norm_rope(
            pk[:, br * KV_LANES:br * KV_LANES + LANES], kg_ref[br:br + 1, :], cos2, sin2)
        kvn_ref[:, br * KV_LANES + LANES:(br + 1) * KV_LANES] = pk[:, br * KV_LANES + LANES:(br + 1) * KV_LANES]

    sz_ref[...] = _silu(_dot(h, w_ref[:, C_Z:C_G]))
    gate_ref[...] = jax.nn.sigmoid(_dot(h, w_ref[:, C_G:W_COLS]))


def _sample_in(x, ng, w_all, cw, cb, qg, kg, cos1, sin1, c0, c1):
    nb = x.shape[0]
    out_shape = (
        jax.ShapeDtypeStruct((nb, CONV_CH), BF16),
        jax.ShapeDtypeStruct((nb, ATTN_W), F32),
        jax.ShapeDtypeStruct((N_HEADS, nb, 2 * LANES), BF16),
        jax.ShapeDtypeStruct((nb, N_BRANCH * KV_LANES), F32),
        jax.ShapeDtypeStruct((nb, LANES), F32),
        jax.ShapeDtypeStruct((nb, CONV_CH), F32),
    )
    return pl.pallas_call(
        _sample_in_kernel,
        out_shape=out_shape,
        compiler_params=pltpu.CompilerParams(vmem_limit_bytes=VMEM_LIMIT),
        name="sample_in_proj",
    )(x, ng, w_all, cw, cb, qg, kg, cos1, sin1, c0, c1)


def _sample_cmp_kernel(pt_ref, qbd_ref, cache_ref, pet_ref, perm_ref, wbd_ref, ocmp_ref, imp_ref,
                       buf_ref, rows_ref, acc_ref, sem, *, n_pages, past):
    b = pl.program_id(0)
    n_chunks = n_pages // PAGES_PER_CHUNK
    blocks_per_pair = 2 * PAGE // CMP_BLK

    def page_copy(ch, p, slot):
        return pltpu.make_async_copy(cache_ref.at[pt_ref[b, ch * PAGES_PER_CHUNK + p]], buf_ref.at[slot, p],
                                     sem.at[slot])

    def start_chunk(ch, slot):
        def body(p, carry):
            page_copy(ch, p, slot).start()
            return carry
        lax.fori_loop(0, PAGES_PER_CHUNK, body, 0)

    def wait_chunk(ch, slot):
        def body(p, carry):
            page_copy(ch, p, slot).wait()
            return carry
        lax.fori_loop(0, PAGES_PER_CHUNK, body, 0)

    start_chunk(0, 0)
    for ch in range(n_chunks):
        slot = ch % 2
        if ch + 1 < n_chunks:
            start_chunk(ch + 1, 1 - slot)
        wait_chunk(ch, slot)

        def pairs_body(it, carry):
            for k in range(PAIR_UNROLL):
                pr = it * PAIR_UNROLL + k
                xt = jnp.concatenate([buf_ref[slot, 2 * pr], buf_ref[slot, 2 * pr + 1]], axis=1) + pet_ref[...]
                x_perm = _dot_nt(perm_ref[...], xt.astype(BF16))
                base = pl.multiple_of((ch * (PAGES_PER_CHUNK // 2) + pr) * blocks_per_pair, blocks_per_pair)
                for r in range(CMP_BLK):
                    rows_ref[r, pl.ds(base, blocks_per_pair), :] = (
                        x_perm[r * blocks_per_pair:(r + 1) * blocks_per_pair, :])
            return carry

        lax.fori_loop(0, PAGES_PER_CHUNK // 2 // PAIR_UNROLL, pairs_body, 0)

    acc_ref[...] = jnp.zeros_like(acc_ref)

    def compress_body(it, carry):
        acc = acc_ref[...]
        for k in range(PAIR_UNROLL):
            r = it * PAIR_UNROLL + k
            acc = acc + _dot(rows_ref[r].astype(BF16), wbd_ref[r])
        acc_ref[...] = acc
        return carry

    lax.fori_loop(0, CMP_BLK // PAIR_UNROLL, compress_body, 0)

    nc = past // CMP_BLK
    kv = acc_ref[...].astype(BF16)
    qbd = qbd_ref[...]
    p_cmp = _masked_softmax(_dot_nt(qbd, kv), (_iota((1, nc), 1) + 1) * CMP_BLK - 1 <= past)
    ocmp_ref[...] = _dot(p_cmp.astype(BF16), kv)

    pair = p_cmp + pltpu.roll(p_cmp, nc - 1, 1)
    row = _iota((N_HEADS, nc), 0)
    imp_ref[...] = jnp.where(row == 0, jnp.sum(jnp.where(row < GROUP, pair, 0.0), axis=0, keepdims=True),
                             jnp.sum(jnp.where(row >= GROUP, pair, 0.0), axis=0, keepdims=True))


def _sample_cmp(page_table, qbd, cache_t, pe_t, perm, wbd, past):
    nb, n_pages = page_table.shape
    assert n_pages % PAGES_PER_CHUNK == 0
    nc = past // CMP_BLK
    const = lambda a: pl.BlockSpec(a.shape, lambda b, pt: (0,) * a.ndim)
    grid_spec = pltpu.PrefetchScalarGridSpec(
        num_scalar_prefetch=1,
        grid=(nb,),
        in_specs=[pl.BlockSpec((None, N_HEADS, 2 * LANES), lambda b, pt: (b, 0, 0)),
                  pl.BlockSpec(memory_space=pl.ANY), const(pe_t), const(perm), const(wbd)],
        out_specs=(pl.BlockSpec((None, N_HEADS, KV_LANES), lambda b, pt: (b, 0, 0)),
                   pl.BlockSpec((None, N_HEADS, nc), lambda b, pt: (b, 0, 0))),
        scratch_shapes=[pltpu.VMEM((2, PAGES_PER_CHUNK, KV_LANES, PAGE), F32),
                        pltpu.VMEM((CMP_BLK, nc, KV_LANES), F32),
                        pltpu.VMEM((nc, KV_LANES), F32), pltpu.SemaphoreType.DMA((2,))],
    )
    return pl.pallas_call(
        functools.partial(_sample_cmp_kernel, n_pages=n_pages, past=past),
        grid_spec=grid_spec,
        out_shape=(jax.ShapeDtypeStruct((nb, N_HEADS, KV_LANES), F32),
                   jax.ShapeDtypeStruct((nb, N_HEADS, nc), F32)),
        compiler_params=pltpu.CompilerParams(dimension_semantics=("arbitrary",), vmem_limit_bytes=VMEM_LIMIT),
        name="sample_compress",
    )(page_table, qbd, cache_t, pe_t, perm, wbd)


def _sample_select_kernel(imp_ref, idx_ref, *, past):
    imp = imp_ref[...]
    lane = _iota(imp.shape, 1)
    blk = lane >> 1
    qb = past // SEL_BLK
    forced = (blk == 0) | (blk == qb) | (blk == qb - 1)
    valid = ((lane & 1) == 0) & (blk * SEL_BLK <= past)
    score = jnp.where(valid, imp + jnp.where(forced, FORCE_BONUS, 0.0), -jnp.inf)
    _, picks = _select_blocks(score, lane.astype(F32), N_SEL - 1)
    out_lane = _iota(idx_ref.shape, 1)
    idx = jnp.zeros(idx_ref.shape, F32)
    for k, pick in enumerate(picks):
        idx = jnp.where(out_lane == k, pick * 0.5, idx)
    idx_ref[...] = idx.astype(jnp.int32)


def _sample_select(imp, past):
    return pl.pallas_call(
        functools.partial(_sample_select_kernel, past=past),
        out_shape=jax.ShapeDtypeStruct((imp.shape[0], LANES), jnp.int32),
        compiler_params=pltpu.CompilerParams(vmem_limit_bytes=VMEM_LIMIT),
        name="sample_select",
    )(imp)


def _sample_attn_kernel(pt_ref, sel_ref, qbd_ref, cache_ref, kvn_ref, win_ref, ocmp_ref, gate_ref, out_ref,
                        buf_ref, sem, *, past):
    b = pl.program_id(0)
    n_pick = N_SEL - 1

    blocks_per_page = PAGE // SEL_BLK

    def page_copy(j):
        return pltpu.make_async_copy(cache_ref.at[pt_ref[b, sel_ref[b, j] >> 1]], buf_ref.at[j], sem.at[0])

    for j in range(N_KV * n_pick):
        page_copy(j).start()

    qbd = qbd_ref[...]
    qf = qbd.astype(F32)
    row = _iota((N_HEADS, 1), 0)

    def new_token(offset):
        kv_new = kvn_ref[:, offset:offset + KV_LANES].astype(BF16).astype(F32)
        return jnp.sum(qf * kv_new, axis=-1, keepdims=True), kv_new

    def attend(keys_t, mask, s_new, kv_new):
        s = jnp.where(mask, _dot(qbd, keys_t), -jnp.inf)
        m = jnp.maximum(jnp.max(s, axis=-1, keepdims=True), s_new)
        p = jnp.exp2(s - m)
        p_new = jnp.exp2(s_new - m)
        norm = 1.0 / (jnp.sum(p, axis=-1, keepdims=True) + p_new)
        return (_dot_nt(p.astype(BF16), keys_t) + p_new.astype(BF16).astype(F32) * kv_new) * norm

    w_buf = win_ref.shape[1]
    s_new, kv_new = new_token(2 * KV_LANES)
    x_win = attend(win_ref[...].astype(BF16), _iota((1, w_buf), 1) > w_buf - WINDOW, s_new, kv_new)

    for j in range(N_KV * n_pick):
        page_copy(j).wait()

    s_new, kv_new = new_token(KV_LANES)
    lane = _iota((1, n_pick * PAGE), 1)
    x_slc = []
    for g in range(N_KV):
        slots = range(g * n_pick, (g + 1) * n_pick)
        keys_t = jnp.concatenate([buf_ref[j] for j in slots], axis=1).astype(BF16)
        half = jnp.concatenate([jnp.full((1, PAGE), sel_ref[b, j] & (blocks_per_page - 1), jnp.int32) for j in slots],
                               axis=1)
        x_slc.append(attend(keys_t, ((lane & (PAGE - 1)) >> (SEL_BLK.bit_length() - 1)) == half, s_new, kv_new))
    x_slc = jnp.where(row < GROUP, x_slc[0], x_slc[1])

    gate = jnp.broadcast_to(gate_ref[...], (N_HEADS, LANES))
    lane = _iota((N_HEADS, LANES), 1)
    mixed = jnp.zeros((N_HEADS, KV_LANES), F32)
    for br, x_br in enumerate((ocmp_ref[...], x_slc, x_win)):
        g_col = jnp.sum(jnp.where(lane == N_BRANCH * _iota((N_HEADS, LANES), 0) + br, gate, 0.0), axis=-1, keepdims=True)
        mixed = mixed + g_col * x_br
    v_lo = mixed[:, 2 * HEAD_DIM:3 * HEAD_DIM]
    v_hi = mixed[:, 3 * HEAD_DIM:4 * HEAD_DIM]
    out_ref[...] = jnp.where(row < GROUP, v_lo, v_hi)


def _sample_attn(page_table, sel, qbd, cache, kvn, win, ocmp, gate, past):
    nb = page_table.shape[0]
    n_pick = N_SEL - 1
    per_b = lambda *shape: pl.BlockSpec((None,) + shape, lambda b, pt, s: (b,) + (0,) * len(shape))
    grid_spec = pltpu.PrefetchScalarGridSpec(
        num_scalar_prefetch=2,
        grid=(nb,),
        in_specs=[per_b(N_HEADS, 2 * LANES),
                  pl.BlockSpec(memory_space=pl.ANY),
                  per_b(1, N_BRANCH * KV_LANES),
                  per_b(KV_LANES, win.shape[2]),
                  per_b(N_HEADS, KV_LANES),
                  per_b(1, LANES)],
        out_specs=per_b(N_HEADS, HEAD_DIM),
        scratch_shapes=[pltpu.VMEM((N_KV * n_pick, KV_LANES, PAGE), F32), pltpu.SemaphoreType.DMA((1,))],
    )
    return pl.pallas_call(
        functools.partial(_sample_attn_kernel, past=past),
        grid_spec=grid_spec,
        out_shape=jax.ShapeDtypeStruct((nb, N_HEADS, HEAD_DIM), F32),
        compiler_params=pltpu.CompilerParams(dimension_semantics=("arbitrary",), vmem_limit_bytes=VMEM_LIMIT),
        name="sample_attention",
    )(page_table, sel, qbd, cache, kvn, win, ocmp, gate)


def _sample_out_kernel(x_ref, convb_ref, attn_ref, sz_ref, w_ref, y_ref):
    mix = jnp.concatenate([convb_ref[...], (attn_ref[...] * sz_ref[...]).astype(BF16)], axis=1)
    y_ref[...] = x_ref[...] + _dot(mix, w_ref[...])


def _sample_out(x, convb, attn, sz, w_out):
    return pl.pallas_call(
        _sample_out_kernel,
        out_shape=jax.ShapeDtypeStruct(x.shape, F32),
        compiler_params=pltpu.CompilerParams(vmem_limit_bytes=VMEM_LIMIT),
        name="sample_out_proj",
    )(x, convb, attn, sz, w_out)


def _rope_tables(pos):
    half = HEAD_DIM // 2
    inv = ROPE_THETA ** (-jnp.arange(half, dtype=F32) / half)
    ang = pos.astype(F32)[:, None] * inv[None, :]
    cos, sin = jnp.cos(ang), jnp.sin(ang)
    return jnp.concatenate([cos, cos, cos, cos], axis=1), jnp.concatenate([-sin, sin, -sin, sin], axis=1)


def _layer_params(norm_g, w_in, conv_w, conv_b, q_gain, k_gain, cmp_pe, cmp_w, w_out):
    c_gates = C_KV + N_BRANCH * KV_LANES
    n_gates = N_HEADS * N_BRANCH
    w_all = jnp.concatenate([w_in[:, :c_gates], w_in[:, c_gates + n_gates:], w_in[:, c_gates:c_gates + n_gates],
                             jnp.zeros((D_MODEL, LANES - n_gates), w_in.dtype)], axis=1).astype(BF16)
    eye_g = jnp.eye(N_KV, dtype=cmp_w.dtype)
    eye_j = jnp.eye(2, dtype=cmp_w.dtype)
    wbd = jnp.einsum('rjde,jk,gh->rjgdkhe', cmp_w, eye_j, eye_g).reshape(CMP_BLK, KV_LANES, KV_LANES).astype(BF16)
    pe_row = jnp.broadcast_to(cmp_pe[:, :, None, :], (CMP_BLK, 2, N_KV, HEAD_DIM)).reshape(CMP_BLK, KV_LANES)
    blocks_per_pair = 2 * PAGE // CMP_BLK
    pe_t = jnp.tile(pe_row.T, (1, blocks_per_pair))
    m = jnp.arange(2 * PAGE)
    perm = (m[None, :] == (CMP_BLK * (m % blocks_per_pair) + m // blocks_per_pair)[:, None]).astype(BF16)
    return dict(
        ng=norm_g.reshape(1, D_MODEL), w_all=w_all, cw=conv_w, cb=conv_b.reshape(1, CONV_CH),
        qg=jnp.tile(q_gain, 2).reshape(1, LANES), kg=jnp.tile(k_gain, (1, 2)),
        pe_row=pe_row, pe_t=pe_t, perm=perm, wbd=wbd, w_out=w_out.astype(BF16))


def _kv_rows(a):
    return a.reshape(a.shape[:-1] + (2, N_KV, HEAD_DIM))


def _prompt_layer(x, p):
    b, t, _ = x.shape
    cos2, sin2 = _rope_tables(jnp.arange(t, dtype=jnp.int32))
    (convb, sz, q, kvc, kvs, kvw, ksa, vsa, kw, vw, gate, utail) = _prompt_in(
        x, p["ng"], p["w_all"], p["cw"], p["cb"], p["qg"], p["kg"], cos2, sin2, tm=512)
    kc, vc = _prompt_compress(kvc, p["pe_row"], p["wbd"])
    attnb = _prompt_attn(q, kc, vc, ksa, vsa, kw, vw, gate, sz, tq=128, tk=512)
    y = _out_proj(x.reshape(b * t, D_MODEL), convb.reshape(b * t, CONV_CH), attnb.reshape(b * t, ATTN_W),
                  p["w_out"], tm=512).reshape(b, t, D_MODEL)
    w_keep = min(WINDOW, t)
    return (y, _kv_rows(kvc), _kv_rows(kvs), _kv_rows(kvw[:, t - w_keep:]), utail[:, 8 - (CONV_W - 1):])


def _sample_layer(x, cache_cmp, cache_slc, win_buf, conv_buf, page_table, p):
    nb, t, _ = x.shape
    assert t == 1
    n_pages = page_table.shape[1]
    past = n_pages * PAGE
    assert past % (2 * SEL_BLK) == 0 and past // SEL_BLK >= LANES and past // SEL_BLK >= N_SEL
    n_pool = cache_cmp.shape[0]
    cos1, sin1 = _rope_tables(jnp.full((1,), past, dtype=jnp.int32))
    convb, sz, qbd, kvn, gate, u = _sample_in(
        x.reshape(nb, D_MODEL), p["ng"], p["w_all"], p["cw"], p["cb"], p["qg"], p["kg"], cos1, sin1,
        conv_buf[:, 0], conv_buf[:, 1])
    qbd = jnp.transpose(qbd, (1, 0, 2))
    feature_major = lambda a: jnp.transpose(a, (0, 2, 3, 4, 1)).reshape(a.shape[0], KV_LANES, a.shape[1])
    ocmp, imp = _sample_cmp(page_table, qbd, feature_major(cache_cmp), p["pe_t"], p["perm"], p["wbd"], past)
    idx = _sample_select(imp[:, :N_KV].reshape(nb * N_KV, past // CMP_BLK), past)
    sel = idx[:, :N_SEL - 1].reshape(nb, N_KV * (N_SEL - 1))
    attn = _sample_attn(page_table, sel, qbd, feature_major(cache_slc), kvn.reshape(nb, 1, N_BRANCH * KV_LANES),
                        feature_major(win_buf), ocmp, gate.reshape(nb, 1, LANES), past)
    y = _sample_out(x.reshape(nb, D_MODEL), convb, attn.reshape(nb, ATTN_W), sz, p["w_out"]).reshape(nb, 1, D_MODEL)
    kv_new = _kv_rows(kvn.reshape(nb, 1, N_BRANCH, KV_LANES))
    win_new = jnp.concatenate([win_buf, kv_new[:, :, 2]], axis=1)[:, t:]
    conv_new = jnp.concatenate([conv_buf, u.reshape(nb, 1, CONV_CH)], axis=1)[:, t:]
    return y, kv_new[:, :, 0], kv_new[:, :, 1], win_new, conv_new


def kernel(x_prompt, x_sample, cache_cmp_kv, cache_slc_kv, state_win_kv, state_conv, page_table, norm_g, w_in,
           conv_w, conv_b, q_gain, k_gain, cmp_pe, cmp_w, w_out):
    yp, ys = x_prompt, x_sample
    outs = [[] for _ in range(8)]
    for layer in range(norm_g.shape[0]):
        p = _layer_params(norm_g[layer], w_in[layer], conv_w[layer], conv_b[layer], q_gain[layer], k_gain[layer],
                          cmp_pe[layer], cmp_w[layer], w_out[layer])
        yp, *prompt_state = _prompt_layer(yp, p)
        ys, *sample_state = _sample_layer(ys, cache_cmp_kv[layer], cache_slc_kv[layer], state_win_kv[layer],
                                          state_conv[layer], page_table, p)
        for acc, a in zip(outs, prompt_state + sample_state):
            acc.append(a)
    return (yp, ys) + tuple(jnp.stack(a) for a in outs)
```

```python
import functools

import jax
import jax.numpy as jnp
from jax import lax
from jax.experimental import pallas as pl
from jax.experimental.pallas import tpu as pltpu

F32 = jnp.float32
BF16 = jnp.bfloat16

D_MODEL = 1024
CONV_CH = 512
CONV_W = 3
N_HEADS = 8
HEAD_DIM = 64
ATTN_W = N_HEADS * HEAD_DIM
N_KV = 2
GROUP = N_HEADS // N_KV
N_BRANCH = 3
CMP_BLK = 32
SEL_BLK = 64
N_SEL = 16
WINDOW = 512
PAGE = 128
ROPE_THETA = 10000.0
NORM_EPS = 1e-6
FORCE_BONUS = 1e4
KV_LANES = 2 * N_KV * HEAD_DIM
LANES = 128
SEL_CHUNK = 64
NEG_BIAS = -1e9
NEG_MASK = -1e30
Q_SCALE = HEAD_DIM ** -0.5 * 1.4426950408889634

C_CONV = 0
C_Q = 4 * CONV_CH
C_KV = C_Q + ATTN_W
C_Z = C_KV + N_BRANCH * KV_LANES
C_G = C_Z + ATTN_W
W_COLS = C_G + LANES

VMEM_LIMIT = 48 * 1024 * 1024
PAGES_PER_CHUNK = 32
PAIR_UNROLL = 4


def _dot(a, b):
    return jnp.dot(a, b, preferred_element_type=F32)


def _dot_nt(a, b):
    return lax.dot_general(a, b, (((1,), (1,)), ((), ())), preferred_element_type=F32)


def _iota(shape, dim):
    return lax.broadcasted_iota(jnp.int32, shape, dim)


def _masked_softmax(s, mask):
    s = jnp.where(mask, s, -jnp.inf)
    m = jnp.max(s, axis=-1, keepdims=True)
    m = jnp.where(m > -jnp.inf, m, 0.0)
    p = jnp.exp2(s - m)
    return p * (1.0 / jnp.maximum(jnp.sum(p, axis=-1, keepdims=True), 1e-30))


def _group_mean_sq(x):
    x2 = x * x
    hi = x2.astype(BF16)
    lo = (x2 - hi.astype(F32)).astype(BF16)
    same = (_iota((LANES, LANES), 0) >> 6) == (_iota((LANES, LANES), 1) >> 6)
    ones = jnp.where(same, 1.0, 0.0).astype(BF16)
    return (_dot(hi, ones) + _dot(lo, ones)) * (1.0 / HEAD_DIM)


def _head_norm_rope(x, gain, cos2, sin2):
    xn = x * lax.rsqrt(_group_mean_sq(x) + NORM_EPS) * gain
    lane = _iota(x.shape, 1)
    swapped = jnp.where((lane & (HEAD_DIM - 1)) < HEAD_DIM // 2,
                        pltpu.roll(xn, LANES - HEAD_DIM // 2, 1), pltpu.roll(xn, HEAD_DIM // 2, 1))
    return xn * cos2 + swapped * sin2


def _low_half(x, other=0.0):
    lane = _iota(x.shape, 1)
    return jnp.where(lane < HEAD_DIM, x, other)


def _swap_halves(x):
    return pltpu.roll(x, HEAD_DIM, 1)


def _normed_input(x, norm_g):
    ms = jnp.mean(x * x, axis=-1, keepdims=True)
    return (x * lax.rsqrt(ms + NORM_EPS) * norm_g).astype(BF16)


def _silu(z):
    return z * jax.nn.sigmoid(z)


def _prompt_in_kernel(x_ref, ng_ref, w_ref, cw_ref, cb_ref, qg_ref, kg_ref, cos_ref, sin_ref,
                      convb_ref, sz_ref, q_ref, kvc_ref, kvs_ref, kvw_ref, ksa_ref, vsa_ref, kw_ref, vw_ref,
                      gate_ref, utail_ref, carry_ref, *, tm):
    ti = pl.program_id(1)
    h = _normed_input(x_ref[...], ng_ref[...])
    cos2 = cos_ref[...]
    sin2 = sin_ref[...]

    pc = _dot(h, w_ref[:, C_CONV:C_Q])
    b_gate = pc[:, 0:CONV_CH]
    u = pc[:, CONV_CH:2 * CONV_CH] * pc[:, 2 * CONV_CH:3 * CONV_CH]
    z_conv = pc[:, 3 * CONV_CH:4 * CONV_CH]

    @pl.when(ti == 0)
    def _():
        carry_ref[...] = jnp.zeros_like(carry_ref)

    prev1 = carry_ref[7:8, :]
    prev2 = carry_ref[6:7, :]
    row = _iota(u.shape, 0)
    u1 = jnp.where(row == 0, prev1, pltpu.roll(u, 1, 0))
    u2 = jnp.where(row == 0, prev2, jnp.where(row == 1, prev1, pltpu.roll(u, 2, 0)))
    conv_y = cw_ref[0:1, :] * u2 + cw_ref[1:2, :] * u1 + cw_ref[2:3, :] * u + cb_ref[...]
    convb_ref[...] = (b_gate * conv_y * _silu(z_conv)).astype(BF16)
    carry_ref[...] = u[tm - 8:tm, :]
    utail_ref[...] = u[tm - 8:tm, :]

    pq = _dot(h, w_ref[:, C_Q:C_KV])
    for c in range(N_HEADS // 2):
        qc = _head_norm_rope(pq[:, c * LANES:(c + 1) * LANES], qg_ref[...], cos2, sin2) * Q_SCALE
        for half in range(2):
            head = 2 * c + half
            src = qc if half == 0 else _swap_halves(qc)
            q_ref[head // GROUP, head % GROUP] = _low_half(src).astype(BF16)

    pk = _dot(h, w_ref[:, C_KV:C_Z])
    lane = _iota((tm, LANES), 1)
    tpos = ti * tm + _iota((tm, LANES), 0)
    onehot = jnp.where(((tpos >> 6) & (SEL_CHUNK - 1)) == lane - HEAD_DIM, 1.0, 0.0)
    ones_col = jnp.where(lane == HEAD_DIM, 1.0, 0.0)
    for br, out_ref in enumerate((kvc_ref, kvs_ref, kvw_ref)):
        kp = _head_norm_rope(pk[:, br * KV_LANES:br * KV_LANES + LANES], kg_ref[br:br + 1, :], cos2, sin2)
        vv = pk[:, br * KV_LANES + LANES:(br + 1) * KV_LANES]
        out_ref[:, 0:LANES] = kp
        out_ref[:, LANES:KV_LANES] = vv
        if br == 1:
            for g in range(N_KV):
                ksa_ref[g] = _low_half(kp if g == 0 else _swap_halves(kp), onehot).astype(BF16)
                vsa_ref[g] = _low_half(vv if g == 0 else _swap_halves(vv), ones_col).astype(BF16)
        if br == 2:
            for g in range(N_KV):
                kw_ref[g] = _low_half(kp if g == 0 else _swap_halves(kp)).astype(BF16)
                vw_ref[g] = _low_half(vv if g == 0 else _swap_halves(vv), ones_col).astype(BF16)

    sz_ref[...] = _silu(_dot(h, w_ref[:, C_Z:C_G]))
    sg = jax.nn.sigmoid(_dot(h, w_ref[:, C_G:W_COLS]))
    gate_ref[0] = sg
    gate_ref[1] = pltpu.roll(sg, LANES - GROUP * N_BRANCH, 1)


def _prompt_in(x, ng, w_all, cw, cb, qg, kg, cos2, sin2, tm):
    b, t, _ = x.shape
    grid = (b, t // tm)
    row_blk = lambda w: pl.BlockSpec((None, tm, w), lambda bi, ti: (bi, ti, 0))
    full = lambda a: pl.BlockSpec(a.shape, lambda bi, ti: (0,) * a.ndim)
    head_blk = pl.BlockSpec((None, N_KV, tm, LANES), lambda bi, ti: (bi, 0, ti, 0))
    out_shape = (
        jax.ShapeDtypeStruct((b, t, CONV_CH), BF16),
        jax.ShapeDtypeStruct((b, t, ATTN_W), F32),
        jax.ShapeDtypeStruct((b, N_KV, GROUP, t, LANES), BF16),
        jax.ShapeDtypeStruct((b, t, KV_LANES), F32),
        jax.ShapeDtypeStruct((b, t, KV_LANES), F32),
        jax.ShapeDtypeStruct((b, t, KV_LANES), F32),
        jax.ShapeDtypeStruct((b, N_KV, t, LANES), BF16),
        jax.ShapeDtypeStruct((b, N_KV, t, LANES), BF16),
        jax.ShapeDtypeStruct((b, N_KV, t, LANES), BF16),
        jax.ShapeDtypeStruct((b, N_KV, t, LANES), BF16),
        jax.ShapeDtypeStruct((b, N_KV, t, LANES), F32),
        jax.ShapeDtypeStruct((b, 8, CONV_CH), F32),
    )
    out_specs = (
        row_blk(CONV_CH), row_blk(ATTN_W),
        pl.BlockSpec((None, N_KV, GROUP, tm, LANES), lambda bi, ti: (bi, 0, 0, ti, 0)),
        row_blk(KV_LANES), row_blk(KV_LANES), row_blk(KV_LANES),
        head_blk, head_blk, head_blk, head_blk, head_blk,
        pl.BlockSpec((None, 8, CONV_CH), lambda bi, ti: (bi, 0, 0)),
    )
    tab = pl.BlockSpec((tm, LANES), lambda bi, ti: (ti, 0))
    return pl.pallas_call(
        functools.partial(_prompt_in_kernel, tm=tm),
        grid=grid,
        in_specs=[row_blk(D_MODEL), full(ng), full(w_all), full(cw), full(cb), full(qg), full(kg), tab, tab],
        out_specs=out_specs,
        out_shape=out_shape,
        scratch_shapes=[pltpu.VMEM((8, CONV_CH), F32)],
        compiler_params=pltpu.CompilerParams(dimension_semantics=("arbitrary", "arbitrary"),
                                             vmem_limit_bytes=VMEM_LIMIT),
        name="prompt_in_proj",
    )(x, ng, w_all, cw, cb, qg, kg, cos2, sin2)


def _compress_rows(xk_ref, xv_ref, pe_ref, wbd_ref, acc_ref, n_even):
    acc_ref[...] = jnp.zeros_like(acc_ref)

    def rows(ref, r):
        even = ref[pl.ds(r, n_even, stride=2 * CMP_BLK), :]
        odd = ref[pl.ds(CMP_BLK + r, n_even, stride=2 * CMP_BLK), :]
        return jnp.concatenate([even, odd], axis=0)

    def body(r, carry):
        xr = jnp.concatenate([rows(xk_ref, r), rows(xv_ref, r)], axis=1) + pe_ref[pl.ds(r, 1), :]
        acc_ref[...] += _dot(xr.astype(BF16), wbd_ref[r])
        return carry

    lax.fori_loop(0, CMP_BLK, body, 0)


def _prompt_compress_kernel(xk_ref, xv_ref, pe_ref, wbd_ref, kc_ref, vc_ref, acc_ref, *, n_even):
    _compress_rows(xk_ref, xv_ref, pe_ref, wbd_ref, acc_ref, n_even)
    kk = acc_ref[:, 0:LANES]
    vv = acc_ref[:, LANES:KV_LANES]
    for g in range(N_KV):
        kc_ref[g] = _low_half(kk if g == 0 else _swap_halves(kk)).astype(BF16)
        vc_ref[g] = _low_half(vv if g == 0 else _swap_halves(vv)).astype(BF16)


def _prompt_compress(kvc, pe_row, wbd):
    b, t, _ = kvc.shape
    nc = t // CMP_BLK
    out = jax.ShapeDtypeStruct((b, N_KV, nc, LANES), BF16)
    blk = pl.BlockSpec((None, N_KV, nc, LANES), lambda bi: (bi, 0, 0, 0))
    return pl.pallas_call(
        functools.partial(_prompt_compress_kernel, n_even=nc // 2),
        grid=(b,),
        in_specs=[pl.BlockSpec((None, t, LANES), lambda bi: (bi, 0, 0)),
                  pl.BlockSpec((None, t, LANES), lambda bi: (bi, 0, 1)),
                  pl.BlockSpec(pe_row.shape, lambda bi: (0, 0)),
                  pl.BlockSpec(wbd.shape, lambda bi: (0, 0, 0))],
        out_specs=(blk, blk),
        out_shape=(out, out),
        scratch_shapes=[pltpu.VMEM((nc, KV_LANES), F32)],
        compiler_params=pltpu.CompilerParams(dimension_semantics=("arbitrary",), vmem_limit_bytes=VMEM_LIMIT),
        name="prompt_compress",
    )(kvc, kvc, pe_row, wbd)


def _select_blocks(score, blk_f, n_rounds):
    sel = jnp.zeros(score.shape, F32)
    picks = []
    for _ in range(n_rounds):
        m = jnp.max(score, axis=-1, keepdims=True)
        first = jnp.min(jnp.where(score == m, blk_f, 1e9), axis=-1, keepdims=True)
        hit = blk_f == first
        sel = jnp.where(hit, 1.0, sel)
        score = jnp.where(hit, -jnp.inf, score)
        picks.append(first)
    return sel, picks


def _selection_bias(imp, qs):
    tq, ns = imp.shape
    n_chunks = tq // LANES
    imp_t = jnp.concatenate([imp[r * LANES:(r + 1) * LANES, :].T for r in range(n_chunks)], axis=1)
    blk = _iota((ns, tq), 0)
    qp = qs + _iota((1, tq), 1)
    qb = qp >> 6
    valid = blk * SEL_BLK <= qp
    forced = ((blk == 0) | (blk == qb) | (blk == qb - 1)) & valid
    blk_f = blk.astype(F32)
    sel = jnp.where(forced, 1.0, 0.0)
    score = jnp.where(valid & jnp.logical_not(forced), imp_t, -jnp.inf)
    for _ in range(min(N_SEL, ns) - 3):
        m = jnp.max(score, axis=0, keepdims=True)
        first = jnp.min(jnp.where(score == m, blk_f, 1e9), axis=0, keepdims=True)
        hit = blk_f == first
        sel = jnp.where(hit, 1.0, sel)
        score = jnp.where(hit, -jnp.inf, score)
    bias_t = jnp.where(sel > 0.0, 0.0, NEG_BIAS)
    return jnp.concatenate([bias_t[:, r * LANES:(r + 1) * LANES].T for r in range(n_chunks)], axis=0)


def _prompt_attn_kernel(q_ref, kc_ref, vc_ref, ksa_ref, vsa_ref, kw_ref, vw_ref, gate_ref, sz_ref, out_ref,
                        qa_ref, sa_ref, sb_ref, m_ref, acc_ref, *, tq, tk, t):
    i = pl.program_id(2)
    qs = i * tq
    rows = GROUP * tq
    nc = t // CMP_BLK
    ns = nc // 2
    q = q_ref[...].reshape(rows, LANES)
    qpos = qs + (_iota((rows, 1), 0) & (tq - 1))

    col = _iota((1, nc), 1)
    cblk = jnp.where(col < ns, 2 * col, 2 * (col - ns) + 1)
    s_cmp = jnp.where((cblk + 1) * CMP_BLK - 1 <= qpos, _dot_nt(q, kc_ref[...]), -jnp.inf)
    m_cmp = jnp.max(s_cmp, axis=-1, keepdims=True)
    e_cmp = jnp.exp2(s_cmp - jnp.where(m_cmp > -jnp.inf, m_cmp, 0.0))
    inv_cmp = 1.0 / jnp.maximum(jnp.sum(e_cmp, axis=-1, keepdims=True), 1e-30)
    o_cmp = _dot(e_cmp.astype(BF16), vc_ref[...]) * inv_cmp

    pair = (e_cmp[:, 0:ns] + e_cmp[:, ns:nc]) * inv_cmp
    imp = pair[0:tq]
    for n in range(1, GROUP):
        imp = imp + pair[n * tq:(n + 1) * tq]
    bias = _selection_bias(imp, qs)

    for c in range(ns // SEL_CHUNK):
        bc = bias[:, (c // 2) * LANES:(c // 2 + 1) * LANES]
        if c % 2 == 0:
            bc = _swap_halves(bc)
        bc = jnp.concatenate([bc] * GROUP, axis=0).astype(BF16)
        qa_ref[c] = _low_half(q, bc)

    wlen = WINDOW + tq
    w0 = pl.multiple_of(jnp.maximum(qs - WINDOW, 0), tq)
    wpos = w0 + _iota((1, wlen), 1)
    s_win = jnp.where((wpos <= qpos) & (wpos > qpos - WINDOW), _dot_nt(q, kw_ref[pl.ds(w0, wlen), :]), NEG_MASK)
    e_win = jnp.exp2(s_win - jnp.max(s_win, axis=-1, keepdims=True))
    a_win = _dot(e_win.astype(BF16), vw_ref[pl.ds(w0, wlen), :])
    o_win = _low_half(a_win * (1.0 / a_win[:, HEAD_DIM:HEAD_DIM + 1]))

    m_ref[...] = jnp.full_like(m_ref, NEG_MASK)
    acc_ref[...] = jnp.zeros_like(acc_ref)

    def scores(kt):
        k0 = pl.multiple_of(kt * tk, tk)
        return _dot_nt(qa_ref[kt // (SEL_CHUNK * SEL_BLK // tk)], ksa_ref[pl.ds(k0, tk), :])

    def update(s, kt, causal):
        k0 = pl.multiple_of(kt * tk, tk)
        if causal:
            s = jnp.where(k0 + _iota((1, tk), 1) <= qpos, s, NEG_MASK)
        m_old = m_ref[...]
        m_new = jnp.maximum(m_old, jnp.broadcast_to(jnp.max(s, axis=-1, keepdims=True), m_old.shape))
        p = jnp.exp2(s - jnp.concatenate([m_new] * (tk // LANES), axis=1))
        acc_ref[...] = jnp.exp2(m_old - m_new) * acc_ref[...] + _dot(p.astype(BF16), vsa_ref[pl.ds(k0, tk), :])
        m_ref[...] = m_new

    n_kt = (qs + tq + tk - 1) // tk
    n_pairs = (n_kt - 1) // 2
    sa_ref[...] = scores(0)

    def pair_step(j, carry):
        sb_ref[...] = scores(2 * j + 1)
        update(sa_ref[...], 2 * j, False)
        sa_ref[...] = scores(2 * j + 2)
        update(sb_ref[...], 2 * j + 1, False)
        return carry

    lax.fori_loop(0, n_pairs, pair_step, 0)
    last = n_kt - 1

    @pl.when(n_kt - 2 * n_pairs == 1)
    def _():
        update(sa_ref[...], last, True)

    @pl.when(n_kt - 2 * n_pairs == 2)
    def _():
        sb_ref[...] = scores(last)
        update(sa_ref[...], last - 1, False)
        update(sb_ref[...], last, True)

    acc = acc_ref[...]
    o_slc = _low_half(acc * (1.0 / acc[:, HEAD_DIM:HEAD_DIM + 1]))

    gate = gate_ref[...]
    mixed = []
    for n in range(GROUP):
        r = slice(n * tq, (n + 1) * tq)
        mixed.append(gate[:, 3 * n:3 * n + 1] * o_cmp[r] + gate[:, 3 * n + 1:3 * n + 2] * o_slc[r]
                     + gate[:, 3 * n + 2:3 * n + 3] * o_win[r])
    for c in range(GROUP // 2):
        o_pair = mixed[2 * c] + _swap_halves(mixed[2 * c + 1])
        out_ref[:, c * LANES:(c + 1) * LANES] = (o_pair * sz_ref[:, c * LANES:(c + 1) * LANES]).astype(BF16)


def _prompt_attn(q, kc, vc, ksa, vsa, kw, vw, gate, sz, tq, tk):
    b, _, _, t, _ = q.shape
    assert t % (2 * SEL_CHUNK * SEL_BLK) == 0 and t >= WINDOW + tq and (SEL_CHUNK * SEL_BLK) % tk == 0
    assert tq % LANES == 0 and N_SEL > 3
    nc = t // CMP_BLK
    rows = GROUP * tq
    seq = lambda n: pl.BlockSpec((None, None, n, LANES), lambda bi, g, i: (bi, g, 0, 0))
    return pl.pallas_call(
        functools.partial(_prompt_attn_kernel, tq=tq, tk=tk, t=t),
        grid=(b, N_KV, t // tq),
        in_specs=[pl.BlockSpec((None, None, GROUP, tq, LANES), lambda bi, g, i: (bi, g, 0, i, 0)),
                  seq(nc), seq(nc), seq(t), seq(t), seq(t), seq(t),
                  pl.BlockSpec((None, None, tq, LANES), lambda bi, g, i: (bi, g, i, 0)),
                  pl.BlockSpec((None, tq, GROUP * HEAD_DIM), lambda bi, g, i: (bi, i, g))],
        out_specs=pl.BlockSpec((None, tq, GROUP * HEAD_DIM), lambda bi, g, i: (bi, i, g)),
        out_shape=jax.ShapeDtypeStruct((b, t, ATTN_W), BF16),
        scratch_shapes=[pltpu.VMEM((t // (SEL_CHUNK * SEL_BLK), rows, LANES), BF16),
                        pltpu.VMEM((rows, tk), F32), pltpu.VMEM((rows, tk), F32),
                        pltpu.VMEM((rows, LANES), F32), pltpu.VMEM((rows, LANES), F32)],
        compiler_params=pltpu.CompilerParams(dimension_semantics=("arbitrary", "arbitrary", "arbitrary"),
                                             vmem_limit_bytes=VMEM_LIMIT),
        name="prompt_attention",
    )(q, kc, vc, ksa, vsa, kw, vw, gate, sz)


def _out_proj_kernel(x_ref, convb_ref, attnb_ref, w_ref, y_ref):
    mix = jnp.concatenate([convb_ref[...], attnb_ref[...]], axis=1)
    y_ref[...] = x_ref[...] + _dot(mix, w_ref[...])


def _out_proj(x2d, convb, attnb, w_out, tm):
    m = x2d.shape[0]
    blk = lambda w: pl.BlockSpec((tm, w), lambda i: (i, 0))
    return pl.pallas_call(
        _out_proj_kernel,
        grid=(m // tm,),
        in_specs=[blk(D_MODEL), blk(CONV_CH), blk(ATTN_W), pl.BlockSpec(w_out.shape, lambda i: (0, 0))],
        out_specs=blk(D_MODEL),
        out_shape=jax.ShapeDtypeStruct((m, D_MODEL), F32),
        compiler_params=pltpu.CompilerParams(dimension_semantics=("arbitrary",), vmem_limit_bytes=VMEM_LIMIT),
        name="out_proj",
    )(x2d, convb, attnb, w_out)


def _sample_in_kernel(x_ref, ng_ref, w_ref, cw_ref, cb_ref, qg_ref, kg_ref, cos_ref, sin_ref, c0_ref, c1_ref,
                      convb_ref, sz_ref, qbd_ref, kvn_ref, gate_ref, u_ref):
    h = _normed_input(x_ref[...], ng_ref[...])
    cos2 = cos_ref[...]
    sin2 = sin_ref[...]
    pc = _dot(h, w_ref[:, C_CONV:C_Q])
    b_gate = pc[:, 0:CONV_CH]
    u = pc[:, CONV_CH:2 * CONV_CH] * pc[:, 2 * CONV_CH:3 * CONV_CH]
    z_conv = pc[:, 3 * CONV_CH:4 * CONV_CH]
    conv_y = cw_ref[0:1, :] * c0_ref[...] + cw_ref[1:2, :] * c1_ref[...] + cw_ref[2:3, :] * u + cb_ref[...]
    convb_ref[...] = (b_gate * conv_y * _silu(z_conv)).astype(BF16)
    u_ref[...] = u

    pq = _dot(h, w_ref[:, C_Q:C_KV])
    zeros = jnp.zeros((x_ref.shape[0], LANES), BF16)
    for c in range(N_HEADS // 2):
        qc = _head_norm_rope(pq[:, c * LANES:(c + 1) * LANES], qg_ref[...], cos2, sin2) * Q_SCALE
        lane = _iota(qc.shape, 1)
        for half in range(2):
            head = 2 * c + half
            g = head // GROUP
            src = qc if half == g else _swap_halves(qc)
            keep = (lane >= g * HEAD_DIM) & (lane < (g + 1) * HEAD_DIM)
            qbd_ref[head] = jnp.concatenate([jnp.where(keep, src, 0.0).astype(BF16), zeros], axis=1)

    pk = _dot(h, w_ref[:, C_KV:C_Z])
    for br in range(N_BRANCH):
        kvn_ref[:, br * KV_LANES:br * KV_LANES + LANES] = _head_norm_rope(
            pk[:, br * KV_LANES:br * KV_LANES + LANES], kg_ref[br:br + 1, :], cos2, sin2)
        kvn_ref[:, br * KV_LANES + LANES:(br + 1) * KV_LANES] = pk[:, br * KV_LANES + LANES:(br + 1) * KV_LANES]

    sz_ref[...] = _silu(_dot(h, w_ref[:, C_Z:C_G]))
    gate_ref[...] = jax.nn.sigmoid(_dot(h, w_ref[:, C_G:W_COLS]))


def _sample_in(x, ng, w_all, cw, cb, qg, kg, cos1, sin1, c0, c1):
    nb = x.shape[0]
    out_shape = (
        jax.ShapeDtypeStruct((nb, CONV_CH), BF16),
        jax.ShapeDtypeStruct((nb, ATTN_W), F32),
        jax.ShapeDtypeStruct((N_HEADS, nb, 2 * LANES), BF16),
        jax.ShapeDtypeStruct((nb, N_BRANCH * KV_LANES), F32),
        jax.ShapeDtypeStruct((nb, LANES), F32),
        jax.ShapeDtypeStruct((nb, CONV_CH), F32),
    )
    return pl.pallas_call(
        _sample_in_kernel,
        out_shape=out_shape,
        compiler_params=pltpu.CompilerParams(vmem_limit_bytes=VMEM_LIMIT),
        name="sample_in_proj",
    )(x, ng, w_all, cw, cb, qg, kg, cos1, sin1, c0, c1)


def _sample_cmp_kernel(pt_ref, qbd_ref, cache_ref, pet_ref, perm_ref, wbd_ref, ocmp_ref, imp_ref,
                       buf_ref, rows_ref, acc_ref, sem, *, n_batch, n_pages, past):
    b = pl.program_id(0)
    n_chunks = n_pages // PAGES_PER_CHUNK
    blocks_per_pair = 2 * PAGE // CMP_BLK

    def page_copy(bb, ch, p, slot):
        return pltpu.make_async_copy(cache_ref.at[pt_ref[bb, ch * PAGES_PER_CHUNK + p]], buf_ref.at[slot, p],
                                     sem.at[slot])

    def start_chunk(bb, ch, slot):
        def body(p, carry):
            page_copy(bb, ch, p, slot).start()
            return carry
        lax.fori_loop(0, PAGES_PER_CHUNK, body, 0)

    def wait_chunk(ch, slot):
        def body(p, carry):
            page_copy(b, ch, p, slot).wait()
            return carry
        lax.fori_loop(0, PAGES_PER_CHUNK, body, 0)

    @pl.when(b == 0)
    def _():
        start_chunk(0, 0, 0)

    for ch in range(n_chunks):
        slot = ch % 2
        if ch + 1 < n_chunks:
            start_chunk(b, ch + 1, 1 - slot)
        else:
            @pl.when(b + 1 < n_batch)
            def _():
                start_chunk(b + 1, 0, 1 - slot)
        wait_chunk(ch, slot)

        def pairs_body(it, carry):
            for k in range(PAIR_UNROLL):
                pr = it * PAIR_UNROLL + k
                xt = jnp.concatenate([buf_ref[slot, 2 * pr], buf_ref[slot, 2 * pr + 1]], axis=1) + pet_ref[...]
                x_perm = _dot_nt(perm_ref[...], xt.astype(BF16))
                base = pl.multiple_of((ch * (PAGES_PER_CHUNK // 2) + pr) * blocks_per_pair, blocks_per_pair)
                for r in range(CMP_BLK):
                    rows_ref[r, pl.ds(base, blocks_per_pair), :] = (
                        x_perm[r * blocks_per_pair:(r + 1) * blocks_per_pair, :])
            return carry

        lax.fori_loop(0, PAGES_PER_CHUNK // 2 // PAIR_UNROLL, pairs_body, 0)

    acc_ref[...] = jnp.zeros_like(acc_ref)

    def compress_body(it, carry):
        acc = acc_ref[...]
        for k in range(PAIR_UNROLL):
            r = it * PAIR_UNROLL + k
            acc = acc + _dot(rows_ref[r].astype(BF16), wbd_ref[r])
        acc_ref[...] = acc
        return carry

    lax.fori_loop(0, CMP_BLK // PAIR_UNROLL, compress_body, 0)

    nc = past // CMP_BLK
    kv = acc_ref[...].astype(BF16)
    qbd = qbd_ref[...]
    p_cmp = _masked_softmax(_dot_nt(qbd, kv), (_iota((1, nc), 1) + 1) * CMP_BLK - 1 <= past)
    ocmp_ref[...] = _dot(p_cmp.astype(BF16), kv)

    pair = p_cmp + pltpu.roll(p_cmp, nc - 1, 1)
    row = _iota((N_HEADS, nc), 0)
    imp_ref[...] = jnp.where(row == 0, jnp.sum(jnp.where(row < GROUP, pair, 0.0), axis=0, keepdims=True),
                             jnp.sum(jnp.where(row >= GROUP, pair, 0.0), axis=0, keepdims=True))


def _sample_cmp(page_table, qbd, cache_t, pe_t, perm, wbd, past):
    nb, n_pages = page_table.shape
    assert n_pages % (2 * PAGES_PER_CHUNK) == 0
    nc = past // CMP_BLK
    const = lambda a: pl.BlockSpec(a.shape, lambda b, pt: (0,) * a.ndim)
    grid_spec = pltpu.PrefetchScalarGridSpec(
        num_scalar_prefetch=1,
        grid=(nb,),
        in_specs=[pl.BlockSpec((None, N_HEADS, 2 * LANES), lambda b, pt: (b, 0, 0)),
                  pl.BlockSpec(memory_space=pl.ANY), const(pe_t), const(perm), const(wbd)],
        out_specs=(pl.BlockSpec((None, N_HEADS, KV_LANES), lambda b, pt: (b, 0, 0)),
                   pl.BlockSpec((None, N_HEADS, nc), lambda b, pt: (b, 0, 0))),
        scratch_shapes=[pltpu.VMEM((2, PAGES_PER_CHUNK, KV_LANES, PAGE), F32),
                        pltpu.VMEM((CMP_BLK, nc, KV_LANES), F32),
                        pltpu.VMEM((nc, KV_LANES), F32), pltpu.SemaphoreType.DMA((2,))],
    )
    return pl.pallas_call(
        functools.partial(_sample_cmp_kernel, n_batch=nb, n_pages=n_pages, past=past),
        grid_spec=grid_spec,
        out_shape=(jax.ShapeDtypeStruct((nb, N_HEADS, KV_LANES), F32),
                   jax.ShapeDtypeStruct((nb, N_HEADS, nc), F32)),
        compiler_params=pltpu.CompilerParams(dimension_semantics=("arbitrary",), vmem_limit_bytes=VMEM_LIMIT),
        name="sample_compress",
    )(page_table, qbd, cache_t, pe_t, perm, wbd)


def _sample_select_kernel(imp_ref, idx_ref, *, past):
    imp = imp_ref[...]
    lane = _iota(imp.shape, 1)
    blk = lane >> 1
    qb = past // SEL_BLK
    forced = (blk == 0) | (blk == qb) | (blk == qb - 1)
    valid = ((lane & 1) == 0) & (blk * SEL_BLK <= past)
    score = jnp.where(valid, imp + jnp.where(forced, FORCE_BONUS, 0.0), -jnp.inf)
    _, picks = _select_blocks(score, lane.astype(F32), N_SEL - 1)
    out_lane = _iota(idx_ref.shape, 1)
    idx = jnp.zeros(idx_ref.shape, F32)
    for k, pick in enumerate(picks):
        idx = jnp.where(out_lane == k, pick * 0.5, idx)
    idx_ref[...] = idx.astype(jnp.int32)


def _sample_select(imp, past):
    return pl.pallas_call(
        functools.partial(_sample_select_kernel, past=past),
        out_shape=jax.ShapeDtypeStruct((imp.shape[0], LANES), jnp.int32),
        compiler_params=pltpu.CompilerParams(vmem_limit_bytes=VMEM_LIMIT),
        name="sample_select",
    )(imp)


def _sample_attn_kernel(pt_ref, sel_ref, qbd_ref, cache_ref, kvn_ref, win_ref, ocmp_ref, gate_ref, out_ref,
                        buf_ref, sem, *, n_batch, past):
    b = pl.program_id(0)
    n_pick = N_SEL - 1
    blocks_per_page = PAGE // SEL_BLK
    slot = b % 2

    def page_copy(bb, j, s):
        return pltpu.make_async_copy(cache_ref.at[pt_ref[bb, sel_ref[bb, j] >> 1]], buf_ref.at[s, j], sem.at[s])

    @pl.when(b == 0)
    def _():
        for j in range(N_KV * n_pick):
            page_copy(0, j, 0).start()

    @pl.when(b + 1 < n_batch)
    def _():
        for j in range(N_KV * n_pick):
            page_copy(b + 1, j, 1 - slot).start()

    qbd = qbd_ref[...]
    qf = qbd.astype(F32)
    row = _iota((N_HEADS, 1), 0)

    def new_token(offset):
        kv_new = kvn_ref[:, offset:offset + KV_LANES].astype(BF16).astype(F32)
        return jnp.sum(qf * kv_new, axis=-1, keepdims=True), kv_new

    def attend(keys_t, mask, s_new, kv_new):
        s = jnp.where(mask, _dot(qbd, keys_t), -jnp.inf)
        m = jnp.maximum(jnp.max(s, axis=-1, keepdims=True), s_new)
        p = jnp.exp2(s - m)
        p_new = jnp.exp2(s_new - m)
        norm = 1.0 / (jnp.sum(p, axis=-1, keepdims=True) + p_new)
        return (_dot_nt(p.astype(BF16), keys_t) + p_new.astype(BF16).astype(F32) * kv_new) * norm

    w_buf = win_ref.shape[1]
    s_new, kv_new = new_token(2 * KV_LANES)
    x_win = attend(win_ref[...].astype(BF16), _iota((1, w_buf), 1) > w_buf - WINDOW, s_new, kv_new)

    for j in range(N_KV * n_pick):
        page_copy(b, j, slot).wait()

    s_new, kv_new = new_token(KV_LANES)
    lane = _iota((1, n_pick * PAGE), 1)
    x_slc = []
    for g in range(N_KV):
        slots = range(g * n_pick, (g + 1) * n_pick)
        keys_t = jnp.concatenate([buf_ref[slot, j] for j in slots], axis=1).astype(BF16)
        half = jnp.concatenate([jnp.full((1, PAGE), sel_ref[b, j] & (blocks_per_page - 1), jnp.int32) for j in slots],
                               axis=1)
        x_slc.append(attend(keys_t, ((lane & (PAGE - 1)) >> (SEL_BLK.bit_length() - 1)) == half, s_new, kv_new))
    x_slc = jnp.where(row < GROUP, x_slc[0], x_slc[1])

    gate = jnp.broadcast_to(gate_ref[...], (N_HEADS, LANES))
    lane = _iota((N_HEADS, LANES), 1)
    mixed = jnp.zeros((N_HEADS, KV_LANES), F32)
    for br, x_br in enumerate((ocmp_ref[...], x_slc, x_win)):
        g_col = jnp.sum(jnp.where(lane == N_BRANCH * _iota((N_HEADS, LANES), 0) + br, gate, 0.0), axis=-1, keepdims=True)
        mixed = mixed + g_col * x_br
    v_lo = mixed[:, 2 * HEAD_DIM:3 * HEAD_DIM]
    v_hi = mixed[:, 3 * HEAD_DIM:4 * HEAD_DIM]
    out_ref[...] = jnp.where(row < GROUP, v_lo, v_hi)


def _sample_attn(page_table, sel, qbd, cache, kvn, win, ocmp, gate, past):
    nb = page_table.shape[0]
    n_pick = N_SEL - 1
    per_b = lambda *shape: pl.BlockSpec((None,) + shape, lambda b, pt, s: (b,) + (0,) * len(shape))
    grid_spec = pltpu.PrefetchScalarGridSpec(
        num_scalar_prefetch=2,
        grid=(nb,),
        in_specs=[per_b(N_HEADS, 2 * LANES),
                  pl.BlockSpec(memory_space=pl.ANY),
                  per_b(1, N_BRANCH * KV_LANES),
                  per_b(KV_LANES, win.shape[2]),
                  per_b(N_HEADS, KV_LANES),
                  per_b(1, LANES)],
        out_specs=per_b(N_HEADS, HEAD_DIM),
        scratch_shapes=[pltpu.VMEM((2, N_KV * n_pick, KV_LANES, PAGE), F32), pltpu.SemaphoreType.DMA((2,))],
    )
    return pl.pallas_call(
        functools.partial(_sample_attn_kernel, n_batch=nb, past=past),
        grid_spec=grid_spec,
        out_shape=jax.ShapeDtypeStruct((nb, N_HEADS, HEAD_DIM), F32),
        compiler_params=pltpu.CompilerParams(dimension_semantics=("arbitrary",), vmem_limit_bytes=VMEM_LIMIT),
        name="sample_attention",
    )(page_table, sel, qbd, cache, kvn, win, ocmp, gate)


def _sample_out_kernel(x_ref, convb_ref, attn_ref, sz_ref, w_ref, y_ref):
    mix = jnp.concatenate([convb_ref[...], (attn_ref[...] * sz_ref[...]).astype(BF16)], axis=1)
    y_ref[...] = x_ref[...] + _dot(mix, w_ref[...])


def _sample_out(x, convb, attn, sz, w_out):
    return pl.pallas_call(
        _sample_out_kernel,
        out_shape=jax.ShapeDtypeStruct(x.shape, F32),
        compiler_params=pltpu.CompilerParams(vmem_limit_bytes=VMEM_LIMIT),
        name="sample_out_proj",
    )(x, convb, attn, sz, w_out)


def _rope_tables(pos):
    half = HEAD_DIM // 2
    inv = ROPE_THETA ** (-jnp.arange(half, dtype=F32) / half)
    ang = pos.astype(F32)[:, None] * inv[None, :]
    cos, sin = jnp.cos(ang), jnp.sin(ang)
    return jnp.concatenate([cos, cos, cos, cos], axis=1), jnp.concatenate([-sin, sin, -sin, sin], axis=1)


def _layer_params(norm_g, w_in, conv_w, conv_b, q_gain, k_gain, cmp_pe, cmp_w, w_out):
    c_gates = C_KV + N_BRANCH * KV_LANES
    n_gates = N_HEADS * N_BRANCH
    w_all = jnp.concatenate([w_in[:, :c_gates], w_in[:, c_gates + n_gates:], w_in[:, c_gates:c_gates + n_gates],
                             jnp.zeros((D_MODEL, LANES - n_gates), w_in.dtype)], axis=1).astype(BF16)
    eye_g = jnp.eye(N_KV, dtype=cmp_w.dtype)
    eye_j = jnp.eye(2, dtype=cmp_w.dtype)
    wbd = jnp.einsum('rjde,jk,gh->rjgdkhe', cmp_w, eye_j, eye_g).reshape(CMP_BLK, KV_LANES, KV_LANES).astype(BF16)
    pe_row = jnp.broadcast_to(cmp_pe[:, :, None, :], (CMP_BLK, 2, N_KV, HEAD_DIM)).reshape(CMP_BLK, KV_LANES)
    blocks_per_pair = 2 * PAGE // CMP_BLK
    pe_t = jnp.tile(pe_row.T, (1, blocks_per_pair))
    m = jnp.arange(2 * PAGE)
    perm = (m[None, :] == (CMP_BLK * (m % blocks_per_pair) + m // blocks_per_pair)[:, None]).astype(BF16)
    return dict(
        ng=norm_g.reshape(1, D_MODEL), w_all=w_all, cw=conv_w, cb=conv_b.reshape(1, CONV_CH),
        qg=jnp.tile(q_gain, 2).reshape(1, LANES), kg=jnp.tile(k_gain, (1, 2)),
        pe_row=pe_row, pe_t=pe_t, perm=perm, wbd=wbd, w_out=w_out.astype(BF16))


def _kv_rows(a):
    return a.reshape(a.shape[:-1] + (2, N_KV, HEAD_DIM))


def _prompt_layer(x, p):
    b, t, _ = x.shape
    cos2, sin2 = _rope_tables(jnp.arange(t, dtype=jnp.int32))
    (convb, sz, q, kvc, kvs, kvw, ksa, vsa, kw, vw, gate, utail) = _prompt_in(
        x, p["ng"], p["w_all"], p["cw"], p["cb"], p["qg"], p["kg"], cos2, sin2, tm=512)
    kc, vc = _prompt_compress(kvc, p["pe_row"], p["wbd"])
    attnb = _prompt_attn(q, kc, vc, ksa, vsa, kw, vw, gate, sz, tq=256, tk=512)
    y = _out_proj(x.reshape(b * t, D_MODEL), convb.reshape(b * t, CONV_CH), attnb.reshape(b * t, ATTN_W),
                  p["w_out"], tm=512).reshape(b, t, D_MODEL)
    w_keep = min(WINDOW, t)
    return (y, _kv_rows(kvc), _kv_rows(kvs), _kv_rows(kvw[:, t - w_keep:]), utail[:, 8 - (CONV_W - 1):])


def _sample_layer(x, cache_cmp, cache_slc, win_buf, conv_buf, page_table, p):
    nb, t, _ = x.shape
    assert t == 1
    n_pages = page_table.shape[1]
    past = n_pages * PAGE
    assert past % (2 * SEL_BLK) == 0 and past // SEL_BLK >= LANES and past // SEL_BLK >= N_SEL
    n_pool = cache_cmp.shape[0]
    cos1, sin1 = _rope_tables(jnp.full((1,), past, dtype=jnp.int32))
    convb, sz, qbd, kvn, gate, u = _sample_in(
        x.reshape(nb, D_MODEL), p["ng"], p["w_all"], p["cw"], p["cb"], p["qg"], p["kg"], cos1, sin1,
        conv_buf[:, 0], conv_buf[:, 1])
    qbd = jnp.transpose(qbd, (1, 0, 2))
    feature_major = lambda a: jnp.transpose(a, (0, 2, 3, 4, 1)).reshape(a.shape[0], KV_LANES, a.shape[1])
    ocmp, imp = _sample_cmp(page_table, qbd, feature_major(cache_cmp), p["pe_t"], p["perm"], p["wbd"], past)
    idx = _sample_select(imp[:, :N_KV].reshape(nb * N_KV, past // CMP_BLK), past)
    sel = idx[:, :N_SEL - 1].reshape(nb, N_KV * (N_SEL - 1))
    attn = _sample_attn(page_table, sel, qbd, feature_major(cache_slc), kvn.reshape(nb, 1, N_BRANCH * KV_LANES),
                        feature_major(win_buf), ocmp, gate.reshape(nb, 1, LANES), past)
    y = _sample_out(x.reshape(nb, D_MODEL), convb, attn.reshape(nb, ATTN_W), sz, p["w_out"]).reshape(nb, 1, D_MODEL)
    kv_new = _kv_rows(kvn.reshape(nb, 1, N_BRANCH, KV_LANES))
    win_new = jnp.concatenate([win_buf, kv_new[:, :, 2]], axis=1)[:, t:]
    conv_new = jnp.concatenate([conv_buf, u.reshape(nb, 1, CONV_CH)], axis=1)[:, t:]
    return y, kv_new[:, :, 0], kv_new[:, :, 1], win_new, conv_new


def kernel(x_prompt, x_sample, cache_cmp_kv, cache_slc_kv, state_win_kv, state_conv, page_table, norm_g, w_in,
           conv_w, conv_b, q_gain, k_gain, cmp_pe, cmp_w, w_out):
    yp, ys = x_prompt, x_sample
    outs = [[] for _ in range(8)]
    for layer in range(norm_g.shape[0]):
        p = _layer_params(norm_g[layer], w_in[layer], conv_w[layer], conv_b[layer], q_gain[layer], k_gain[layer],
                          cmp_pe[layer], cmp_w[layer], w_out[layer])
        yp, *prompt_state = _prompt_layer(yp, p)
        ys, *sample_state = _sample_layer(ys, cache_cmp_kv[layer], cache_slc_kv[layer], state_win_kv[layer],
                                          state_conv[layer], page_table, p)
        for acc, a in zip(outs, prompt_state + sample_state):
            acc.append(a)
    return (yp, ys) + tuple(jnp.stack(a) for a in outs)
```

```python
import functools

import jax
import jax.numpy as jnp
from jax import lax
from jax.experimental import pallas as pl
from jax.experimental.pallas import tpu as pltpu

F32 = jnp.float32
BF16 = jnp.bfloat16

D_MODEL = 1024
CONV_CH = 512
CONV_W = 3
N_HEADS = 8
HEAD_DIM = 64
ATTN_W = N_HEADS * HEAD_DIM
N_KV = 2
GROUP = N_HEADS // N_KV
N_BRANCH = 3
CMP_BLK = 32
SEL_BLK = 64
N_SEL = 16
WINDOW = 512
PAGE = 128
ROPE_THETA = 10000.0
NORM_EPS = 1e-6
FORCE_BONUS = 1e4
KV_LANES = 2 * N_KV * HEAD_DIM
LANES = 128
SEL_CHUNK = 64
NEG_BIAS = -1e9
NEG_MASK = -1e30
Q_SCALE = HEAD_DIM ** -0.5 * 1.4426950408889634

C_CONV = 0
C_Q = 4 * CONV_CH
C_KV = C_Q + ATTN_W
C_Z = C_KV + N_BRANCH * KV_LANES
C_G = C_Z + ATTN_W
W_COLS = C_G + LANES

VMEM_LIMIT = 48 * 1024 * 1024
PAGES_PER_CHUNK = 32
PAIR_UNROLL = 16


def _dot(a, b):
    return jnp.dot(a, b, preferred_element_type=F32)


def _dot_nt(a, b):
    return lax.dot_general(a, b, (((1,), (1,)), ((), ())), preferred_element_type=F32)


def _iota(shape, dim):
    return lax.broadcasted_iota(jnp.int32, shape, dim)


def _masked_softmax(s, mask):
    s = jnp.where(mask, s, -jnp.inf)
    m = jnp.max(s, axis=-1, keepdims=True)
    m = jnp.where(m > -jnp.inf, m, 0.0)
    p = jnp.exp2(s - m)
    return p * (1.0 / jnp.maximum(jnp.sum(p, axis=-1, keepdims=True), 1e-30))


def _group_mean_sq(x):
    x2 = x * x
    hi = x2.astype(BF16)
    lo = (x2 - hi.astype(F32)).astype(BF16)
    same = ((_iota((2 * LANES, LANES), 0) >> 6) & 1) == (_iota((2 * LANES, LANES), 1) >> 6)
    ones = jnp.where(same, 1.0, 0.0).astype(BF16)
    return _dot(jnp.concatenate([hi, lo], axis=1), ones) * (1.0 / HEAD_DIM)


def _head_norm_rope(x, gain, cos2, sin2):
    xn = x * lax.rsqrt(_group_mean_sq(x) + NORM_EPS) * gain
    lane = _iota(x.shape, 1)
    swapped = jnp.where((lane & (HEAD_DIM - 1)) < HEAD_DIM // 2,
                        pltpu.roll(xn, LANES - HEAD_DIM // 2, 1), pltpu.roll(xn, HEAD_DIM // 2, 1))
    return xn * cos2 + swapped * sin2


def _low_half(x, other=0.0):
    lane = _iota(x.shape, 1)
    return jnp.where(lane < HEAD_DIM, x, other)


def _swap_halves(x):
    return pltpu.roll(x, HEAD_DIM, 1)


def _normed_input(x, norm_g):
    ms = jnp.mean(x * x, axis=-1, keepdims=True)
    return (x * lax.rsqrt(ms + NORM_EPS) * norm_g).astype(BF16)


def _silu(z):
    return z * jax.nn.sigmoid(z)


def _prompt_in_kernel(x_ref, ng_ref, w_ref, cw_ref, cb_ref, qg_ref, kg_ref, cos_ref, sin_ref,
                      convb_ref, sz_ref, q_ref, kvc_ref, kvs_ref, kvw_ref, ksa_ref, vsa_ref, kw_ref, vw_ref,
                      gate_ref, utail_ref, carry_ref, *, tm):
    ti = pl.program_id(1)
    h = _normed_input(x_ref[...], ng_ref[...])
    cos2 = cos_ref[...]
    sin2 = sin_ref[...]

    u = _dot(h, w_ref[:, CONV_CH:2 * CONV_CH]) * _dot(h, w_ref[:, 2 * CONV_CH:3 * CONV_CH])

    @pl.when(ti == 0)
    def _():
        carry_ref[...] = jnp.zeros_like(carry_ref)

    prev1 = carry_ref[7:8, :]
    prev2 = carry_ref[6:7, :]
    row = _iota(u.shape, 0)
    u1 = jnp.where(row == 0, prev1, pltpu.roll(u, 1, 0))
    u2 = jnp.where(row == 0, prev2, jnp.where(row == 1, prev1, pltpu.roll(u, 2, 0)))
    conv_y = cw_ref[0:1, :] * u2 + cw_ref[1:2, :] * u1 + cw_ref[2:3, :] * u + cb_ref[...]
    b_gate = _dot(h, w_ref[:, 0:CONV_CH])
    z_conv = _dot(h, w_ref[:, 3 * CONV_CH:4 * CONV_CH])
    convb_ref[...] = (b_gate * conv_y * _silu(z_conv)).astype(BF16)
    carry_ref[...] = u[tm - 8:tm, :]
    utail_ref[...] = u[tm - 8:tm, :]

    pq = _dot(h, w_ref[:, C_Q:C_KV])
    for c in range(N_HEADS // 2):
        qc = _head_norm_rope(pq[:, c * LANES:(c + 1) * LANES], qg_ref[...], cos2, sin2) * Q_SCALE
        for half in range(2):
            head = 2 * c + half
            src = qc if half == 0 else _swap_halves(qc)
            q_ref[head // GROUP, head % GROUP] = _low_half(src).astype(BF16)

    pk = _dot(h, w_ref[:, C_KV:C_Z])
    lane = _iota((tm, LANES), 1)
    tpos = ti * tm + _iota((tm, LANES), 0)
    onehot = jnp.where(((tpos >> 6) & (SEL_CHUNK - 1)) == lane - HEAD_DIM, 1.0, 0.0)
    ones_col = jnp.where(lane == HEAD_DIM, 1.0, 0.0)
    for br, out_ref in enumerate((kvc_ref, kvs_ref, kvw_ref)):
        kp = _head_norm_rope(pk[:, br * KV_LANES:br * KV_LANES + LANES], kg_ref[br:br + 1, :], cos2, sin2)
        vv = pk[:, br * KV_LANES + LANES:(br + 1) * KV_LANES]
        out_ref[:, 0:LANES] = kp
        out_ref[:, LANES:KV_LANES] = vv
        if br == 1:
            for g in range(N_KV):
                ksa_ref[g] = _low_half(kp if g == 0 else _swap_halves(kp), onehot).astype(BF16)
                vsa_ref[g] = _low_half(vv if g == 0 else _swap_halves(vv), ones_col).astype(BF16)
        if br == 2:
            for g in range(N_KV):
                kw_ref[g] = _low_half(kp if g == 0 else _swap_halves(kp)).astype(BF16)
                vw_ref[g] = _low_half(vv if g == 0 else _swap_halves(vv), ones_col).astype(BF16)

    sz_ref[...] = _silu(_dot(h, w_ref[:, C_Z:C_G]))
    sg = jax.nn.sigmoid(_dot(h, w_ref[:, C_G:W_COLS]))
    gate_ref[0] = sg
    gate_ref[1] = pltpu.roll(sg, LANES - GROUP * N_BRANCH, 1)


def _prompt_in(x, ng, w_all, cw, cb, qg, kg, cos2, sin2, tm):
    b, t, _ = x.shape
    grid = (b, t // tm)
    row_blk = lambda w: pl.BlockSpec((None, tm, w), lambda bi, ti: (bi, ti, 0))
    full = lambda a: pl.BlockSpec(a.shape, lambda bi, ti: (0,) * a.ndim)
    head_blk = pl.BlockSpec((None, N_KV, tm, LANES), lambda bi, ti: (bi, 0, ti, 0))
    out_shape = (
        jax.ShapeDtypeStruct((b, t, CONV_CH), BF16),
        jax.ShapeDtypeStruct((b, t, ATTN_W), F32),
        jax.ShapeDtypeStruct((b, N_KV, GROUP, t, LANES), BF16),
        jax.ShapeDtypeStruct((b, t, KV_LANES), F32),
        jax.ShapeDtypeStruct((b, t, KV_LANES), F32),
        jax.ShapeDtypeStruct((b, t, KV_LANES), F32),
        jax.ShapeDtypeStruct((b, N_KV, t, LANES), BF16),
        jax.ShapeDtypeStruct((b, N_KV, t, LANES), BF16),
        jax.ShapeDtypeStruct((b, N_KV, t, LANES), BF16),
        jax.ShapeDtypeStruct((b, N_KV, t, LANES), BF16),
        jax.ShapeDtypeStruct((b, N_KV, t, LANES), F32),
        jax.ShapeDtypeStruct((b, 8, CONV_CH), F32),
    )
    out_specs = (
        row_blk(CONV_CH), row_blk(ATTN_W),
        pl.BlockSpec((None, N_KV, GROUP, tm, LANES), lambda bi, ti: (bi, 0, 0, ti, 0)),
        row_blk(KV_LANES), row_blk(KV_LANES), row_blk(KV_LANES),
        head_blk, head_blk, head_blk, head_blk, head_blk,
        pl.BlockSpec((None, 8, CONV_CH), lambda bi, ti: (bi, 0, 0)),
    )
    tab = pl.BlockSpec((tm, LANES), lambda bi, ti: (ti, 0))
    return pl.pallas_call(
        functools.partial(_prompt_in_kernel, tm=tm),
        grid=grid,
        in_specs=[row_blk(D_MODEL), full(ng), full(w_all), full(cw), full(cb), full(qg), full(kg), tab, tab],
        out_specs=out_specs,
        out_shape=out_shape,
        scratch_shapes=[pltpu.VMEM((8, CONV_CH), F32)],
        compiler_params=pltpu.CompilerParams(dimension_semantics=("arbitrary", "arbitrary"),
                                             vmem_limit_bytes=VMEM_LIMIT),
        name="prompt_in_proj",
    )(x, ng, w_all, cw, cb, qg, kg, cos2, sin2)


def _compress_rows(xk_ref, xv_ref, pe_ref, wbd_ref, acc_ref, n_even):
    acc_ref[...] = jnp.zeros_like(acc_ref)

    def rows(ref, r):
        even = ref[pl.ds(r, n_even, stride=2 * CMP_BLK), :]
        odd = ref[pl.ds(CMP_BLK + r, n_even, stride=2 * CMP_BLK), :]
        return jnp.concatenate([even, odd], axis=0)

    def body(r, carry):
        xr = jnp.concatenate([rows(xk_ref, r), rows(xv_ref, r)], axis=1) + pe_ref[pl.ds(r, 1), :]
        acc_ref[...] += _dot(xr.astype(BF16), wbd_ref[r])
        return carry

    lax.fori_loop(0, CMP_BLK, body, 0)


def _prompt_compress_kernel(xk_ref, xv_ref, pe_ref, wbd_ref, kc_ref, vc_ref, acc_ref, *, n_even):
    _compress_rows(xk_ref, xv_ref, pe_ref, wbd_ref, acc_ref, n_even)
    kk = acc_ref[:, 0:LANES]
    vv = acc_ref[:, LANES:KV_LANES]
    for g in range(N_KV):
        kc_ref[g] = _low_half(kk if g == 0 else _swap_halves(kk)).astype(BF16)
        vc_ref[g] = _low_half(vv if g == 0 else _swap_halves(vv)).astype(BF16)


def _prompt_compress(kvc, pe_row, wbd):
    b, t, _ = kvc.shape
    nc = t // CMP_BLK
    out = jax.ShapeDtypeStruct((b, N_KV, nc, LANES), BF16)
    blk = pl.BlockSpec((None, N_KV, nc, LANES), lambda bi: (bi, 0, 0, 0))
    return pl.pallas_call(
        functools.partial(_prompt_compress_kernel, n_even=nc // 2),
        grid=(b,),
        in_specs=[pl.BlockSpec((None, t, LANES), lambda bi: (bi, 0, 0)),
                  pl.BlockSpec((None, t, LANES), lambda bi: (bi, 0, 1)),
                  pl.BlockSpec(pe_row.shape, lambda bi: (0, 0)),
                  pl.BlockSpec(wbd.shape, lambda bi: (0, 0, 0))],
        out_specs=(blk, blk),
        out_shape=(out, out),
        scratch_shapes=[pltpu.VMEM((nc, KV_LANES), F32)],
        compiler_params=pltpu.CompilerParams(dimension_semantics=("arbitrary",), vmem_limit_bytes=VMEM_LIMIT),
        name="prompt_compress",
    )(kvc, kvc, pe_row, wbd)


def _select_blocks(score, blk_f, n_rounds):
    sel = jnp.zeros(score.shape, F32)
    picks = []
    for _ in range(n_rounds):
        m = jnp.max(score, axis=-1, keepdims=True)
        first = jnp.min(jnp.where(score == m, blk_f, 1e9), axis=-1, keepdims=True)
        hit = blk_f == first
        sel = jnp.where(hit, 1.0, sel)
        score = jnp.where(hit, -jnp.inf, score)
        picks.append(first)
    return sel, picks


def _selection_bias(imp, qs):
    tq, ns = imp.shape
    n_chunks = tq // LANES
    imp_t = jnp.concatenate([imp[r * LANES:(r + 1) * LANES, :].T for r in range(n_chunks)], axis=1)
    blk = _iota((ns, tq), 0)
    qp = qs + _iota((1, tq), 1)
    qb = qp >> 6
    valid = blk * SEL_BLK <= qp
    forced = ((blk == 0) | (blk == qb) | (blk == qb - 1)) & valid
    blk_f = blk.astype(F32)
    candidate = valid & jnp.logical_not(forced)
    score = jnp.where(candidate, imp_t, -jnp.inf)
    for _ in range(min(N_SEL, ns) - 3):
        m = jnp.max(score, axis=0, keepdims=True)
        first = jnp.min(jnp.where(score == m, blk_f, 1e9), axis=0, keepdims=True)
        score = jnp.where(blk_f == first, -jnp.inf, score)
    bias_t = jnp.where(forced | (candidate & (score == -jnp.inf)), 0.0, NEG_BIAS)
    return jnp.concatenate([bias_t[:, r * LANES:(r + 1) * LANES].T for r in range(n_chunks)], axis=0)


def _prompt_attn_kernel(q_ref, kc_ref, vc_ref, ksa_ref, vsa_ref, kw_ref, vw_ref, gate_ref, sz_ref, out_ref,
                        qa_ref, sa_ref, sb_ref, m_ref, acc_ref, *, tq, tk, t):
    i = pl.program_id(2)
    qs = i * tq
    rows = GROUP * tq
    nc = t // CMP_BLK
    ns = nc // 2
    q = q_ref[...].reshape(rows, LANES)
    qpos = qs + (_iota((rows, 1), 0) & (tq - 1))

    col = _iota((1, nc), 1)
    cblk = jnp.where(col < ns, 2 * col, 2 * (col - ns) + 1)
    s_cmp = jnp.where((cblk + 1) * CMP_BLK - 1 <= qpos, _dot_nt(q, kc_ref[...]), -jnp.inf)
    m_cmp = jnp.max(s_cmp, axis=-1, keepdims=True)
    e_cmp = jnp.exp2(s_cmp - jnp.where(m_cmp > -jnp.inf, m_cmp, 0.0))
    inv_cmp = 1.0 / jnp.maximum(jnp.sum(e_cmp, axis=-1, keepdims=True), 1e-30)
    o_cmp = _dot(e_cmp.astype(BF16), vc_ref[...]) * inv_cmp

    pair = (e_cmp[:, 0:ns] + e_cmp[:, ns:nc]) * inv_cmp
    imp = pair[0:tq]
    for n in range(1, GROUP):
        imp = imp + pair[n * tq:(n + 1) * tq]
    bias = _selection_bias(imp, qs)

    for c in range(ns // SEL_CHUNK):
        bc = bias[:, (c // 2) * LANES:(c // 2 + 1) * LANES]
        if c % 2 == 0:
            bc = _swap_halves(bc)
        bc = jnp.concatenate([bc] * GROUP, axis=0).astype(BF16)
        qa_ref[c] = _low_half(q, bc)

    q_win = jnp.where(_iota((rows, LANES), 1) == HEAD_DIM, NEG_BIAS, q.astype(F32)).astype(BF16)
    r0 = _iota((GROUP * LANES, 1), 0) & (LANES - 1)
    upper = _iota((1, LANES), 1) > r0
    n_mid = WINDOW // LANES - 1
    o_win_blocks = []
    for a in range(tq // LANES):
        q_a = jnp.concatenate([q_win[n * tq + a * LANES:n * tq + (a + 1) * LANES] for n in range(GROUP)], axis=0)
        base = pl.multiple_of(qs + a * LANES, LANES)
        s_a = _dot_nt(q_a, kw_ref[pl.ds(base, WINDOW + LANES), :])
        s_a = jnp.concatenate([jnp.where(upper, s_a[:, 0:LANES], NEG_MASK), s_a[:, LANES:(n_mid + 1) * LANES],
                               jnp.where(upper, NEG_MASK, s_a[:, (n_mid + 1) * LANES:])], axis=1)
        e_a = jnp.exp2(s_a - jnp.max(s_a, axis=-1, keepdims=True))
        acc_a = _dot(e_a.astype(BF16), vw_ref[pl.ds(base, WINDOW + LANES), :])
        o_win_blocks.append(_low_half(acc_a * (1.0 / acc_a[:, HEAD_DIM:HEAD_DIM + 1])))
    o_win = jnp.concatenate([blk_a[n * LANES:(n + 1) * LANES] for n in range(GROUP) for blk_a in o_win_blocks], axis=0)

    m_ref[...] = jnp.full_like(m_ref, NEG_MASK)
    acc_ref[...] = jnp.zeros_like(acc_ref)

    def scores(kt):
        k0 = pl.multiple_of(kt * tk, tk)
        return _dot_nt(qa_ref[kt // (SEL_CHUNK * SEL_BLK // tk)], ksa_ref[pl.ds(k0, tk), :])

    def update(s, kt, causal):
        k0 = pl.multiple_of(kt * tk, tk)
        if causal:
            s = jnp.where(k0 + _iota((1, tk), 1) <= qpos, s, NEG_MASK)
        m_old = m_ref[...]
        m_new = jnp.maximum(m_old, jnp.broadcast_to(jnp.max(s, axis=-1, keepdims=True), m_old.shape))
        p = jnp.exp2(s - jnp.concatenate([m_new] * (tk // LANES), axis=1))
        acc_ref[...] = jnp.exp2(m_old - m_new) * acc_ref[...] + _dot(p.astype(BF16), vsa_ref[pl.ds(k0, tk), :])
        m_ref[...] = m_new

    n_kt = (qs + tq + tk - 1) // tk
    n_pairs = (n_kt - 1) // 2
    sa_ref[...] = scores(0)

    def pair_step(j, carry):
        sb_ref[...] = scores(2 * j + 1)
        update(sa_ref[...], 2 * j, False)
        sa_ref[...] = scores(2 * j + 2)
        update(sb_ref[...], 2 * j + 1, False)
        return carry

    lax.fori_loop(0, n_pairs, pair_step, 0)
    last = n_kt - 1

    @pl.when(n_kt - 2 * n_pairs == 1)
    def _():
        update(sa_ref[...], last, True)

    @pl.when(n_kt - 2 * n_pairs == 2)
    def _():
        sb_ref[...] = scores(last)
        update(sa_ref[...], last - 1, False)
        update(sb_ref[...], last, True)

    acc = acc_ref[...]
    o_slc = _low_half(acc * (1.0 / acc[:, HEAD_DIM:HEAD_DIM + 1]))

    gate = gate_ref[...]
    mixed = []
    for n in range(GROUP):
        r = slice(n * tq, (n + 1) * tq)
        mixed.append(gate[:, 3 * n:3 * n + 1] * o_cmp[r] + gate[:, 3 * n + 1:3 * n + 2] * o_slc[r]
                     + gate[:, 3 * n + 2:3 * n + 3] * o_win[r])
    for c in range(GROUP // 2):
        o_pair = mixed[2 * c] + _swap_halves(mixed[2 * c + 1])
        out_ref[:, c * LANES:(c + 1) * LANES] = (o_pair * sz_ref[:, c * LANES:(c + 1) * LANES]).astype(BF16)


def _prompt_attn(q, kc, vc, ksa, vsa, kw, vw, gate, sz, tq, tk):
    b, _, _, t, _ = q.shape
    assert t % (2 * SEL_CHUNK * SEL_BLK) == 0 and t >= WINDOW + tq and (SEL_CHUNK * SEL_BLK) % tk == 0
    assert tq % LANES == 0 and N_SEL > 3
    nc = t // CMP_BLK
    rows = GROUP * tq
    seq = lambda n: pl.BlockSpec((None, None, n, LANES), lambda bi, g, i: (bi, g, 0, 0))
    return pl.pallas_call(
        functools.partial(_prompt_attn_kernel, tq=tq, tk=tk, t=t),
        grid=(b, N_KV, t // tq),
        in_specs=[pl.BlockSpec((None, None, GROUP, tq, LANES), lambda bi, g, i: (bi, g, 0, i, 0)),
                  seq(nc), seq(nc), seq(t), seq(t), seq(t + WINDOW), seq(t + WINDOW),
                  pl.BlockSpec((None, None, tq, LANES), lambda bi, g, i: (bi, g, i, 0)),
                  pl.BlockSpec((None, tq, GROUP * HEAD_DIM), lambda bi, g, i: (bi, i, g))],
        out_specs=pl.BlockSpec((None, tq, GROUP * HEAD_DIM), lambda bi, g, i: (bi, i, g)),
        out_shape=jax.ShapeDtypeStruct((b, t, ATTN_W), BF16),
        scratch_shapes=[pltpu.VMEM((t // (SEL_CHUNK * SEL_BLK), rows, LANES), BF16),
                        pltpu.VMEM((rows, tk), F32), pltpu.VMEM((rows, tk), F32),
                        pltpu.VMEM((rows, LANES), F32), pltpu.VMEM((rows, LANES), F32)],
        compiler_params=pltpu.CompilerParams(dimension_semantics=("arbitrary", "arbitrary", "arbitrary"),
                                             vmem_limit_bytes=VMEM_LIMIT),
        name="prompt_attention",
    )(q, kc, vc, ksa, vsa, kw, vw, gate, sz)


def _out_proj_kernel(x_ref, convb_ref, attnb_ref, w_ref, y_ref):
    mix = jnp.concatenate([convb_ref[...], attnb_ref[...]], axis=1)
    y_ref[...] = x_ref[...] + _dot(mix, w_ref[...])


def _out_proj(x2d, convb, attnb, w_out, tm):
    m = x2d.shape[0]
    blk = lambda w: pl.BlockSpec((tm, w), lambda i: (i, 0))
    return pl.pallas_call(
        _out_proj_kernel,
        grid=(m // tm,),
        in_specs=[blk(D_MODEL), blk(CONV_CH), blk(ATTN_W), pl.BlockSpec(w_out.shape, lambda i: (0, 0))],
        out_specs=blk(D_MODEL),
        out_shape=jax.ShapeDtypeStruct((m, D_MODEL), F32),
        compiler_params=pltpu.CompilerParams(dimension_semantics=("arbitrary",), vmem_limit_bytes=VMEM_LIMIT),
        name="out_proj",
    )(x2d, convb, attnb, w_out)


def _sample_in_kernel(x_ref, ng_ref, w_ref, cw_ref, cb_ref, qg_ref, kg_ref, cos_ref, sin_ref, c0_ref, c1_ref,
                      convb_ref, sz_ref, qbd_ref, kvn_ref, gate_ref, u_ref):
    h = _normed_input(x_ref[...], ng_ref[...])
    cos2 = cos_ref[...]
    sin2 = sin_ref[...]
    pc = _dot(h, w_ref[:, C_CONV:C_Q])
    b_gate = pc[:, 0:CONV_CH]
    u = pc[:, CONV_CH:2 * CONV_CH] * pc[:, 2 * CONV_CH:3 * CONV_CH]
    z_conv = pc[:, 3 * CONV_CH:4 * CONV_CH]
    conv_y = cw_ref[0:1, :] * c0_ref[...] + cw_ref[1:2, :] * c1_ref[...] + cw_ref[2:3, :] * u + cb_ref[...]
    convb_ref[...] = (b_gate * conv_y * _silu(z_conv)).astype(BF16)
    u_ref[...] = u

    pq = _dot(h, w_ref[:, C_Q:C_KV])
    zeros = jnp.zeros((x_ref.shape[0], LANES), BF16)
    for c in range(N_HEADS // 2):
        qc = _head_norm_rope(pq[:, c * LANES:(c + 1) * LANES], qg_ref[...], cos2, sin2) * Q_SCALE
        lane = _iota(qc.shape, 1)
        for half in range(2):
            head = 2 * c + half
            g = head // GROUP
            src = qc if half == g else _swap_halves(qc)
            keep = (lane >= g * HEAD_DIM) & (lane < (g + 1) * HEAD_DIM)
            qbd_ref[head] = jnp.concatenate([jnp.where(keep, src, 0.0).astype(BF16), zeros], axis=1)

    pk = _dot(h, w_ref[:, C_KV:C_Z])
    for br in range(N_BRANCH):
        kvn_ref[:, br * KV_LANES:br * KV_LANES + LANES] = _head_norm_rope(
            pk[:, br * KV_LANES:br * KV_LANES + LANES], kg_ref[br:br + 1, :], cos2, sin2)
        kvn_ref[:, br * KV_LANES + LANES:(br + 1) * KV_LANES] = pk[:, br * KV_LANES + LANES:(br + 1) * KV_LANES]

    sz_ref[...] = _silu(_dot(h, w_ref[:, C_Z:C_G]))
    gate_ref[...] = jax.nn.sigmoid(_dot(h, w_ref[:, C_G:W_COLS]))


def _sample_in(x, ng, w_all, cw, cb, qg, kg, cos1, sin1, c0, c1):
    nb = x.shape[0]
    out_shape = (
        jax.ShapeDtypeStruct((nb, CONV_CH), BF16),
        jax.ShapeDtypeStruct((nb, ATTN_W), F32),
        jax.ShapeDtypeStruct((N_HEADS, nb, 2 * LANES), BF16),
        jax.ShapeDtypeStruct((nb, N_BRANCH * KV_LANES), F32),
        jax.ShapeDtypeStruct((nb, LANES), F32),
        jax.ShapeDtypeStruct((nb, CONV_CH), F32),
    )
    return pl.pallas_call(
        _sample_in_kernel,
        out_shape=out_shape,
        compiler_params=pltpu.CompilerParams(vmem_limit_bytes=VMEM_LIMIT),
        name="sample_in_proj",
    )(x, ng, w_all, cw, cb, qg, kg, cos1, sin1, c0, c1)


def _sample_cmp_kernel(pt_ref, qbd_ref, cache_ref, pet_ref, perm_ref, wbd_ref, ocmp_ref, imp_ref,
                       buf_ref, rows_ref, acc_ref, sem, *, n_batch, n_pages, past):
    b = pl.program_id(0)
    n_chunks = n_pages // PAGES_PER_CHUNK
    blocks_per_pair = 2 * PAGE // CMP_BLK

    def page_copy(bb, ch, p, slot):
        return pltpu.make_async_copy(cache_ref.at[pt_ref[bb, ch * PAGES_PER_CHUNK + p]], buf_ref.at[slot, p],
                                     sem.at[slot])

    def start_chunk(bb, ch, slot):
        def body(p, carry):
            page_copy(bb, ch, p, slot).start()
            return carry
        lax.fori_loop(0, PAGES_PER_CHUNK, body, 0)

    def wait_chunk(ch, slot):
        def body(p, carry):
            page_copy(b, ch, p, slot).wait()
            return carry
        lax.fori_loop(0, PAGES_PER_CHUNK, body, 0)

    @pl.when(b == 0)
    def _():
        start_chunk(0, 0, 0)

    for ch in range(n_chunks):
        slot = ch % 2
        if ch + 1 < n_chunks:
            start_chunk(b, ch + 1, 1 - slot)
        else:
            @pl.when(b + 1 < n_batch)
            def _():
                start_chunk(b + 1, 0, 1 - slot)
        wait_chunk(ch, slot)

        def pairs_body(it, carry):
            for k in range(PAIR_UNROLL):
                pr = it * PAIR_UNROLL + k
                xt = jnp.concatenate([buf_ref[slot, 2 * pr], buf_ref[slot, 2 * pr + 1]], axis=1) + pet_ref[...]
                x_perm = _dot_nt(perm_ref[...], xt.astype(BF16))
                base = pl.multiple_of((ch * (PAGES_PER_CHUNK // 2) + pr) * blocks_per_pair, blocks_per_pair)
                for r in range(CMP_BLK):
                    rows_ref[r, pl.ds(base, blocks_per_pair), :] = (
                        x_perm[r * blocks_per_pair:(r + 1) * blocks_per_pair, :])
            return carry

        lax.fori_loop(0, PAGES_PER_CHUNK // 2 // PAIR_UNROLL, pairs_body, 0)

    acc_ref[...] = jnp.zeros_like(acc_ref)

    def compress_body(it, carry):
        acc = acc_ref[...]
        for k in range(PAIR_UNROLL):
            r = it * PAIR_UNROLL + k
            acc = acc + _dot(rows_ref[r].astype(BF16), wbd_ref[r])
        acc_ref[...] = acc
        return carry

    lax.fori_loop(0, CMP_BLK // PAIR_UNROLL, compress_body, 0)

    nc = past // CMP_BLK
    kv = acc_ref[...].astype(BF16)
    qbd = qbd_ref[...]
    p_cmp = _masked_softmax(_dot_nt(qbd, kv), (_iota((1, nc), 1) + 1) * CMP_BLK - 1 <= past)
    ocmp_ref[...] = _dot(p_cmp.astype(BF16), kv)

    pair = p_cmp + pltpu.roll(p_cmp, nc - 1, 1)
    row = _iota((N_HEADS, nc), 0)
    imp_ref[...] = jnp.where(row == 0, jnp.sum(jnp.where(row < GROUP, pair, 0.0), axis=0, keepdims=True),
                             jnp.sum(jnp.where(row >= GROUP, pair, 0.0), axis=0, keepdims=True))


def _sample_cmp(page_table, qbd, cache_t, pe_t, perm, wbd, past):
    nb, n_pages = page_table.shape
    assert n_pages % (2 * PAGES_PER_CHUNK) == 0
    nc = past // CMP_BLK
    const = lambda a: pl.BlockSpec(a.shape, lambda b, pt: (0,) * a.ndim)
    grid_spec = pltpu.PrefetchScalarGridSpec(
        num_scalar_prefetch=1,
        grid=(nb,),
        in_specs=[pl.BlockSpec((None, N_HEADS, 2 * LANES), lambda b, pt: (b, 0, 0)),
                  pl.BlockSpec(memory_space=pl.ANY), const(pe_t), const(perm), const(wbd)],
        out_specs=(pl.BlockSpec((None, N_HEADS, KV_LANES), lambda b, pt: (b, 0, 0)),
                   pl.BlockSpec((None, N_HEADS, nc), lambda b, pt: (b, 0, 0))),
        scratch_shapes=[pltpu.VMEM((2, PAGES_PER_CHUNK, KV_LANES, PAGE), F32),
                        pltpu.VMEM((CMP_BLK, nc, KV_LANES), F32),
                        pltpu.VMEM((nc, KV_LANES), F32), pltpu.SemaphoreType.DMA((2,))],
    )
    return pl.pallas_call(
        functools.partial(_sample_cmp_kernel, n_batch=nb, n_pages=n_pages, past=past),
        grid_spec=grid_spec,
        out_shape=(jax.ShapeDtypeStruct((nb, N_HEADS, KV_LANES), F32),
                   jax.ShapeDtypeStruct((nb, N_HEADS, nc), F32)),
        compiler_params=pltpu.CompilerParams(dimension_semantics=("arbitrary",), vmem_limit_bytes=VMEM_LIMIT),
        name="sample_compress",
    )(page_table, qbd, cache_t, pe_t, perm, wbd)


def _sample_select_kernel(imp_ref, idx_ref, *, past):
    imp = imp_ref[...]
    lane = _iota(imp.shape, 1)
    blk = lane >> 1
    qb = past // SEL_BLK
    forced = (blk == 0) | (blk == qb) | (blk == qb - 1)
    valid = ((lane & 1) == 0) & (blk * SEL_BLK <= past)
    score = jnp.where(valid, imp + jnp.where(forced, FORCE_BONUS, 0.0), -jnp.inf)
    _, picks = _select_blocks(score, lane.astype(F32), N_SEL - 1)
    out_lane = _iota(idx_ref.shape, 1)
    idx = jnp.zeros(idx_ref.shape, F32)
    for k, pick in enumerate(picks):
        idx = jnp.where(out_lane == k, pick * 0.5, idx)
    idx_ref[...] = idx.astype(jnp.int32)


def _sample_select(imp, past):
    return pl.pallas_call(
        functools.partial(_sample_select_kernel, past=past),
        out_shape=jax.ShapeDtypeStruct((imp.shape[0], LANES), jnp.int32),
        compiler_params=pltpu.CompilerParams(vmem_limit_bytes=VMEM_LIMIT),
        name="sample_select",
    )(imp)


def _sample_attn_kernel(pt_ref, sel_ref, qbd_ref, cache_ref, kvn_ref, win_ref, ocmp_ref, gate_ref, out_ref,
                        buf_ref, sem, *, n_batch, past):
    b = pl.program_id(0)
    n_pick = N_SEL - 1
    blocks_per_page = PAGE // SEL_BLK
    slot = b % 2

    def page_copy(bb, j, s):
        return pltpu.make_async_copy(cache_ref.at[pt_ref[bb, sel_ref[bb, j] >> 1]], buf_ref.at[s, j], sem.at[s])

    @pl.when(b == 0)
    def _():
        for j in range(N_KV * n_pick):
            page_copy(0, j, 0).start()

    @pl.when(b + 1 < n_batch)
    def _():
        for j in range(N_KV * n_pick):
            page_copy(b + 1, j, 1 - slot).start()

    qbd = qbd_ref[...]
    qf = qbd.astype(F32)
    row = _iota((N_HEADS, 1), 0)

    def new_token(offset):
        kv_new = kvn_ref[:, offset:offset + KV_LANES].astype(BF16).astype(F32)
        return jnp.sum(qf * kv_new, axis=-1, keepdims=True), kv_new

    def attend(keys_t, mask, s_new, kv_new):
        s = jnp.where(mask, _dot(qbd, keys_t), -jnp.inf)
        m = jnp.maximum(jnp.max(s, axis=-1, keepdims=True), s_new)
        p = jnp.exp2(s - m)
        p_new = jnp.exp2(s_new - m)
        norm = 1.0 / (jnp.sum(p, axis=-1, keepdims=True) + p_new)
        return (_dot_nt(p.astype(BF16), keys_t) + p_new.astype(BF16).astype(F32) * kv_new) * norm

    w_buf = win_ref.shape[1]
    s_new, kv_new = new_token(2 * KV_LANES)
    x_win = attend(win_ref[...].astype(BF16), _iota((1, w_buf), 1) > w_buf - WINDOW, s_new, kv_new)

    for j in range(N_KV * n_pick):
        page_copy(b, j, slot).wait()

    s_new, kv_new = new_token(KV_LANES)
    lane = _iota((1, n_pick * PAGE), 1)
    x_slc = []
    for g in range(N_KV):
        slots = range(g * n_pick, (g + 1) * n_pick)
        keys_t = jnp.concatenate([buf_ref[slot, j] for j in slots], axis=1).astype(BF16)
        half = jnp.concatenate([jnp.full((1, PAGE), sel_ref[b, j] & (blocks_per_page - 1), jnp.int32) for j in slots],
                               axis=1)
        x_slc.append(attend(keys_t, ((lane & (PAGE - 1)) >> (SEL_BLK.bit_length() - 1)) == half, s_new, kv_new))
    x_slc = jnp.where(row < GROUP, x_slc[0], x_slc[1])

    gate = jnp.broadcast_to(gate_ref[...], (N_HEADS, LANES))
    lane = _iota((N_HEADS, LANES), 1)
    mixed = jnp.zeros((N_HEADS, KV_LANES), F32)
    for br, x_br in enumerate((ocmp_ref[...], x_slc, x_win)):
        g_col = jnp.sum(jnp.where(lane == N_BRANCH * _iota((N_HEADS, LANES), 0) + br, gate, 0.0), axis=-1, keepdims=True)
        mixed = mixed + g_col * x_br
    v_lo = mixed[:, 2 * HEAD_DIM:3 * HEAD_DIM]
    v_hi = mixed[:, 3 * HEAD_DIM:4 * HEAD_DIM]
    out_ref[...] = jnp.where(row < GROUP, v_lo, v_hi)


def _sample_attn(page_table, sel, qbd, cache, kvn, win, ocmp, gate, past):
    nb = page_table.shape[0]
    n_pick = N_SEL - 1
    per_b = lambda *shape: pl.BlockSpec((None,) + shape, lambda b, pt, s: (b,) + (0,) * len(shape))
    grid_spec = pltpu.PrefetchScalarGridSpec(
        num_scalar_prefetch=2,
        grid=(nb,),
        in_specs=[per_b(N_HEADS, 2 * LANES),
                  pl.BlockSpec(memory_space=pl.ANY),
                  per_b(1, N_BRANCH * KV_LANES),
                  per_b(KV_LANES, win.shape[2]),
                  per_b(N_HEADS, KV_LANES),
                  per_b(1, LANES)],
        out_specs=per_b(N_HEADS, HEAD_DIM),
        scratch_shapes=[pltpu.VMEM((2, N_KV * n_pick, KV_LANES, PAGE), F32), pltpu.SemaphoreType.DMA((2,))],
    )
    return pl.pallas_call(
        functools.partial(_sample_attn_kernel, n_batch=nb, past=past),
        grid_spec=grid_spec,
        out_shape=jax.ShapeDtypeStruct((nb, N_HEADS, HEAD_DIM), F32),
        compiler_params=pltpu.CompilerParams(dimension_semantics=("arbitrary",), vmem_limit_bytes=VMEM_LIMIT),
        name="sample_attention",
    )(page_table, sel, qbd, cache, kvn, win, ocmp, gate)


def _sample_out_kernel(x_ref, convb_ref, attn_ref, sz_ref, w_ref, y_ref):
    mix = jnp.concatenate([convb_ref[...], (attn_ref[...] * sz_ref[...]).astype(BF16)], axis=1)
    y_ref[...] = x_ref[...] + _dot(mix, w_ref[...])


def _sample_out(x, convb, attn, sz, w_out):
    return pl.pallas_call(
        _sample_out_kernel,
        out_shape=jax.ShapeDtypeStruct(x.shape, F32),
        compiler_params=pltpu.CompilerParams(vmem_limit_bytes=VMEM_LIMIT),
        name="sample_out_proj",
    )(x, convb, attn, sz, w_out)


def _rope_tables(pos):
    half = HEAD_DIM // 2
    inv = ROPE_THETA ** (-jnp.arange(half, dtype=F32) / half)
    ang = pos.astype(F32)[:, None] * inv[None, :]
    cos, sin = jnp.cos(ang), jnp.sin(ang)
    return jnp.concatenate([cos, cos, cos, cos], axis=1), jnp.concatenate([-sin, sin, -sin, sin], axis=1)


def _layer_params(norm_g, w_in, conv_w, conv_b, q_gain, k_gain, cmp_pe, cmp_w, w_out):
    c_gates = C_KV + N_BRANCH * KV_LANES
    n_gates = N_HEADS * N_BRANCH
    w_all = jnp.concatenate([w_in[:, :c_gates], w_in[:, c_gates + n_gates:], w_in[:, c_gates:c_gates + n_gates],
                             jnp.zeros((D_MODEL, LANES - n_gates), w_in.dtype)], axis=1).astype(BF16)
    eye_g = jnp.eye(N_KV, dtype=cmp_w.dtype)
    eye_j = jnp.eye(2, dtype=cmp_w.dtype)
    wbd = jnp.einsum('rjde,jk,gh->rjgdkhe', cmp_w, eye_j, eye_g).reshape(CMP_BLK, KV_LANES, KV_LANES).astype(BF16)
    pe_row = jnp.broadcast_to(cmp_pe[:, :, None, :], (CMP_BLK, 2, N_KV, HEAD_DIM)).reshape(CMP_BLK, KV_LANES)
    blocks_per_pair = 2 * PAGE // CMP_BLK
    pe_t = jnp.tile(pe_row.T, (1, blocks_per_pair))
    m = jnp.arange(2 * PAGE)
    perm = (m[None, :] == (CMP_BLK * (m % blocks_per_pair) + m // blocks_per_pair)[:, None]).astype(BF16)
    return dict(
        ng=norm_g.reshape(1, D_MODEL), w_all=w_all, cw=conv_w, cb=conv_b.reshape(1, CONV_CH),
        qg=jnp.tile(q_gain, 2).reshape(1, LANES), kg=jnp.tile(k_gain, (1, 2)),
        pe_row=pe_row, pe_t=pe_t, perm=perm, wbd=wbd, w_out=w_out.astype(BF16))


def _kv_rows(a):
    return a.reshape(a.shape[:-1] + (2, N_KV, HEAD_DIM))


def _prompt_layer(x, p):
    b, t, _ = x.shape
    cos2, sin2 = _rope_tables(jnp.arange(t, dtype=jnp.int32))
    (convb, sz, q, kvc, kvs, kvw, ksa, vsa, kw, vw, gate, utail) = _prompt_in(
        x, p["ng"], p["w_all"], p["cw"], p["cb"], p["qg"], p["kg"], cos2, sin2, tm=512)
    kc, vc = _prompt_compress(kvc, p["pe_row"], p["wbd"])
    pad_k = jnp.zeros((b, N_KV, WINDOW, LANES), BF16).at[..., HEAD_DIM].set(1.0)
    kw = jnp.concatenate([pad_k, kw], axis=2)
    vw = jnp.concatenate([jnp.zeros((b, N_KV, WINDOW, LANES), BF16), vw], axis=2)
    attnb = _prompt_attn(q, kc, vc, ksa, vsa, kw, vw, gate, sz, tq=256, tk=512)
    y = _out_proj(x.reshape(b * t, D_MODEL), convb.reshape(b * t, CONV_CH), attnb.reshape(b * t, ATTN_W),
                  p["w_out"], tm=512).reshape(b, t, D_MODEL)
    w_keep = min(WINDOW, t)
    return (y, _kv_rows(kvc), _kv_rows(kvs), _kv_rows(kvw[:, t - w_keep:]), utail[:, 8 - (CONV_W - 1):])


def _sample_layer(x, cache_cmp, cache_slc, win_buf, conv_buf, page_table, p):
    nb, t, _ = x.shape
    assert t == 1
    n_pages = page_table.shape[1]
    past = n_pages * PAGE
    assert past % (2 * SEL_BLK) == 0 and past // SEL_BLK >= LANES and past // SEL_BLK >= N_SEL
    n_pool = cache_cmp.shape[0]
    cos1, sin1 = _rope_tables(jnp.full((1,), past, dtype=jnp.int32))
    convb, sz, qbd, kvn, gate, u = _sample_in(
        x.reshape(nb, D_MODEL), p["ng"], p["w_all"], p["cw"], p["cb"], p["qg"], p["kg"], cos1, sin1,
        conv_buf[:, 0], conv_buf[:, 1])
    qbd = jnp.transpose(qbd, (1, 0, 2))
    feature_major = lambda a: jnp.transpose(a, (0, 2, 3, 4, 1)).reshape(a.shape[0], KV_LANES, a.shape[1])
    ocmp, imp = _sample_cmp(page_table, qbd, feature_major(cache_cmp), p["pe_t"], p["perm"], p["wbd"], past)
    idx = _sample_select(imp[:, :N_KV].reshape(nb * N_KV, past // CMP_BLK), past)
    sel = idx[:, :N_SEL - 1].reshape(nb, N_KV * (N_SEL - 1))
    attn = _sample_attn(page_table, sel, qbd, feature_major(cache_slc), kvn.reshape(nb, 1, N_BRANCH * KV_LANES),
                        feature_major(win_buf), ocmp, gate.reshape(nb, 1, LANES), past)
    y = _sample_out(x.reshape(nb, D_MODEL), convb, attn.reshape(nb, ATTN_W), sz, p["w_out"]).reshape(nb, 1, D_MODEL)
    kv_new = _kv_rows(kvn.reshape(nb, 1, N_BRANCH, KV_LANES))
    win_new = jnp.concatenate([win_buf, kv_new[:, :, 2]], axis=1)[:, t:]
    conv_new = jnp.concatenate([conv_buf, u.reshape(nb, 1, CONV_CH)], axis=1)[:, t:]
    return y, kv_new[:, :, 0], kv_new[:, :, 1], win_new, conv_new


def kernel(x_prompt, x_sample, cache_cmp_kv, cache_slc_kv, state_win_kv, state_conv, page_table, norm_g, w_in,
           conv_w, conv_b, q_gain, k_gain, cmp_pe, cmp_w, w_out):
    yp, ys = x_prompt, x_sample
    outs = [[] for _ in range(8)]
    for layer in range(norm_g.shape[0]):
        p = _layer_params(norm_g[layer], w_in[layer], conv_w[layer], conv_b[layer], q_gain[layer], k_gain[layer],
                          cmp_pe[layer], cmp_w[layer], w_out[layer])
        yp, *prompt_state = _prompt_layer(yp, p)
        ys, *sample_state = _sample_layer(ys, cache_cmp_kv[layer], cache_slc_kv[layer], state_win_kv[layer],
                                          state_conv[layer], page_table, p)
        for acc, a in zip(outs, prompt_state + sample_state):
            acc.append(a)
    return (yp, ys) + tuple(jnp.stack(a) for a in outs)
```

```python
import functools

import jax
import jax.numpy as jnp
from jax import lax
from jax.experimental import pallas as pl
from jax.experimental.pallas import tpu as pltpu

F32 = jnp.float32
BF16 = jnp.bfloat16

D_MODEL = 1024
CONV_CH = 512
CONV_W = 3
N_HEADS = 8
HEAD_DIM = 64
ATTN_W = N_HEADS * HEAD_DIM
N_KV = 2
GROUP = N_HEADS // N_KV
N_BRANCH = 3
CMP_BLK = 32
SEL_BLK = 64
N_SEL = 16
WINDOW = 512
PAGE = 128
ROPE_THETA = 10000.0
NORM_EPS = 1e-6
FORCE_BONUS = 1e4
KV_LANES = 2 * N_KV * HEAD_DIM
LANES = 128
SEL_CHUNK = 64
NEG_BIAS = -1e9
NEG_MASK = -1e30
Q_SCALE = HEAD_DIM ** -0.5 * 1.4426950408889634

C_CONV = 0
C_Q = 4 * CONV_CH
C_KV = C_Q + ATTN_W
C_Z = C_KV + N_BRANCH * KV_LANES
C_G = C_Z + ATTN_W
W_COLS = C_G + LANES

VMEM_LIMIT = 48 * 1024 * 1024
PAGES_PER_CHUNK = 32
CHUNK_SLOTS = 3
PAIR_UNROLL = 16


def _dot(a, b):
    return jnp.dot(a, b, preferred_element_type=F32)


def _dot_nt(a, b):
    return lax.dot_general(a, b, (((1,), (1,)), ((), ())), preferred_element_type=F32)


def _iota(shape, dim):
    return lax.broadcasted_iota(jnp.int32, shape, dim)


def _masked_softmax(s, mask):
    s = jnp.where(mask, s, -jnp.inf)
    m = jnp.max(s, axis=-1, keepdims=True)
    m = jnp.where(m > -jnp.inf, m, 0.0)
    p = jnp.exp2(s - m)
    return p * (1.0 / jnp.maximum(jnp.sum(p, axis=-1, keepdims=True), 1e-30))


def _group_mean_sq(x):
    x2 = x * x
    hi = x2.astype(BF16)
    lo = (x2 - hi.astype(F32)).astype(BF16)
    same = ((_iota((2 * LANES, LANES), 0) >> 6) & 1) == (_iota((2 * LANES, LANES), 1) >> 6)
    ones = jnp.where(same, 1.0, 0.0).astype(BF16)
    return _dot(jnp.concatenate([hi, lo], axis=1), ones) * (1.0 / HEAD_DIM)


def _head_norm_rope(x, gain, cos2, sin2):
    xn = x * lax.rsqrt(_group_mean_sq(x) + NORM_EPS) * gain
    lane = _iota(x.shape, 1)
    swapped = jnp.where((lane & (HEAD_DIM - 1)) < HEAD_DIM // 2,
                        pltpu.roll(xn, LANES - HEAD_DIM // 2, 1), pltpu.roll(xn, HEAD_DIM // 2, 1))
    return xn * cos2 + swapped * sin2


def _low_half(x, other=0.0):
    lane = _iota(x.shape, 1)
    return jnp.where(lane < HEAD_DIM, x, other)


def _swap_halves(x):
    return pltpu.roll(x, HEAD_DIM, 1)


def _normed_input(x, norm_g):
    ms = jnp.mean(x * x, axis=-1, keepdims=True)
    return (x * lax.rsqrt(ms + NORM_EPS) * norm_g).astype(BF16)


def _silu(z):
    return z * jax.nn.sigmoid(z)


def _prompt_in_kernel(x_ref, ng_ref, w_ref, cw_ref, cb_ref, qg_ref, kg_ref, cos_ref, sin_ref,
                      convb_ref, sz_ref, q_ref, kvc_ref, kvs_ref, kvw_ref, ksa_ref, vsa_ref, kw_ref, vw_ref,
                      gate_ref, utail_ref, carry_ref, *, tm):
    ti = pl.program_id(1)
    h = _normed_input(x_ref[...], ng_ref[...])
    cos2 = cos_ref[...]
    sin2 = sin_ref[...]

    u = _dot(h, w_ref[:, CONV_CH:2 * CONV_CH]) * _dot(h, w_ref[:, 2 * CONV_CH:3 * CONV_CH])

    @pl.when(ti == 0)
    def _():
        carry_ref[...] = jnp.zeros_like(carry_ref)

    prev1 = carry_ref[7:8, :]
    prev2 = carry_ref[6:7, :]
    row = _iota(u.shape, 0)
    u1 = jnp.where(row == 0, prev1, pltpu.roll(u, 1, 0))
    u2 = jnp.where(row == 0, prev2, jnp.where(row == 1, prev1, pltpu.roll(u, 2, 0)))
    conv_y = cw_ref[0:1, :] * u2 + cw_ref[1:2, :] * u1 + cw_ref[2:3, :] * u + cb_ref[...]
    b_gate = _dot(h, w_ref[:, 0:CONV_CH])
    z_conv = _dot(h, w_ref[:, 3 * CONV_CH:4 * CONV_CH])
    convb_ref[...] = (b_gate * conv_y * _silu(z_conv)).astype(BF16)
    carry_ref[...] = u[tm - 8:tm, :]
    utail_ref[...] = u[tm - 8:tm, :]

    pq = _dot(h, w_ref[:, C_Q:C_KV])
    for c in range(N_HEADS // 2):
        qc = _head_norm_rope(pq[:, c * LANES:(c + 1) * LANES], qg_ref[...], cos2, sin2) * Q_SCALE
        for half in range(2):
            head = 2 * c + half
            src = qc if half == 0 else _swap_halves(qc)
            q_ref[head // GROUP, head % GROUP] = _low_half(src).astype(BF16)

    pk = _dot(h, w_ref[:, C_KV:C_Z])
    lane = _iota((tm, LANES), 1)
    tpos = ti * tm + _iota((tm, LANES), 0)
    onehot = jnp.where(((tpos >> 6) & (SEL_CHUNK - 1)) == lane - HEAD_DIM, 1.0, 0.0)
    ones_col = jnp.where(lane == HEAD_DIM, 1.0, 0.0)
    for br, out_ref in enumerate((kvc_ref, kvs_ref, kvw_ref)):
        kp = _head_norm_rope(pk[:, br * KV_LANES:br * KV_LANES + LANES], kg_ref[br:br + 1, :], cos2, sin2)
        vv = pk[:, br * KV_LANES + LANES:(br + 1) * KV_LANES]
        out_ref[:, 0:LANES] = kp
        out_ref[:, LANES:KV_LANES] = vv
        if br == 1:
            for g in range(N_KV):
                ksa_ref[g] = _low_half(kp if g == 0 else _swap_halves(kp), onehot).astype(BF16)
                vsa_ref[g] = _low_half(vv if g == 0 else _swap_halves(vv), ones_col).astype(BF16)
        if br == 2:
            for g in range(N_KV):
                kw_ref[g] = _low_half(kp if g == 0 else _swap_halves(kp)).astype(BF16)
                vw_ref[g] = _low_half(vv if g == 0 else _swap_halves(vv), ones_col).astype(BF16)

    sz_ref[...] = _silu(_dot(h, w_ref[:, C_Z:C_G]))
    sg = jax.nn.sigmoid(_dot(h, w_ref[:, C_G:W_COLS]))
    gate_ref[0] = sg
    gate_ref[1] = pltpu.roll(sg, LANES - GROUP * N_BRANCH, 1)


def _prompt_in(x, ng, w_all, cw, cb, qg, kg, cos2, sin2, tm):
    b, t, _ = x.shape
    grid = (b, t // tm)
    row_blk = lambda w: pl.BlockSpec((None, tm, w), lambda bi, ti: (bi, ti, 0))
    full = lambda a: pl.BlockSpec(a.shape, lambda bi, ti: (0,) * a.ndim)
    head_blk = pl.BlockSpec((None, N_KV, tm, LANES), lambda bi, ti: (bi, 0, ti, 0))
    out_shape = (
        jax.ShapeDtypeStruct((b, t, CONV_CH), BF16),
        jax.ShapeDtypeStruct((b, t, ATTN_W), F32),
        jax.ShapeDtypeStruct((b, N_KV, GROUP, t, LANES), BF16),
        jax.ShapeDtypeStruct((b, t, KV_LANES), F32),
        jax.ShapeDtypeStruct((b, t, KV_LANES), F32),
        jax.ShapeDtypeStruct((b, t, KV_LANES), F32),
        jax.ShapeDtypeStruct((b, N_KV, t, LANES), BF16),
        jax.ShapeDtypeStruct((b, N_KV, t, LANES), BF16),
        jax.ShapeDtypeStruct((b, N_KV, t, LANES), BF16),
        jax.ShapeDtypeStruct((b, N_KV, t, LANES), BF16),
        jax.ShapeDtypeStruct((b, N_KV, t, LANES), F32),
        jax.ShapeDtypeStruct((b, 8, CONV_CH), F32),
    )
    out_specs = (
        row_blk(CONV_CH), row_blk(ATTN_W),
        pl.BlockSpec((None, N_KV, GROUP, tm, LANES), lambda bi, ti: (bi, 0, 0, ti, 0)),
        row_blk(KV_LANES), row_blk(KV_LANES), row_blk(KV_LANES),
        head_blk, head_blk, head_blk, head_blk, head_blk,
        pl.BlockSpec((None, 8, CONV_CH), lambda bi, ti: (bi, 0, 0)),
    )
    tab = pl.BlockSpec((tm, LANES), lambda bi, ti: (ti, 0))
    return pl.pallas_call(
        functools.partial(_prompt_in_kernel, tm=tm),
        grid=grid,
        in_specs=[row_blk(D_MODEL), full(ng), full(w_all), full(cw), full(cb), full(qg), full(kg), tab, tab],
        out_specs=out_specs,
        out_shape=out_shape,
        scratch_shapes=[pltpu.VMEM((8, CONV_CH), F32)],
        compiler_params=pltpu.CompilerParams(dimension_semantics=("arbitrary", "arbitrary"),
                                             vmem_limit_bytes=VMEM_LIMIT),
        name="prompt_in_proj",
    )(x, ng, w_all, cw, cb, qg, kg, cos2, sin2)


def _compress_rows(xk_ref, xv_ref, pe_ref, wbd_ref, acc_ref, n_even):
    acc_ref[...] = jnp.zeros_like(acc_ref)

    def rows(ref, r):
        even = ref[pl.ds(r, n_even, stride=2 * CMP_BLK), :]
        odd = ref[pl.ds(CMP_BLK + r, n_even, stride=2 * CMP_BLK), :]
        return jnp.concatenate([even, odd], axis=0)

    def body(r, carry):
        xr = jnp.concatenate([rows(xk_ref, r), rows(xv_ref, r)], axis=1) + pe_ref[pl.ds(r, 1), :]
        acc_ref[...] += _dot(xr.astype(BF16), wbd_ref[r])
        return carry

    lax.fori_loop(0, CMP_BLK, body, 0)


def _prompt_compress_kernel(xk_ref, xv_ref, pe_ref, wbd_ref, kc_ref, vc_ref, acc_ref, *, n_even):
    _compress_rows(xk_ref, xv_ref, pe_ref, wbd_ref, acc_ref, n_even)
    kk = acc_ref[:, 0:LANES]
    vv = acc_ref[:, LANES:KV_LANES]
    for g in range(N_KV):
        kc_ref[g] = _low_half(kk if g == 0 else _swap_halves(kk)).astype(BF16)
        vc_ref[g] = _low_half(vv if g == 0 else _swap_halves(vv)).astype(BF16)


def _prompt_compress(kvc, pe_row, wbd):
    b, t, _ = kvc.shape
    nc = t // CMP_BLK
    out = jax.ShapeDtypeStruct((b, N_KV, nc, LANES), BF16)
    blk = pl.BlockSpec((None, N_KV, nc, LANES), lambda bi: (bi, 0, 0, 0))
    return pl.pallas_call(
        functools.partial(_prompt_compress_kernel, n_even=nc // 2),
        grid=(b,),
        in_specs=[pl.BlockSpec((None, t, LANES), lambda bi: (bi, 0, 0)),
                  pl.BlockSpec((None, t, LANES), lambda bi: (bi, 0, 1)),
                  pl.BlockSpec(pe_row.shape, lambda bi: (0, 0)),
                  pl.BlockSpec(wbd.shape, lambda bi: (0, 0, 0))],
        out_specs=(blk, blk),
        out_shape=(out, out),
        scratch_shapes=[pltpu.VMEM((nc, KV_LANES), F32)],
        compiler_params=pltpu.CompilerParams(dimension_semantics=("arbitrary",), vmem_limit_bytes=VMEM_LIMIT),
        name="prompt_compress",
    )(kvc, kvc, pe_row, wbd)


def _select_blocks(score, blk_f, n_rounds):
    sel = jnp.zeros(score.shape, F32)
    picks = []
    for _ in range(n_rounds):
        m = jnp.max(score, axis=-1, keepdims=True)
        first = jnp.min(jnp.where(score == m, blk_f, 1e9), axis=-1, keepdims=True)
        hit = blk_f == first
        sel = jnp.where(hit, 1.0, sel)
        score = jnp.where(hit, -jnp.inf, score)
        picks.append(first)
    return sel, picks


def _selection_bias(imp, qs):
    tq, ns = imp.shape
    n_chunks = tq // LANES
    imp_t = jnp.concatenate([imp[r * LANES:(r + 1) * LANES, :].T for r in range(n_chunks)], axis=1)
    blk = _iota((ns, tq), 0)
    qp = qs + _iota((1, tq), 1)
    qb = qp >> 6
    valid = blk * SEL_BLK <= qp
    forced = ((blk == 0) | (blk == qb) | (blk == qb - 1)) & valid
    blk_f = blk.astype(F32)
    candidate = valid & jnp.logical_not(forced)
    score = jnp.where(candidate, imp_t, -jnp.inf)
    for _ in range(min(N_SEL, ns) - 3):
        m = jnp.max(score, axis=0, keepdims=True)
        first = jnp.min(jnp.where(score == m, blk_f, 1e9), axis=0, keepdims=True)
        score = jnp.where(blk_f == first, -jnp.inf, score)
    bias_t = jnp.where(forced | (candidate & (score == -jnp.inf)), 0.0, NEG_BIAS)
    return jnp.concatenate([bias_t[:, r * LANES:(r + 1) * LANES].T for r in range(n_chunks)], axis=0)


def _prompt_attn_kernel(q_ref, kc_ref, vc_ref, ksa_ref, vsa_ref, kw_ref, vw_ref, gate_ref, sz_ref, out_ref,
                        qa_ref, sa_ref, sb_ref, m_ref, acc_ref, *, tq, tk, t):
    i = pl.program_id(2)
    qs = i * tq
    rows = GROUP * tq
    nc = t // CMP_BLK
    ns = nc // 2
    q = q_ref[...].reshape(rows, LANES)
    qpos = qs + (_iota((rows, 1), 0) & (tq - 1))

    col = _iota((1, nc), 1)
    cblk = jnp.where(col < ns, 2 * col, 2 * (col - ns) + 1)
    s_cmp = jnp.where((cblk + 1) * CMP_BLK - 1 <= qpos, _dot_nt(q, kc_ref[...]), -jnp.inf)
    m_cmp = jnp.max(s_cmp, axis=-1, keepdims=True)
    e_cmp = jnp.exp2(s_cmp - jnp.where(m_cmp > -jnp.inf, m_cmp, 0.0))
    inv_cmp = 1.0 / jnp.maximum(jnp.sum(e_cmp, axis=-1, keepdims=True), 1e-30)
    o_cmp = _dot(e_cmp.astype(BF16), vc_ref[...]) * inv_cmp

    pair = (e_cmp[:, 0:ns] + e_cmp[:, ns:nc]) * inv_cmp
    imp = pair[0:tq]
    for n in range(1, GROUP):
        imp = imp + pair[n * tq:(n + 1) * tq]
    bias = _selection_bias(imp, qs)

    for c in range(ns // SEL_CHUNK):
        bc = bias[:, (c // 2) * LANES:(c // 2 + 1) * LANES]
        if c % 2 == 0:
            bc = _swap_halves(bc)
        bc = jnp.concatenate([bc] * GROUP, axis=0).astype(BF16)
        qa_ref[c] = _low_half(q, bc)

    q_win = jnp.where(_iota((rows, LANES), 1) == HEAD_DIM, NEG_BIAS, q.astype(F32)).astype(BF16)
    r0 = _iota((GROUP * LANES, 1), 0) & (LANES - 1)
    upper = _iota((1, LANES), 1) > r0
    n_mid = WINDOW // LANES - 1
    o_win_blocks = []
    for a in range(tq // LANES):
        q_a = jnp.concatenate([q_win[n * tq + a * LANES:n * tq + (a + 1) * LANES] for n in range(GROUP)], axis=0)
        base = pl.multiple_of(qs + a * LANES, LANES)
        s_a = _dot_nt(q_a, kw_ref[pl.ds(base, WINDOW + LANES), :])
        s_a = jnp.concatenate([jnp.where(upper, s_a[:, 0:LANES], NEG_MASK), s_a[:, LANES:(n_mid + 1) * LANES],
                               jnp.where(upper, NEG_MASK, s_a[:, (n_mid + 1) * LANES:])], axis=1)
        e_a = jnp.exp2(s_a - jnp.max(s_a, axis=-1, keepdims=True))
        acc_a = _dot(e_a.astype(BF16), vw_ref[pl.ds(base, WINDOW + LANES), :])
        o_win_blocks.append(_low_half(acc_a * (1.0 / acc_a[:, HEAD_DIM:HEAD_DIM + 1])))
    o_win = jnp.concatenate([blk_a[n * LANES:(n + 1) * LANES] for n in range(GROUP) for blk_a in o_win_blocks], axis=0)

    m_ref[...] = jnp.full_like(m_ref, NEG_MASK)
    acc_ref[...] = jnp.zeros_like(acc_ref)

    def scores(kt):
        k0 = pl.multiple_of(kt * tk, tk)
        return _dot_nt(qa_ref[kt // (SEL_CHUNK * SEL_BLK // tk)], ksa_ref[pl.ds(k0, tk), :])

    def update(s, kt, causal):
        k0 = pl.multiple_of(kt * tk, tk)
        if causal:
            s = jnp.where(k0 + _iota((1, tk), 1) <= qpos, s, NEG_MASK)
        m_old = m_ref[...]
        m_new = jnp.maximum(m_old, jnp.broadcast_to(jnp.max(s, axis=-1, keepdims=True), m_old.shape))
        p = jnp.exp2(s - jnp.concatenate([m_new] * (tk // LANES), axis=1))
        acc_ref[...] = jnp.exp2(m_old - m_new) * acc_ref[...] + _dot(p.astype(BF16), vsa_ref[pl.ds(k0, tk), :])
        m_ref[...] = m_new

    n_kt = (qs + tq + tk - 1) // tk
    n_pairs = (n_kt - 1) // 2
    sa_ref[...] = scores(0)

    def pair_step(j, carry):
        sb_ref[...] = scores(2 * j + 1)
        update(sa_ref[...], 2 * j, False)
        sa_ref[...] = scores(2 * j + 2)
        update(sb_ref[...], 2 * j + 1, False)
        return carry

    lax.fori_loop(0, n_pairs, pair_step, 0)
    last = n_kt - 1

    @pl.when(n_kt - 2 * n_pairs == 1)
    def _():
        update(sa_ref[...], last, True)

    @pl.when(n_kt - 2 * n_pairs == 2)
    def _():
        sb_ref[...] = scores(last)
        update(sa_ref[...], last - 1, False)
        update(sb_ref[...], last, True)

    acc = acc_ref[...]
    o_slc = _low_half(acc * (1.0 / acc[:, HEAD_DIM:HEAD_DIM + 1]))

    gate = gate_ref[...]
    mixed = []
    for n in range(GROUP):
        r = slice(n * tq, (n + 1) * tq)
        mixed.append(gate[:, 3 * n:3 * n + 1] * o_cmp[r] + gate[:, 3 * n + 1:3 * n + 2] * o_slc[r]
                     + gate[:, 3 * n + 2:3 * n + 3] * o_win[r])
    for c in range(GROUP // 2):
        o_pair = mixed[2 * c] + _swap_halves(mixed[2 * c + 1])
        out_ref[:, c * LANES:(c + 1) * LANES] = (o_pair * sz_ref[:, c * LANES:(c + 1) * LANES]).astype(BF16)


def _prompt_attn(q, kc, vc, ksa, vsa, kw, vw, gate, sz, tq, tk):
    b, _, _, t, _ = q.shape
    assert t % (2 * SEL_CHUNK * SEL_BLK) == 0 and t >= WINDOW + tq and (SEL_CHUNK * SEL_BLK) % tk == 0
    assert tq % LANES == 0 and N_SEL > 3
    nc = t // CMP_BLK
    rows = GROUP * tq
    seq = lambda n: pl.BlockSpec((None, None, n, LANES), lambda bi, g, i: (bi, g, 0, 0))
    return pl.pallas_call(
        functools.partial(_prompt_attn_kernel, tq=tq, tk=tk, t=t),
        grid=(b, N_KV, t // tq),
        in_specs=[pl.BlockSpec((None, None, GROUP, tq, LANES), lambda bi, g, i: (bi, g, 0, i, 0)),
                  seq(nc), seq(nc), seq(t), seq(t), seq(t + WINDOW), seq(t + WINDOW),
                  pl.BlockSpec((None, None, tq, LANES), lambda bi, g, i: (bi, g, i, 0)),
                  pl.BlockSpec((None, tq, GROUP * HEAD_DIM), lambda bi, g, i: (bi, i, g))],
        out_specs=pl.BlockSpec((None, tq, GROUP * HEAD_DIM), lambda bi, g, i: (bi, i, g)),
        out_shape=jax.ShapeDtypeStruct((b, t, ATTN_W), BF16),
        scratch_shapes=[pltpu.VMEM((t // (SEL_CHUNK * SEL_BLK), rows, LANES), BF16),
                        pltpu.VMEM((rows, tk), F32), pltpu.VMEM((rows, tk), F32),
                        pltpu.VMEM((rows, LANES), F32), pltpu.VMEM((rows, LANES), F32)],
        compiler_params=pltpu.CompilerParams(dimension_semantics=("arbitrary", "arbitrary", "arbitrary"),
                                             vmem_limit_bytes=VMEM_LIMIT),
        name="prompt_attention",
    )(q, kc, vc, ksa, vsa, kw, vw, gate, sz)


def _out_proj_kernel(x_ref, convb_ref, attnb_ref, w_ref, y_ref):
    mix = jnp.concatenate([convb_ref[...], attnb_ref[...]], axis=1)
    y_ref[...] = x_ref[...] + _dot(mix, w_ref[...])


def _out_proj(x2d, convb, attnb, w_out, tm):
    m = x2d.shape[0]
    blk = lambda w: pl.BlockSpec((tm, w), lambda i: (i, 0))
    return pl.pallas_call(
        _out_proj_kernel,
        grid=(m // tm,),
        in_specs=[blk(D_MODEL), blk(CONV_CH), blk(ATTN_W), pl.BlockSpec(w_out.shape, lambda i: (0, 0))],
        out_specs=blk(D_MODEL),
        out_shape=jax.ShapeDtypeStruct((m, D_MODEL), F32),
        compiler_params=pltpu.CompilerParams(dimension_semantics=("arbitrary",), vmem_limit_bytes=VMEM_LIMIT),
        name="out_proj",
    )(x2d, convb, attnb, w_out)


def _sample_in_kernel(x_ref, ng_ref, w_ref, cw_ref, cb_ref, qg_ref, kg_ref, cos_ref, sin_ref, c0_ref, c1_ref,
                      convb_ref, sz_ref, qbd_ref, kvn_ref, gate_ref, u_ref):
    h = _normed_input(x_ref[...], ng_ref[...])
    cos2 = cos_ref[...]
    sin2 = sin_ref[...]
    pc = _dot(h, w_ref[:, C_CONV:C_Q])
    b_gate = pc[:, 0:CONV_CH]
    u = pc[:, CONV_CH:2 * CONV_CH] * pc[:, 2 * CONV_CH:3 * CONV_CH]
    z_conv = pc[:, 3 * CONV_CH:4 * CONV_CH]
    conv_y = cw_ref[0:1, :] * c0_ref[...] + cw_ref[1:2, :] * c1_ref[...] + cw_ref[2:3, :] * u + cb_ref[...]
    convb_ref[...] = (b_gate * conv_y * _silu(z_conv)).astype(BF16)
    u_ref[...] = u

    pq = _dot(h, w_ref[:, C_Q:C_KV])
    zeros = jnp.zeros((x_ref.shape[0], LANES), BF16)
    for c in range(N_HEADS // 2):
        qc = _head_norm_rope(pq[:, c * LANES:(c + 1) * LANES], qg_ref[...], cos2, sin2) * Q_SCALE
        lane = _iota(qc.shape, 1)
        for half in range(2):
            head = 2 * c + half
            g = head // GROUP
            src = qc if half == g else _swap_halves(qc)
            keep = (lane >= g * HEAD_DIM) & (lane < (g + 1) * HEAD_DIM)
            qbd_ref[head] = jnp.concatenate([jnp.where(keep, src, 0.0).astype(BF16), zeros], axis=1)

    pk = _dot(h, w_ref[:, C_KV:C_Z])
    for br in range(N_BRANCH):
        kvn_ref[:, br * KV_LANES:br * KV_LANES + LANES] = _head_norm_rope(
            pk[:, br * KV_LANES:br * KV_LANES + LANES], kg_ref[br:br + 1, :], cos2, sin2)
        kvn_ref[:, br * KV_LANES + LANES:(br + 1) * KV_LANES] = pk[:, br * KV_LANES + LANES:(br + 1) * KV_LANES]

    sz_ref[...] = _silu(_dot(h, w_ref[:, C_Z:C_G]))
    gate_ref[...] = jax.nn.sigmoid(_dot(h, w_ref[:, C_G:W_COLS]))


def _sample_in(x, ng, w_all, cw, cb, qg, kg, cos1, sin1, c0, c1):
    nb = x.shape[0]
    out_shape = (
        jax.ShapeDtypeStruct((nb, CONV_CH), BF16),
        jax.ShapeDtypeStruct((nb, ATTN_W), F32),
        jax.ShapeDtypeStruct((N_HEADS, nb, 2 * LANES), BF16),
        jax.ShapeDtypeStruct((nb, N_BRANCH * KV_LANES), F32),
        jax.ShapeDtypeStruct((nb, LANES), F32),
        jax.ShapeDtypeStruct((nb, CONV_CH), F32),
    )
    return pl.pallas_call(
        _sample_in_kernel,
        out_shape=out_shape,
        compiler_params=pltpu.CompilerParams(vmem_limit_bytes=VMEM_LIMIT),
        name="sample_in_proj",
    )(x, ng, w_all, cw, cb, qg, kg, cos1, sin1, c0, c1)


def _sample_cmp_kernel(pt_ref, qbd_ref, cache_ref, pet_ref, perm_ref, wbd_ref, ocmp_ref, imp_ref,
                       buf_ref, rows_ref, acc_ref, sem, *, n_batch, n_pages, past):
    b = pl.program_id(0)
    n_chunks = n_pages // PAGES_PER_CHUNK
    blocks_per_pair = 2 * PAGE // CMP_BLK

    def page_copy(bb, ch, p, slot):
        return pltpu.make_async_copy(cache_ref.at[pt_ref[bb, ch * PAGES_PER_CHUNK + p]], buf_ref.at[slot, p],
                                     sem.at[slot])

    def start_chunk(bb, ch, slot):
        def body(p, carry):
            page_copy(bb, ch, p, slot).start()
            return carry
        lax.fori_loop(0, PAGES_PER_CHUNK, body, 0)

    def wait_chunk(ch, slot):
        def body(p, carry):
            page_copy(b, ch, p, slot).wait()
            return carry
        lax.fori_loop(0, PAGES_PER_CHUNK, body, 0)

    def stream_slot(offset):
        return (b * n_chunks + offset) % CHUNK_SLOTS

    def start_ahead(offset):
        bb_off, ch = divmod(offset, n_chunks)
        if bb_off == 0:
            start_chunk(b, ch, stream_slot(offset))
        else:
            @pl.when(b + bb_off < n_batch)
            def _():
                start_chunk(b + bb_off, ch, stream_slot(offset))

    @pl.when(b == 0)
    def _():
        for offset in range(CHUNK_SLOTS - 1):
            start_ahead(offset)

    for ch in range(n_chunks):
        slot = stream_slot(ch)
        start_ahead(ch + CHUNK_SLOTS - 1)
        wait_chunk(ch, slot)

        def pairs_body(it, carry):
            for k in range(PAIR_UNROLL):
                pr = it * PAIR_UNROLL + k
                xt = jnp.concatenate([buf_ref[slot, 2 * pr], buf_ref[slot, 2 * pr + 1]], axis=1) + pet_ref[...]
                x_perm = _dot_nt(perm_ref[...], xt.astype(BF16))
                base = pl.multiple_of((ch * (PAGES_PER_CHUNK // 2) + pr) * blocks_per_pair, blocks_per_pair)
                for r in range(CMP_BLK):
                    rows_ref[r, pl.ds(base, blocks_per_pair), :] = (
                        x_perm[r * blocks_per_pair:(r + 1) * blocks_per_pair, :])
            return carry

        lax.fori_loop(0, PAGES_PER_CHUNK // 2 // PAIR_UNROLL, pairs_body, 0)

    acc_ref[...] = jnp.zeros_like(acc_ref)

    def compress_body(it, carry):
        acc = acc_ref[...]
        for k in range(PAIR_UNROLL):
            r = it * PAIR_UNROLL + k
            acc = acc + _dot(rows_ref[r].astype(BF16), wbd_ref[r])
        acc_ref[...] = acc
        return carry

    lax.fori_loop(0, CMP_BLK // PAIR_UNROLL, compress_body, 0)

    nc = past // CMP_BLK
    kv = acc_ref[...].astype(BF16)
    qbd = qbd_ref[...]
    p_cmp = _masked_softmax(_dot_nt(qbd, kv), (_iota((1, nc), 1) + 1) * CMP_BLK - 1 <= past)
    ocmp_ref[...] = _dot(p_cmp.astype(BF16), kv)

    pair = p_cmp + pltpu.roll(p_cmp, nc - 1, 1)
    row = _iota((N_HEADS, nc), 0)
    imp_ref[...] = jnp.where(row == 0, jnp.sum(jnp.where(row < GROUP, pair, 0.0), axis=0, keepdims=True),
                             jnp.sum(jnp.where(row >= GROUP, pair, 0.0), axis=0, keepdims=True))


def _sample_cmp(page_table, qbd, cache_t, pe_t, perm, wbd, past):
    nb, n_pages = page_table.shape
    assert n_pages % PAGES_PER_CHUNK == 0 and n_pages // PAGES_PER_CHUNK >= CHUNK_SLOTS - 1
    nc = past // CMP_BLK
    const = lambda a: pl.BlockSpec(a.shape, lambda b, pt: (0,) * a.ndim)
    grid_spec = pltpu.PrefetchScalarGridSpec(
        num_scalar_prefetch=1,
        grid=(nb,),
        in_specs=[pl.BlockSpec((None, N_HEADS, 2 * LANES), lambda b, pt: (b, 0, 0)),
                  pl.BlockSpec(memory_space=pl.ANY), const(pe_t), const(perm), const(wbd)],
        out_specs=(pl.BlockSpec((None, N_HEADS, KV_LANES), lambda b, pt: (b, 0, 0)),
                   pl.BlockSpec((None, N_HEADS, nc), lambda b, pt: (b, 0, 0))),
        scratch_shapes=[pltpu.VMEM((CHUNK_SLOTS, PAGES_PER_CHUNK, KV_LANES, PAGE), F32),
                        pltpu.VMEM((CMP_BLK, nc, KV_LANES), F32),
                        pltpu.VMEM((nc, KV_LANES), F32), pltpu.SemaphoreType.DMA((CHUNK_SLOTS,))],
    )
    return pl.pallas_call(
        functools.partial(_sample_cmp_kernel, n_batch=nb, n_pages=n_pages, past=past),
        grid_spec=grid_spec,
        out_shape=(jax.ShapeDtypeStruct((nb, N_HEADS, KV_LANES), F32),
                   jax.ShapeDtypeStruct((nb, N_HEADS, nc), F32)),
        compiler_params=pltpu.CompilerParams(dimension_semantics=("arbitrary",), vmem_limit_bytes=VMEM_LIMIT),
        name="sample_compress",
    )(page_table, qbd, cache_t, pe_t, perm, wbd)


def _sample_select_kernel(imp_ref, idx_ref, *, past):
    imp = imp_ref[...]
    lane = _iota(imp.shape, 1)
    blk = lane >> 1
    qb = past // SEL_BLK
    forced = (blk == 0) | (blk == qb) | (blk == qb - 1)
    valid = ((lane & 1) == 0) & (blk * SEL_BLK <= past)
    score = jnp.where(valid, imp + jnp.where(forced, FORCE_BONUS, 0.0), -jnp.inf)
    _, picks = _select_blocks(score, lane.astype(F32), N_SEL - 1)
    out_lane = _iota(idx_ref.shape, 1)
    idx = jnp.zeros(idx_ref.shape, F32)
    for k, pick in enumerate(picks):
        idx = jnp.where(out_lane == k, pick * 0.5, idx)
    idx_ref[...] = idx.astype(jnp.int32)


def _sample_select(imp, past):
    return pl.pallas_call(
        functools.partial(_sample_select_kernel, past=past),
        out_shape=jax.ShapeDtypeStruct((imp.shape[0], LANES), jnp.int32),
        compiler_params=pltpu.CompilerParams(vmem_limit_bytes=VMEM_LIMIT),
        name="sample_select",
    )(imp)


def _sample_attn_kernel(pt_ref, sel_ref, qbd_ref, cache_ref, kvn_ref, win_ref, ocmp_ref, gate_ref, out_ref,
                        buf_ref, sem, *, n_batch, past):
    b = pl.program_id(0)
    n_pick = N_SEL - 1
    blocks_per_page = PAGE // SEL_BLK
    slot = b % 2

    def page_copy(bb, j, s):
        return pltpu.make_async_copy(cache_ref.at[pt_ref[bb, sel_ref[bb, j] >> 1]], buf_ref.at[s, j], sem.at[s])

    @pl.when(b == 0)
    def _():
        for j in range(N_KV * n_pick):
            page_copy(0, j, 0).start()

    @pl.when(b + 1 < n_batch)
    def _():
        for j in range(N_KV * n_pick):
            page_copy(b + 1, j, 1 - slot).start()

    qbd = qbd_ref[...]
    qf = qbd.astype(F32)
    row = _iota((N_HEADS, 1), 0)

    def new_token(offset):
        kv_new = kvn_ref[:, offset:offset + KV_LANES].astype(BF16).astype(F32)
        return jnp.sum(qf * kv_new, axis=-1, keepdims=True), kv_new

    def attend(keys_t, mask, s_new, kv_new):
        s = jnp.where(mask, _dot(qbd, keys_t), -jnp.inf)
        m = jnp.maximum(jnp.max(s, axis=-1, keepdims=True), s_new)
        p = jnp.exp2(s - m)
        p_new = jnp.exp2(s_new - m)
        norm = 1.0 / (jnp.sum(p, axis=-1, keepdims=True) + p_new)
        return (_dot_nt(p.astype(BF16), keys_t) + p_new.astype(BF16).astype(F32) * kv_new) * norm

    w_buf = win_ref.shape[1]
    s_new, kv_new = new_token(2 * KV_LANES)
    x_win = attend(win_ref[...].astype(BF16), _iota((1, w_buf), 1) > w_buf - WINDOW, s_new, kv_new)

    for j in range(N_KV * n_pick):
        page_copy(b, j, slot).wait()

    s_new, kv_new = new_token(KV_LANES)
    lane = _iota((1, n_pick * PAGE), 1)
    x_slc = []
    for g in range(N_KV):
        slots = range(g * n_pick, (g + 1) * n_pick)
        keys_t = jnp.concatenate([buf_ref[slot, j] for j in slots], axis=1).astype(BF16)
        half = jnp.concatenate([jnp.full((1, PAGE), sel_ref[b, j] & (blocks_per_page - 1), jnp.int32) for j in slots],
                               axis=1)
        x_slc.append(attend(keys_t, ((lane & (PAGE - 1)) >> (SEL_BLK.bit_length() - 1)) == half, s_new, kv_new))
    x_slc = jnp.where(row < GROUP, x_slc[0], x_slc[1])

    gate = jnp.broadcast_to(gate_ref[...], (N_HEADS, LANES))
    lane = _iota((N_HEADS, LANES), 1)
    mixed = jnp.zeros((N_HEADS, KV_LANES), F32)
    for br, x_br in enumerate((ocmp_ref[...], x_slc, x_win)):
        g_col = jnp.sum(jnp.where(lane == N_BRANCH * _iota((N_HEADS, LANES), 0) + br, gate, 0.0), axis=-1, keepdims=True)
        mixed = mixed + g_col * x_br
    v_lo = mixed[:, 2 * HEAD_DIM:3 * HEAD_DIM]
    v_hi = mixed[:, 3 * HEAD_DIM:4 * HEAD_DIM]
    out_ref[...] = jnp.where(row < GROUP, v_lo, v_hi)


def _sample_attn(page_table, sel, qbd, cache, kvn, win, ocmp, gate, past):
    nb = page_table.shape[0]
    n_pick = N_SEL - 1
    per_b = lambda *shape: pl.BlockSpec((None,) + shape, lambda b, pt, s: (b,) + (0,) * len(shape))
    grid_spec = pltpu.PrefetchScalarGridSpec(
        num_scalar_prefetch=2,
        grid=(nb,),
        in_specs=[per_b(N_HEADS, 2 * LANES),
                  pl.BlockSpec(memory_space=pl.ANY),
                  per_b(1, N_BRANCH * KV_LANES),
                  per_b(KV_LANES, win.shape[2]),
                  per_b(N_HEADS, KV_LANES),
                  per_b(1, LANES)],
        out_specs=per_b(N_HEADS, HEAD_DIM),
        scratch_shapes=[pltpu.VMEM((2, N_KV * n_pick, KV_LANES, PAGE), F32), pltpu.SemaphoreType.DMA((2,))],
    )
    return pl.pallas_call(
        functools.partial(_sample_attn_kernel, n_batch=nb, past=past),
        grid_spec=grid_spec,
        out_shape=jax.ShapeDtypeStruct((nb, N_HEADS, HEAD_DIM), F32),
        compiler_params=pltpu.CompilerParams(dimension_semantics=("arbitrary",), vmem_limit_bytes=VMEM_LIMIT),
        name="sample_attention",
    )(page_table, sel, qbd, cache, kvn, win, ocmp, gate)


def _sample_out_kernel(x_ref, convb_ref, attn_ref, sz_ref, w_ref, y_ref):
    mix = jnp.concatenate([convb_ref[...], (attn_ref[...] * sz_ref[...]).astype(BF16)], axis=1)
    y_ref[...] = x_ref[...] + _dot(mix, w_ref[...])


def _sample_out(x, convb, attn, sz, w_out):
    return pl.pallas_call(
        _sample_out_kernel,
        out_shape=jax.ShapeDtypeStruct(x.shape, F32),
        compiler_params=pltpu.CompilerParams(vmem_limit_bytes=VMEM_LIMIT),
        name="sample_out_proj",
    )(x, convb, attn, sz, w_out)


def _rope_tables(pos):
    half = HEAD_DIM // 2
    inv = ROPE_THETA ** (-jnp.arange(half, dtype=F32) / half)
    ang = pos.astype(F32)[:, None] * inv[None, :]
    cos, sin = lax.optimization_barrier((jnp.cos(ang), jnp.sin(ang)))
    return jnp.concatenate([cos, cos, cos, cos], axis=1), jnp.concatenate([-sin, sin, -sin, sin], axis=1)


def _layer_params(norm_g, w_in, conv_w, conv_b, q_gain, k_gain, cmp_pe, cmp_w, w_out):
    c_gates = C_KV + N_BRANCH * KV_LANES
    n_gates = N_HEADS * N_BRANCH
    w_all = jnp.concatenate([w_in[:, :c_gates], w_in[:, c_gates + n_gates:], w_in[:, c_gates:c_gates + n_gates],
                             jnp.zeros((D_MODEL, LANES - n_gates), w_in.dtype)], axis=1).astype(BF16)
    wbd = jnp.zeros((CMP_BLK, KV_LANES, KV_LANES), BF16)
    for j in range(2):
        for g in range(N_KV):
            o = (j * N_KV + g) * HEAD_DIM
            wbd = wbd.at[:, o:o + HEAD_DIM, o:o + HEAD_DIM].set(cmp_w[:, j].astype(BF16))
    pe_row = jnp.broadcast_to(cmp_pe[:, :, None, :], (CMP_BLK, 2, N_KV, HEAD_DIM)).reshape(CMP_BLK, KV_LANES)
    blocks_per_pair = 2 * PAGE // CMP_BLK
    pe_t = jnp.tile(pe_row.T, (1, blocks_per_pair))
    m = jnp.arange(2 * PAGE)
    perm = (m[None, :] == (CMP_BLK * (m % blocks_per_pair) + m // blocks_per_pair)[:, None]).astype(BF16)
    return dict(
        ng=norm_g.reshape(1, D_MODEL), w_all=w_all, cw=conv_w, cb=conv_b.reshape(1, CONV_CH),
        qg=jnp.tile(q_gain, 2).reshape(1, LANES), kg=jnp.tile(k_gain, (1, 2)),
        pe_row=pe_row, pe_t=pe_t, perm=perm, wbd=wbd, w_out=w_out.astype(BF16))


def _kv_rows(a):
    return a.reshape(a.shape[:-1] + (2, N_KV, HEAD_DIM))


def _prompt_layer(x, p):
    b, t, _ = x.shape
    cos2, sin2 = _rope_tables(jnp.arange(t, dtype=jnp.int32))
    (convb, sz, q, kvc, kvs, kvw, ksa, vsa, kw, vw, gate, utail) = _prompt_in(
        x, p["ng"], p["w_all"], p["cw"], p["cb"], p["qg"], p["kg"], cos2, sin2, tm=512)
    kc, vc = _prompt_compress(kvc, p["pe_row"], p["wbd"])
    pad_k = jnp.zeros((b, N_KV, WINDOW, LANES), BF16).at[..., HEAD_DIM].set(1.0)
    kw = jnp.concatenate([pad_k, kw], axis=2)
    vw = jnp.concatenate([jnp.zeros((b, N_KV, WINDOW, LANES), BF16), vw], axis=2)
    attnb = _prompt_attn(q, kc, vc, ksa, vsa, kw, vw, gate, sz, tq=256, tk=512)
    y = _out_proj(x.reshape(b * t, D_MODEL), convb.reshape(b * t, CONV_CH), attnb.reshape(b * t, ATTN_W),
                  p["w_out"], tm=1024).reshape(b, t, D_MODEL)
    w_keep = min(WINDOW, t)
    return (y, _kv_rows(kvc), _kv_rows(kvs), _kv_rows(kvw[:, t - w_keep:]), utail[:, 8 - (CONV_W - 1):])


def _sample_layer(x, cache_cmp, cache_slc, win_buf, conv_buf, page_table, p):
    nb, t, _ = x.shape
    assert t == 1
    n_pages = page_table.shape[1]
    past = n_pages * PAGE
    assert past % (2 * SEL_BLK) == 0 and past // SEL_BLK >= LANES and past // SEL_BLK >= N_SEL
    n_pool = cache_cmp.shape[0]
    cos1, sin1 = _rope_tables(jnp.full((1,), past, dtype=jnp.int32))
    convb, sz, qbd, kvn, gate, u = _sample_in(
        x.reshape(nb, D_MODEL), p["ng"], p["w_all"], p["cw"], p["cb"], p["qg"], p["kg"], cos1, sin1,
        conv_buf[:, 0], conv_buf[:, 1])
    qbd = jnp.transpose(qbd, (1, 0, 2))
    feature_major = lambda a: jnp.transpose(a, (0, 2, 3, 4, 1)).reshape(a.shape[0], KV_LANES, a.shape[1])
    ocmp, imp = _sample_cmp(page_table, qbd, feature_major(cache_cmp), p["pe_t"], p["perm"], p["wbd"], past)
    idx = _sample_select(imp[:, :N_KV].reshape(nb * N_KV, past // CMP_BLK), past)
    sel = idx[:, :N_SEL - 1].reshape(nb, N_KV * (N_SEL - 1))
    attn = _sample_attn(page_table, sel, qbd, feature_major(cache_slc), kvn.reshape(nb, 1, N_BRANCH * KV_LANES),
                        feature_major(win_buf), ocmp, gate.reshape(nb, 1, LANES), past)
    y = _sample_out(x.reshape(nb, D_MODEL), convb, attn.reshape(nb, ATTN_W), sz, p["w_out"]).reshape(nb, 1, D_MODEL)
    kv_new = _kv_rows(kvn.reshape(nb, 1, N_BRANCH, KV_LANES))
    win_new = jnp.concatenate([win_buf, kv_new[:, :, 2]], axis=1)[:, t:]
    conv_new = jnp.concatenate([conv_buf, u.reshape(nb, 1, CONV_CH)], axis=1)[:, t:]
    return y, kv_new[:, :, 0], kv_new[:, :, 1], win_new, conv_new


def kernel(x_prompt, x_sample, cache_cmp_kv, cache_slc_kv, state_win_kv, state_conv, page_table, norm_g, w_in,
           conv_w, conv_b, q_gain, k_gain, cmp_pe, cmp_w, w_out):
    yp, ys = x_prompt, x_sample
    outs = [[] for _ in range(8)]
    for layer in range(norm_g.shape[0]):
        p = _layer_params(norm_g[layer], w_in[layer], conv_w[layer], conv_b[layer], q_gain[layer], k_gain[layer],
                          cmp_pe[layer], cmp_w[layer], w_out[layer])
        yp, *prompt_state = _prompt_layer(yp, p)
        ys, *sample_state = _sample_layer(ys, cache_cmp_kv[layer], cache_slc_kv[layer], state_win_kv[layer],
                                          state_conv[layer], page_table, p)
        for acc, a in zip(outs, prompt_state + sample_state):
            acc.append(a)
    return (yp, ys) + tuple(jnp.stack(a) for a in outs)
```

```python
import functools

import jax
import jax.numpy as jnp
from jax import lax
from jax.experimental import pallas as pl
from jax.experimental.pallas import tpu as pltpu

F32 = jnp.float32
BF16 = jnp.bfloat16

D_MODEL = 1024
CONV_CH = 512
CONV_W = 3
N_HEADS = 8
HEAD_DIM = 64
ATTN_W = N_HEADS * HEAD_DIM
N_KV = 2
GROUP = N_HEADS // N_KV
N_BRANCH = 3
CMP_BLK = 32
SEL_BLK = 64
N_SEL = 16
WINDOW = 512
PAGE = 128
ROPE_THETA = 10000.0
NORM_EPS = 1e-6
FORCE_BONUS = 1e4
KV_LANES = 2 * N_KV * HEAD_DIM
LANES = 128
SEL_CHUNK = 64
NEG_BIAS = -1e9
NEG_MASK = -1e30
Q_SCALE = HEAD_DIM ** -0.5 * 1.4426950408889634

C_CONV = 0
C_Q = 4 * CONV_CH
C_KV = C_Q + ATTN_W
C_Z = C_KV + N_BRANCH * KV_LANES
C_G = C_Z + ATTN_W
W_COLS = C_G + LANES

VMEM_LIMIT = 48 * 1024 * 1024
PAGES_PER_CHUNK = 32
CHUNK_SLOTS = 3
PAIR_UNROLL = 16


def _dot(a, b):
    return jnp.dot(a, b, preferred_element_type=F32)


def _dot_nt(a, b):
    return lax.dot_general(a, b, (((1,), (1,)), ((), ())), preferred_element_type=F32)


def _iota(shape, dim):
    return lax.broadcasted_iota(jnp.int32, shape, dim)


def _masked_softmax(s, mask):
    s = jnp.where(mask, s, -jnp.inf)
    m = jnp.max(s, axis=-1, keepdims=True)
    m = jnp.where(m > -jnp.inf, m, 0.0)
    p = jnp.exp2(s - m)
    return p * (1.0 / jnp.maximum(jnp.sum(p, axis=-1, keepdims=True), 1e-30))


def _group_mean_sq(x):
    x2 = x * x
    hi = x2.astype(BF16)
    lo = (x2 - hi.astype(F32)).astype(BF16)
    same = ((_iota((2 * LANES, LANES), 0) >> 6) & 1) == (_iota((2 * LANES, LANES), 1) >> 6)
    ones = jnp.where(same, 1.0, 0.0).astype(BF16)
    return _dot(jnp.concatenate([hi, lo], axis=1), ones) * (1.0 / HEAD_DIM)


def _head_norm_rope(x, gain, cos2, sin2):
    xn = x * lax.rsqrt(_group_mean_sq(x) + NORM_EPS) * gain
    lane = _iota(x.shape, 1)
    swapped = jnp.where((lane & (HEAD_DIM - 1)) < HEAD_DIM // 2,
                        pltpu.roll(xn, LANES - HEAD_DIM // 2, 1), pltpu.roll(xn, HEAD_DIM // 2, 1))
    return xn * cos2 + swapped * sin2


def _low_half(x, other=0.0):
    lane = _iota(x.shape, 1)
    return jnp.where(lane < HEAD_DIM, x, other)


def _swap_halves(x):
    return pltpu.roll(x, HEAD_DIM, 1)


def _normed_input(x, norm_g):
    ms = jnp.mean(x * x, axis=-1, keepdims=True)
    return (x * lax.rsqrt(ms + NORM_EPS) * norm_g).astype(BF16)


def _silu(z):
    return z * jax.nn.sigmoid(z)


def _prompt_in_kernel(x_ref, ng_ref, w_ref, cw_ref, cb_ref, qg_ref, kg_ref, cos_ref, sin_ref,
                      convb_ref, sz_ref, q_ref, kvc_ref, kvs_ref, kvw_ref, ksa_ref, vsa_ref, kw_ref, vw_ref,
                      gate_ref, utail_ref, carry_ref, *, tm):
    ti = pl.program_id(1)
    h = _normed_input(x_ref[...], ng_ref[...])
    cos2 = cos_ref[...]
    sin2 = sin_ref[...]

    u = _dot(h, w_ref[:, CONV_CH:2 * CONV_CH]) * _dot(h, w_ref[:, 2 * CONV_CH:3 * CONV_CH])

    @pl.when(ti == 0)
    def _():
        carry_ref[...] = jnp.zeros_like(carry_ref)

    prev1 = carry_ref[7:8, :]
    prev2 = carry_ref[6:7, :]
    row = _iota(u.shape, 0)
    u1 = jnp.where(row == 0, prev1, pltpu.roll(u, 1, 0))
    u2 = jnp.where(row == 0, prev2, jnp.where(row == 1, prev1, pltpu.roll(u, 2, 0)))
    conv_y = cw_ref[0:1, :] * u2 + cw_ref[1:2, :] * u1 + cw_ref[2:3, :] * u + cb_ref[...]
    b_gate = _dot(h, w_ref[:, 0:CONV_CH])
    z_conv = _dot(h, w_ref[:, 3 * CONV_CH:4 * CONV_CH])
    convb_ref[...] = (b_gate * conv_y * _silu(z_conv)).astype(BF16)
    carry_ref[...] = u[tm - 8:tm, :]
    utail_ref[...] = u[tm - 8:tm, :]

    pq = _dot(h, w_ref[:, C_Q:C_KV])
    for c in range(N_HEADS // 2):
        qc = _head_norm_rope(pq[:, c * LANES:(c + 1) * LANES], qg_ref[...], cos2, sin2) * Q_SCALE
        for half in range(2):
            head = 2 * c + half
            src = qc if half == 0 else _swap_halves(qc)
            q_ref[head // GROUP, head % GROUP] = _low_half(src).astype(BF16)

    pk = _dot(h, w_ref[:, C_KV:C_Z])
    lane = _iota((tm, LANES), 1)
    tpos = ti * tm + _iota((tm, LANES), 0)
    onehot = jnp.where(((tpos >> 6) & (SEL_CHUNK - 1)) == lane - HEAD_DIM, 1.0, 0.0)
    ones_col = jnp.where(lane == HEAD_DIM, 1.0, 0.0)
    for br, out_ref in enumerate((kvc_ref, kvs_ref, kvw_ref)):
        kp = _head_norm_rope(pk[:, br * KV_LANES:br * KV_LANES + LANES], kg_ref[br:br + 1, :], cos2, sin2)
        vv = pk[:, br * KV_LANES + LANES:(br + 1) * KV_LANES]
        out_ref[:, 0:LANES] = kp
        out_ref[:, LANES:KV_LANES] = vv
        if br == 1:
            for g in range(N_KV):
                ksa_ref[g] = _low_half(kp if g == 0 else _swap_halves(kp), onehot).astype(BF16)
                vsa_ref[g] = _low_half(vv if g == 0 else _swap_halves(vv), ones_col).astype(BF16)
        if br == 2:
            for g in range(N_KV):
                kw_ref[g] = _low_half(kp if g == 0 else _swap_halves(kp)).astype(BF16)
                vw_ref[g] = _low_half(vv if g == 0 else _swap_halves(vv), ones_col).astype(BF16)

    sz_ref[...] = _silu(_dot(h, w_ref[:, C_Z:C_G]))
    sg = jax.nn.sigmoid(_dot(h, w_ref[:, C_G:W_COLS]))
    gate_ref[0] = sg
    gate_ref[1] = pltpu.roll(sg, LANES - GROUP * N_BRANCH, 1)


def _prompt_in(x, ng, w_all, cw, cb, qg, kg, cos2, sin2, tm):
    b, t, _ = x.shape
    grid = (b, t // tm)
    row_blk = lambda w: pl.BlockSpec((None, tm, w), lambda bi, ti: (bi, ti, 0))
    full = lambda a: pl.BlockSpec(a.shape, lambda bi, ti: (0,) * a.ndim)
    head_blk = pl.BlockSpec((None, N_KV, tm, LANES), lambda bi, ti: (bi, 0, ti, 0))
    out_shape = (
        jax.ShapeDtypeStruct((b, t, CONV_CH), BF16),
        jax.ShapeDtypeStruct((b, t, ATTN_W), F32),
        jax.ShapeDtypeStruct((b, N_KV, GROUP, t, LANES), BF16),
        jax.ShapeDtypeStruct((b, t, KV_LANES), F32),
        jax.ShapeDtypeStruct((b, t, KV_LANES), F32),
        jax.ShapeDtypeStruct((b, t, KV_LANES), F32),
        jax.ShapeDtypeStruct((b, N_KV, t, LANES), BF16),
        jax.ShapeDtypeStruct((b, N_KV, t, LANES), BF16),
        jax.ShapeDtypeStruct((b, N_KV, t, LANES), BF16),
        jax.ShapeDtypeStruct((b, N_KV, t, LANES), BF16),
        jax.ShapeDtypeStruct((b, N_KV, t, LANES), F32),
        jax.ShapeDtypeStruct((b, 8, CONV_CH), F32),
    )
    out_specs = (
        row_blk(CONV_CH), row_blk(ATTN_W),
        pl.BlockSpec((None, N_KV, GROUP, tm, LANES), lambda bi, ti: (bi, 0, 0, ti, 0)),
        row_blk(KV_LANES), row_blk(KV_LANES), row_blk(KV_LANES),
        head_blk, head_blk, head_blk, head_blk, head_blk,
        pl.BlockSpec((None, 8, CONV_CH), lambda bi, ti: (bi, 0, 0)),
    )
    tab = pl.BlockSpec((tm, LANES), lambda bi, ti: (ti, 0))
    return pl.pallas_call(
        functools.partial(_prompt_in_kernel, tm=tm),
        grid=grid,
        in_specs=[row_blk(D_MODEL), full(ng), full(w_all), full(cw), full(cb), full(qg), full(kg), tab, tab],
        out_specs=out_specs,
        out_shape=out_shape,
        scratch_shapes=[pltpu.VMEM((8, CONV_CH), F32)],
        compiler_params=pltpu.CompilerParams(dimension_semantics=("arbitrary", "arbitrary"),
                                             vmem_limit_bytes=VMEM_LIMIT),
        name="prompt_in_proj",
    )(x, ng, w_all, cw, cb, qg, kg, cos2, sin2)


def _compress_rows(xk_ref, xv_ref, pe_ref, wbd_ref, acc_ref, n_even):
    acc_ref[...] = jnp.zeros_like(acc_ref)

    def rows(ref, r):
        even = ref[pl.ds(r, n_even, stride=2 * CMP_BLK), :]
        odd = ref[pl.ds(CMP_BLK + r, n_even, stride=2 * CMP_BLK), :]
        return jnp.concatenate([even, odd], axis=0)

    def body(r, carry):
        xr = jnp.concatenate([rows(xk_ref, r), rows(xv_ref, r)], axis=1) + pe_ref[pl.ds(r, 1), :]
        acc_ref[...] += _dot(xr.astype(BF16), wbd_ref[r])
        return carry

    lax.fori_loop(0, CMP_BLK, body, 0)


def _prompt_compress_kernel(xk_ref, xv_ref, pe_ref, wbd_ref, kc_ref, vc_ref, acc_ref, *, n_even):
    _compress_rows(xk_ref, xv_ref, pe_ref, wbd_ref, acc_ref, n_even)
    kk = acc_ref[:, 0:LANES]
    vv = acc_ref[:, LANES:KV_LANES]
    for g in range(N_KV):
        kc_ref[g] = _low_half(kk if g == 0 else _swap_halves(kk)).astype(BF16)
        vc_ref[g] = _low_half(vv if g == 0 else _swap_halves(vv)).astype(BF16)


def _prompt_compress(kvc, pe_row, wbd):
    b, t, _ = kvc.shape
    nc = t // CMP_BLK
    out = jax.ShapeDtypeStruct((b, N_KV, nc, LANES), BF16)
    blk = pl.BlockSpec((None, N_KV, nc, LANES), lambda bi: (bi, 0, 0, 0))
    return pl.pallas_call(
        functools.partial(_prompt_compress_kernel, n_even=nc // 2),
        grid=(b,),
        in_specs=[pl.BlockSpec((None, t, LANES), lambda bi: (bi, 0, 0)),
                  pl.BlockSpec((None, t, LANES), lambda bi: (bi, 0, 1)),
                  pl.BlockSpec(pe_row.shape, lambda bi: (0, 0)),
                  pl.BlockSpec(wbd.shape, lambda bi: (0, 0, 0))],
        out_specs=(blk, blk),
        out_shape=(out, out),
        scratch_shapes=[pltpu.VMEM((nc, KV_LANES), F32)],
        compiler_params=pltpu.CompilerParams(dimension_semantics=("arbitrary",), vmem_limit_bytes=VMEM_LIMIT),
        name="prompt_compress",
    )(kvc, kvc, pe_row, wbd)


def _select_blocks(score, blk_f, n_rounds):
    sel = jnp.zeros(score.shape, F32)
    picks = []
    for _ in range(n_rounds):
        m = jnp.max(score, axis=-1, keepdims=True)
        first = jnp.min(jnp.where(score == m, blk_f, 1e9), axis=-1, keepdims=True)
        hit = blk_f == first
        sel = jnp.where(hit, 1.0, sel)
        score = jnp.where(hit, -jnp.inf, score)
        picks.append(first)
    return sel, picks


def _selection_bias(imp, qs):
    tq, ns = imp.shape
    n_chunks = tq // LANES
    imp_t = jnp.concatenate([imp[r * LANES:(r + 1) * LANES, :].T for r in range(n_chunks)], axis=1)
    blk = _iota((ns, tq), 0)
    qp = qs + _iota((1, tq), 1)
    qb = qp >> 6
    valid = blk * SEL_BLK <= qp
    forced = ((blk == 0) | (blk == qb) | (blk == qb - 1)) & valid
    blk_f = blk.astype(F32)
    candidate = valid & jnp.logical_not(forced)
    score = jnp.where(candidate, imp_t, -jnp.inf)
    for _ in range(min(N_SEL, ns) - 3):
        m = jnp.max(score, axis=0, keepdims=True)
        first = jnp.min(jnp.where(score == m, blk_f, 1e9), axis=0, keepdims=True)
        score = jnp.where(blk_f == first, -jnp.inf, score)
    bias_t = jnp.where(forced | (candidate & (score == -jnp.inf)), 0.0, NEG_BIAS)
    return jnp.concatenate([bias_t[:, r * LANES:(r + 1) * LANES].T for r in range(n_chunks)], axis=0)


def _prepare_query_tile(q_ref, kc_ref, vc_ref, kw_ref, vw_ref, qa_ref, ocmp_ref, owin_ref, tile, dst, *, tq, t):
    qs = tile * tq
    rows = GROUP * tq
    nc = t // CMP_BLK
    ns = nc // 2
    n_variants = ns // SEL_CHUNK
    q = q_ref[...].reshape(rows, LANES)
    qpos = qs + (_iota((rows, 1), 0) & (tq - 1))

    col = _iota((1, nc), 1)
    cblk = jnp.where(col < ns, 2 * col, 2 * (col - ns) + 1)
    s_cmp = jnp.where((cblk + 1) * CMP_BLK - 1 <= qpos, _dot_nt(q, kc_ref[...]), -jnp.inf)
    m_cmp = jnp.max(s_cmp, axis=-1, keepdims=True)
    e_cmp = jnp.exp2(s_cmp - jnp.where(m_cmp > -jnp.inf, m_cmp, 0.0))
    inv_cmp = 1.0 / jnp.maximum(jnp.sum(e_cmp, axis=-1, keepdims=True), 1e-30)
    ocmp_ref[dst] = _dot(e_cmp.astype(BF16), vc_ref[...]) * inv_cmp

    pair = (e_cmp[:, 0:ns] + e_cmp[:, ns:nc]) * inv_cmp
    imp = pair[0:tq]
    for n in range(1, GROUP):
        imp = imp + pair[n * tq:(n + 1) * tq]
    bias = _selection_bias(imp, qs)

    for c in range(ns // SEL_CHUNK):
        bc = bias[:, (c // 2) * LANES:(c // 2 + 1) * LANES]
        if c % 2 == 0:
            bc = _swap_halves(bc)
        bc = jnp.concatenate([bc] * GROUP, axis=0).astype(BF16)
        qa_ref[dst * n_variants + c] = _low_half(q, bc)

    q_win = jnp.where(_iota((rows, LANES), 1) == HEAD_DIM, NEG_BIAS, q.astype(F32)).astype(BF16)
    r0 = _iota((GROUP * LANES, 1), 0) & (LANES - 1)
    upper = _iota((1, LANES), 1) > r0
    n_mid = WINDOW // LANES - 1
    o_win_blocks = []
    for a in range(tq // LANES):
        q_a = jnp.concatenate([q_win[n * tq + a * LANES:n * tq + (a + 1) * LANES] for n in range(GROUP)], axis=0)
        base = pl.multiple_of(qs + a * LANES, LANES)
        s_a = _dot_nt(q_a, kw_ref[pl.ds(base, WINDOW + LANES), :])
        s_a = jnp.concatenate([jnp.where(upper, s_a[:, 0:LANES], NEG_MASK), s_a[:, LANES:(n_mid + 1) * LANES],
                               jnp.where(upper, NEG_MASK, s_a[:, (n_mid + 1) * LANES:])], axis=1)
        e_a = jnp.exp2(s_a - jnp.max(s_a, axis=-1, keepdims=True))
        acc_a = _dot(e_a.astype(BF16), vw_ref[pl.ds(base, WINDOW + LANES), :])
        o_win_blocks.append(_low_half(acc_a * (1.0 / acc_a[:, HEAD_DIM:HEAD_DIM + 1])))
    owin_ref[dst] = jnp.concatenate(
        [blk_a[n * LANES:(n + 1) * LANES] for n in range(GROUP) for blk_a in o_win_blocks], axis=0)


def _prompt_attn_kernel(q_ref, kc_ref, vc_ref, ksa_ref, vsa_ref, kw_ref, vw_ref, gate_ref, sz_ref, out_ref,
                        qa_ref, ocmp_ref, owin_ref, sa_ref, sb_ref, m_ref, acc_ref, *, tq, tk, t):
    i = pl.program_id(2)
    qs = i * tq
    rows = GROUP * tq
    n_variants = t // (SEL_CHUNK * SEL_BLK)
    cur = 0
    _prepare_query_tile(q_ref, kc_ref, vc_ref, kw_ref, vw_ref, qa_ref, ocmp_ref, owin_ref, i, cur, tq=tq, t=t)
    qpos = qs + (_iota((rows, 1), 0) & (tq - 1))

    m_ref[...] = jnp.full_like(m_ref, NEG_MASK)
    acc_ref[...] = jnp.zeros_like(acc_ref)

    def scores(kt):
        k0 = pl.multiple_of(kt * tk, tk)
        return _dot_nt(qa_ref[cur * n_variants + kt // (SEL_CHUNK * SEL_BLK // tk)], ksa_ref[pl.ds(k0, tk), :])

    def update(s, kt, causal):
        k0 = pl.multiple_of(kt * tk, tk)
        if causal:
            s = jnp.where(k0 + _iota((1, tk), 1) <= qpos, s, NEG_MASK)
        m_old = m_ref[...]
        m_new = jnp.maximum(m_old, jnp.broadcast_to(jnp.max(s, axis=-1, keepdims=True), m_old.shape))
        p = jnp.exp2(s - jnp.concatenate([m_new] * (tk // LANES), axis=1))
        acc_ref[...] = jnp.exp2(m_old - m_new) * acc_ref[...] + _dot(p.astype(BF16), vsa_ref[pl.ds(k0, tk), :])
        m_ref[...] = m_new

    n_kt = (qs + tq + tk - 1) // tk
    n_pairs = (n_kt - 1) // 2
    sa_ref[...] = scores(0)

    def pair_step(j, carry):
        sb_ref[...] = scores(2 * j + 1)
        update(sa_ref[...], 2 * j, False)
        sa_ref[...] = scores(2 * j + 2)
        update(sb_ref[...], 2 * j + 1, False)
        return carry

    lax.fori_loop(0, n_pairs, pair_step, 0)
    last = n_kt - 1

    @pl.when(n_kt - 2 * n_pairs == 2)
    def _():
        update(sa_ref[...], last - 1, False)
        sa_ref[...] = scores(last)

    o_cmp = ocmp_ref[cur]
    o_win = owin_ref[cur]
    update(sa_ref[...], last, True)

    acc = acc_ref[...]
    o_slc = _low_half(acc * (1.0 / acc[:, HEAD_DIM:HEAD_DIM + 1]))

    gate = gate_ref[...]
    mixed = []
    for n in range(GROUP):
        r = slice(n * tq, (n + 1) * tq)
        mixed.append(gate[:, 3 * n:3 * n + 1] * o_cmp[r] + gate[:, 3 * n + 1:3 * n + 2] * o_slc[r]
                     + gate[:, 3 * n + 2:3 * n + 3] * o_win[r])
    for c in range(GROUP // 2):
        o_pair = mixed[2 * c] + _swap_halves(mixed[2 * c + 1])
        out_ref[:, c * LANES:(c + 1) * LANES] = (o_pair * sz_ref[:, c * LANES:(c + 1) * LANES]).astype(BF16)


def _prompt_attn(q, kc, vc, ksa, vsa, kw, vw, gate, sz, tq, tk):
    b, _, _, t, _ = q.shape
    assert t % (2 * SEL_CHUNK * SEL_BLK) == 0 and t >= WINDOW + tq and (SEL_CHUNK * SEL_BLK) % tk == 0
    assert tq % LANES == 0 and N_SEL > 3
    nc = t // CMP_BLK
    rows = GROUP * tq
    seq = lambda n: pl.BlockSpec((None, None, n, LANES), lambda bi, g, i: (bi, g, 0, 0))
    return pl.pallas_call(
        functools.partial(_prompt_attn_kernel, tq=tq, tk=tk, t=t),
        grid=(b, N_KV, t // tq),
        in_specs=[pl.BlockSpec((None, None, GROUP, tq, LANES), lambda bi, g, i: (bi, g, 0, i, 0)),
                  seq(nc), seq(nc), seq(t), seq(t), seq(t + WINDOW), seq(t + WINDOW),
                  pl.BlockSpec((None, None, tq, LANES), lambda bi, g, i: (bi, g, i, 0)),
                  pl.BlockSpec((None, tq, GROUP * HEAD_DIM), lambda bi, g, i: (bi, i, g))],
        out_specs=pl.BlockSpec((None, tq, GROUP * HEAD_DIM), lambda bi, g, i: (bi, i, g)),
        out_shape=jax.ShapeDtypeStruct((b, t, ATTN_W), BF16),
        scratch_shapes=[pltpu.VMEM((t // (SEL_CHUNK * SEL_BLK), rows, LANES), BF16),
                        pltpu.VMEM((1, rows, LANES), F32), pltpu.VMEM((1, rows, LANES), F32),
                        pltpu.VMEM((rows, tk), F32), pltpu.VMEM((rows, tk), F32),
                        pltpu.VMEM((rows, LANES), F32), pltpu.VMEM((rows, LANES), F32)],
        compiler_params=pltpu.CompilerParams(dimension_semantics=("arbitrary", "arbitrary", "arbitrary"),
                                             vmem_limit_bytes=VMEM_LIMIT),
        name="prompt_attention",
    )(q, kc, vc, ksa, vsa, kw, vw, gate, sz)


def _out_proj_kernel(x_ref, convb_ref, attnb_ref, w_ref, y_ref):
    mix = jnp.concatenate([convb_ref[...], attnb_ref[...]], axis=1)
    y_ref[...] = x_ref[...] + _dot(mix, w_ref[...])


def _out_proj(x2d, convb, attnb, w_out, tm):
    m = x2d.shape[0]
    blk = lambda w: pl.BlockSpec((tm, w), lambda i: (i, 0))
    return pl.pallas_call(
        _out_proj_kernel,
        grid=(m // tm,),
        in_specs=[blk(D_MODEL), blk(CONV_CH), blk(ATTN_W), pl.BlockSpec(w_out.shape, lambda i: (0, 0))],
        out_specs=blk(D_MODEL),
        out_shape=jax.ShapeDtypeStruct((m, D_MODEL), F32),
        compiler_params=pltpu.CompilerParams(dimension_semantics=("arbitrary",), vmem_limit_bytes=VMEM_LIMIT),
        name="out_proj",
    )(x2d, convb, attnb, w_out)


def _sample_in_kernel(x_ref, ng_ref, w_ref, cw_ref, cb_ref, qg_ref, kg_ref, cos_ref, sin_ref, c0_ref, c1_ref,
                      convb_ref, sz_ref, qbd_ref, kvn_ref, gate_ref, u_ref):
    h = _normed_input(x_ref[...], ng_ref[...])
    cos2 = cos_ref[...]
    sin2 = sin_ref[...]
    pc = _dot(h, w_ref[:, C_CONV:C_Q])
    b_gate = pc[:, 0:CONV_CH]
    u = pc[:, CONV_CH:2 * CONV_CH] * pc[:, 2 * CONV_CH:3 * CONV_CH]
    z_conv = pc[:, 3 * CONV_CH:4 * CONV_CH]
    conv_y = cw_ref[0:1, :] * c0_ref[...] + cw_ref[1:2, :] * c1_ref[...] + cw_ref[2:3, :] * u + cb_ref[...]
    convb_ref[...] = (b_gate * conv_y * _silu(z_conv)).astype(BF16)
    u_ref[...] = u

    pq = _dot(h, w_ref[:, C_Q:C_KV])
    zeros = jnp.zeros((x_ref.shape[0], LANES), BF16)
    for c in range(N_HEADS // 2):
        qc = _head_norm_rope(pq[:, c * LANES:(c + 1) * LANES], qg_ref[...], cos2, sin2) * Q_SCALE
        lane = _iota(qc.shape, 1)
        for half in range(2):
            head = 2 * c + half
            g = head // GROUP
            src = qc if half == g else _swap_halves(qc)
            keep = (lane >= g * HEAD_DIM) & (lane < (g + 1) * HEAD_DIM)
            qbd_ref[head] = jnp.concatenate([jnp.where(keep, src, 0.0).astype(BF16), zeros], axis=1)

    pk = _dot(h, w_ref[:, C_KV:C_Z])
    for br in range(N_BRANCH):
        kvn_ref[:, br * KV_LANES:br * KV_LANES + LANES] = _head_norm_rope(
            pk[:, br * KV_LANES:br * KV_LANES + LANES], kg_ref[br:br + 1, :], cos2, sin2)
        kvn_ref[:, br * KV_LANES + LANES:(br + 1) * KV_LANES] = pk[:, br * KV_LANES + LANES:(br + 1) * KV_LANES]

    sz_ref[...] = _silu(_dot(h, w_ref[:, C_Z:C_G]))
    gate_ref[...] = jax.nn.sigmoid(_dot(h, w_ref[:, C_G:W_COLS]))


def _sample_in(x, ng, w_all, cw, cb, qg, kg, cos1, sin1, c0, c1):
    nb = x.shape[0]
    out_shape = (
        jax.ShapeDtypeStruct((nb, CONV_CH), BF16),
        jax.ShapeDtypeStruct((nb, ATTN_W), F32),
        jax.ShapeDtypeStruct((N_HEADS, nb, 2 * LANES), BF16),
        jax.ShapeDtypeStruct((nb, N_BRANCH * KV_LANES), F32),
        jax.ShapeDtypeStruct((nb, LANES), F32),
        jax.ShapeDtypeStruct((nb, CONV_CH), F32),
    )
    return pl.pallas_call(
        _sample_in_kernel,
        out_shape=out_shape,
        compiler_params=pltpu.CompilerParams(vmem_limit_bytes=VMEM_LIMIT),
        name="sample_in_proj",
    )(x, ng, w_all, cw, cb, qg, kg, cos1, sin1, c0, c1)


def _sample_cmp_kernel(pt_ref, qbd_ref, cache_ref, pet_ref, perm_ref, wbd_ref, ocmp_ref, imp_ref,
                       buf_ref, rows_ref, acc_ref, sem, *, n_batch, n_pages, past):
    b = pl.program_id(0)
    n_chunks = n_pages // PAGES_PER_CHUNK
    blocks_per_pair = 2 * PAGE // CMP_BLK

    def page_copy(bb, ch, p, slot):
        return pltpu.make_async_copy(cache_ref.at[pt_ref[bb, ch * PAGES_PER_CHUNK + p]], buf_ref.at[slot, p],
                                     sem.at[slot])

    def start_chunk(bb, ch, slot):
        def body(p, carry):
            page_copy(bb, ch, p, slot).start()
            return carry
        lax.fori_loop(0, PAGES_PER_CHUNK, body, 0)

    def wait_chunk(ch, slot):
        def body(p, carry):
            page_copy(b, ch, p, slot).wait()
            return carry
        lax.fori_loop(0, PAGES_PER_CHUNK, body, 0)

    def stream_slot(offset):
        return (b * n_chunks + offset) % CHUNK_SLOTS

    def start_ahead(offset):
        bb_off, ch = divmod(offset, n_chunks)
        if bb_off == 0:
            start_chunk(b, ch, stream_slot(offset))
        else:
            @pl.when(b + bb_off < n_batch)
            def _():
                start_chunk(b + bb_off, ch, stream_slot(offset))

    @pl.when(b == 0)
    def _():
        for offset in range(CHUNK_SLOTS - 1):
            start_ahead(offset)

    for ch in range(n_chunks):
        slot = stream_slot(ch)
        start_ahead(ch + CHUNK_SLOTS - 1)
        wait_chunk(ch, slot)

        def pairs_body(it, carry):
            for k in range(PAIR_UNROLL):
                pr = it * PAIR_UNROLL + k
                xt = jnp.concatenate([buf_ref[slot, 2 * pr], buf_ref[slot, 2 * pr + 1]], axis=1) + pet_ref[...]
                x_perm = _dot_nt(perm_ref[...], xt.astype(BF16))
                base = pl.multiple_of((ch * (PAGES_PER_CHUNK // 2) + pr) * blocks_per_pair, blocks_per_pair)
                for r in range(CMP_BLK):
                    rows_ref[r, pl.ds(base, blocks_per_pair), :] = (
                        x_perm[r * blocks_per_pair:(r + 1) * blocks_per_pair, :])
            return carry

        lax.fori_loop(0, PAGES_PER_CHUNK // 2 // PAIR_UNROLL, pairs_body, 0)

    acc_ref[...] = jnp.zeros_like(acc_ref)

    def compress_body(it, carry):
        acc = acc_ref[...]
        for k in range(PAIR_UNROLL):
            r = it * PAIR_UNROLL + k
            acc = acc + _dot(rows_ref[r].astype(BF16), wbd_ref[r])
        acc_ref[...] = acc
        return carry

    lax.fori_loop(0, CMP_BLK // PAIR_UNROLL, compress_body, 0)

    nc = past // CMP_BLK
    kv = acc_ref[...].astype(BF16)
    qbd = qbd_ref[...]
    p_cmp = _masked_softmax(_dot_nt(qbd, kv), (_iota((1, nc), 1) + 1) * CMP_BLK - 1 <= past)
    ocmp_ref[...] = _dot(p_cmp.astype(BF16), kv)

    pair = p_cmp + pltpu.roll(p_cmp, nc - 1, 1)
    row = _iota((N_HEADS, nc), 0)
    imp_ref[...] = jnp.where(row == 0, jnp.sum(jnp.where(row < GROUP, pair, 0.0), axis=0, keepdims=True),
                             jnp.sum(jnp.where(row >= GROUP, pair, 0.0), axis=0, keepdims=True))


def _sample_cmp(page_table, qbd, cache_t, pe_t, perm, wbd, past):
    nb, n_pages = page_table.shape
    assert n_pages % PAGES_PER_CHUNK == 0 and n_pages // PAGES_PER_CHUNK >= CHUNK_SLOTS - 1
    nc = past // CMP_BLK
    const = lambda a: pl.BlockSpec(a.shape, lambda b, pt: (0,) * a.ndim)
    grid_spec = pltpu.PrefetchScalarGridSpec(
        num_scalar_prefetch=1,
        grid=(nb,),
        in_specs=[pl.BlockSpec((None, N_HEADS, 2 * LANES), lambda b, pt: (b, 0, 0)),
                  pl.BlockSpec(memory_space=pl.ANY), const(pe_t), const(perm), const(wbd)],
        out_specs=(pl.BlockSpec((None, N_HEADS, KV_LANES), lambda b, pt: (b, 0, 0)),
                   pl.BlockSpec((None, N_HEADS, nc), lambda b, pt: (b, 0, 0))),
        scratch_shapes=[pltpu.VMEM((CHUNK_SLOTS, PAGES_PER_CHUNK, KV_LANES, PAGE), F32),
                        pltpu.VMEM((CMP_BLK, nc, KV_LANES), F32),
                        pltpu.VMEM((nc, KV_LANES), F32), pltpu.SemaphoreType.DMA((CHUNK_SLOTS,))],
    )
    return pl.pallas_call(
        functools.partial(_sample_cmp_kernel, n_batch=nb, n_pages=n_pages, past=past),
        grid_spec=grid_spec,
        out_shape=(jax.ShapeDtypeStruct((nb, N_HEADS, KV_LANES), F32),
                   jax.ShapeDtypeStruct((nb, N_HEADS, nc), F32)),
        compiler_params=pltpu.CompilerParams(dimension_semantics=("arbitrary",), vmem_limit_bytes=VMEM_LIMIT),
        name="sample_compress",
    )(page_table, qbd, cache_t, pe_t, perm, wbd)


def _sample_select_kernel(imp_ref, idx_ref, *, past):
    imp = imp_ref[...]
    lane = _iota(imp.shape, 1)
    blk = lane >> 1
    qb = past // SEL_BLK
    forced = (blk == 0) | (blk == qb) | (blk == qb - 1)
    valid = ((lane & 1) == 0) & (blk * SEL_BLK <= past)
    score = jnp.where(valid, imp + jnp.where(forced, FORCE_BONUS, 0.0), -jnp.inf)
    _, picks = _select_blocks(score, lane.astype(F32), N_SEL - 1)
    out_lane = _iota(idx_ref.shape, 1)
    idx = jnp.zeros(idx_ref.shape, F32)
    for k, pick in enumerate(picks):
        idx = jnp.where(out_lane == k, pick * 0.5, idx)
    idx_ref[...] = idx.astype(jnp.int32)


def _sample_select(imp, past):
    return pl.pallas_call(
        functools.partial(_sample_select_kernel, past=past),
        out_shape=jax.ShapeDtypeStruct((imp.shape[0], LANES), jnp.int32),
        compiler_params=pltpu.CompilerParams(vmem_limit_bytes=VMEM_LIMIT),
        name="sample_select",
    )(imp)


def _sample_attn_kernel(pt_ref, sel_ref, qbd_ref, cache_ref, kvn_ref, win_ref, ocmp_ref, gate_ref, out_ref,
                        buf_ref, sem, *, n_batch, past):
    b = pl.program_id(0)
    n_pick = N_SEL - 1
    blocks_per_page = PAGE // SEL_BLK
    slot = b % 2

    def page_copy(bb, j, s):
        return pltpu.make_async_copy(cache_ref.at[pt_ref[bb, sel_ref[bb, j] >> 1]], buf_ref.at[s, j], sem.at[s])

    @pl.when(b == 0)
    def _():
        for j in range(N_KV * n_pick):
            page_copy(0, j, 0).start()

    @pl.when(b + 1 < n_batch)
    def _():
        for j in range(N_KV * n_pick):
            page_copy(b + 1, j, 1 - slot).start()

    qbd = qbd_ref[...]
    qf = qbd.astype(F32)
    row = _iota((N_HEADS, 1), 0)

    def new_token(offset):
        kv_new = kvn_ref[:, offset:offset + KV_LANES].astype(BF16).astype(F32)
        return jnp.sum(qf * kv_new, axis=-1, keepdims=True), kv_new

    def attend(keys_t, mask, s_new, kv_new):
        s = jnp.where(mask, _dot(qbd, keys_t), -jnp.inf)
        m = jnp.maximum(jnp.max(s, axis=-1, keepdims=True), s_new)
        p = jnp.exp2(s - m)
        p_new = jnp.exp2(s_new - m)
        norm = 1.0 / (jnp.sum(p, axis=-1, keepdims=True) + p_new)
        return (_dot_nt(p.astype(BF16), keys_t) + p_new.astype(BF16).astype(F32) * kv_new) * norm

    w_buf = win_ref.shape[1]
    s_new, kv_new = new_token(2 * KV_LANES)
    x_win = attend(win_ref[...].astype(BF16), _iota((1, w_buf), 1) > w_buf - WINDOW, s_new, kv_new)

    for j in range(N_KV * n_pick):
        page_copy(b, j, slot).wait()

    s_new, kv_new = new_token(KV_LANES)
    lane = _iota((1, n_pick * PAGE), 1)
    x_slc = []
    for g in range(N_KV):
        slots = range(g * n_pick, (g + 1) * n_pick)
        keys_t = jnp.concatenate([buf_ref[slot, j] for j in slots], axis=1).astype(BF16)
        half = jnp.concatenate([jnp.full((1, PAGE), sel_ref[b, j] & (blocks_per_page - 1), jnp.int32) for j in slots],
                               axis=1)
        x_slc.append(attend(keys_t, ((lane & (PAGE - 1)) >> (SEL_BLK.bit_length() - 1)) == half, s_new, kv_new))
    x_slc = jnp.where(row < GROUP, x_slc[0], x_slc[1])

    gate = jnp.broadcast_to(gate_ref[...], (N_HEADS, LANES))
    lane = _iota((N_HEADS, LANES), 1)
    mixed = jnp.zeros((N_HEADS, KV_LANES), F32)
    for br, x_br in enumerate((ocmp_ref[...], x_slc, x_win)):
        g_col = jnp.sum(jnp.where(lane == N_BRANCH * _iota((N_HEADS, LANES), 0) + br, gate, 0.0), axis=-1, keepdims=True)
        mixed = mixed + g_col * x_br
    v_lo = mixed[:, 2 * HEAD_DIM:3 * HEAD_DIM]
    v_hi = mixed[:, 3 * HEAD_DIM:4 * HEAD_DIM]
    out_ref[...] = jnp.where(row < GROUP, v_lo, v_hi)


def _sample_attn(page_table, sel, qbd, cache, kvn, win, ocmp, gate, past):
    nb = page_table.shape[0]
    n_pick = N_SEL - 1
    per_b = lambda *shape: pl.BlockSpec((None,) + shape, lambda b, pt, s: (b,) + (0,) * len(shape))
    grid_spec = pltpu.PrefetchScalarGridSpec(
        num_scalar_prefetch=2,
        grid=(nb,),
        in_specs=[per_b(N_HEADS, 2 * LANES),
                  pl.BlockSpec(memory_space=pl.ANY),
                  per_b(1, N_BRANCH * KV_LANES),
                  per_b(KV_LANES, win.shape[2]),
                  per_b(N_HEADS, KV_LANES),
                  per_b(1, LANES)],
        out_specs=per_b(N_HEADS, HEAD_DIM),
        scratch_shapes=[pltpu.VMEM((2, N_KV * n_pick, KV_LANES, PAGE), F32), pltpu.SemaphoreType.DMA((2,))],
    )
    return pl.pallas_call(
        functools.partial(_sample_attn_kernel, n_batch=nb, past=past),
        grid_spec=grid_spec,
        out_shape=jax.ShapeDtypeStruct((nb, N_HEADS, HEAD_DIM), F32),
        compiler_params=pltpu.CompilerParams(dimension_semantics=("arbitrary",), vmem_limit_bytes=VMEM_LIMIT),
        name="sample_attention",
    )(page_table, sel, qbd, cache, kvn, win, ocmp, gate)


def _sample_out_kernel(x_ref, convb_ref, attn_ref, sz_ref, w_ref, y_ref):
    mix = jnp.concatenate([convb_ref[...], (attn_ref[...] * sz_ref[...]).astype(BF16)], axis=1)
    y_ref[...] = x_ref[...] + _dot(mix, w_ref[...])


def _sample_out(x, convb, attn, sz, w_out):
    return pl.pallas_call(
        _sample_out_kernel,
        out_shape=jax.ShapeDtypeStruct(x.shape, F32),
        compiler_params=pltpu.CompilerParams(vmem_limit_bytes=VMEM_LIMIT),
        name="sample_out_proj",
    )(x, convb, attn, sz, w_out)


def _rope_tables(pos):
    half = HEAD_DIM // 2
    inv = ROPE_THETA ** (-jnp.arange(half, dtype=F32) / half)
    ang = pos.astype(F32)[:, None] * inv[None, :]
    cos, sin = lax.optimization_barrier((jnp.cos(ang), jnp.sin(ang)))
    return jnp.concatenate([cos, cos, cos, cos], axis=1), jnp.concatenate([-sin, sin, -sin, sin], axis=1)


def _layer_params(norm_g, w_in, conv_w, conv_b, q_gain, k_gain, cmp_pe, cmp_w, w_out):
    c_gates = C_KV + N_BRANCH * KV_LANES
    n_gates = N_HEADS * N_BRANCH
    w_all = jnp.concatenate([w_in[:, :c_gates], w_in[:, c_gates + n_gates:], w_in[:, c_gates:c_gates + n_gates],
                             jnp.zeros((D_MODEL, LANES - n_gates), w_in.dtype)], axis=1).astype(BF16)
    w16 = cmp_w.astype(BF16)
    wbd = jnp.concatenate(
        [jnp.pad(w16[:, j], ((0, 0), (0, 0), (o, KV_LANES - HEAD_DIM - o)))
         for j in range(2) for o in ((j * N_KV + g) * HEAD_DIM for g in range(N_KV))], axis=1)
    pe_row = jnp.broadcast_to(cmp_pe[:, :, None, :], (CMP_BLK, 2, N_KV, HEAD_DIM)).reshape(CMP_BLK, KV_LANES)
    blocks_per_pair = 2 * PAGE // CMP_BLK
    pe_t = jnp.tile(pe_row.T, (1, blocks_per_pair))
    m = jnp.arange(2 * PAGE)
    perm = (m[None, :] == (CMP_BLK * (m % blocks_per_pair) + m // blocks_per_pair)[:, None]).astype(BF16)
    return dict(
        ng=norm_g.reshape(1, D_MODEL), w_all=w_all, cw=conv_w, cb=conv_b.reshape(1, CONV_CH),
        qg=jnp.tile(q_gain, 2).reshape(1, LANES), kg=jnp.tile(k_gain, (1, 2)),
        pe_row=pe_row, pe_t=pe_t, perm=perm, wbd=wbd, w_out=w_out.astype(BF16))


def _kv_rows(a):
    return a.reshape(a.shape[:-1] + (2, N_KV, HEAD_DIM))


def _prompt_layer(x, p):
    b, t, _ = x.shape
    cos2, sin2 = _rope_tables(jnp.arange(t, dtype=jnp.int32))
    (convb, sz, q, kvc, kvs, kvw, ksa, vsa, kw, vw, gate, utail) = _prompt_in(
        x, p["ng"], p["w_all"], p["cw"], p["cb"], p["qg"], p["kg"], cos2, sin2, tm=512)
    kc, vc = _prompt_compress(kvc, p["pe_row"], p["wbd"])
    pad_k = jnp.zeros((b, N_KV, WINDOW, LANES), BF16).at[..., HEAD_DIM].set(1.0)
    kw = jnp.concatenate([pad_k, kw], axis=2)
    vw = jnp.concatenate([jnp.zeros((b, N_KV, WINDOW, LANES), BF16), vw], axis=2)
    attnb = _prompt_attn(q, kc, vc, ksa, vsa, kw, vw, gate, sz, tq=256, tk=512)
    y = _out_proj(x.reshape(b * t, D_MODEL), convb.reshape(b * t, CONV_CH), attnb.reshape(b * t, ATTN_W),
                  p["w_out"], tm=1024).reshape(b, t, D_MODEL)
    w_keep = min(WINDOW, t)
    return (y, _kv_rows(kvc), _kv_rows(kvs), _kv_rows(kvw[:, t - w_keep:]), utail[:, 8 - (CONV_W - 1):])


def _sample_layer(x, cache_cmp, cache_slc, win_buf, conv_buf, page_table, p):
    nb, t, _ = x.shape
    assert t == 1
    n_pages = page_table.shape[1]
    past = n_pages * PAGE
    assert past % (2 * SEL_BLK) == 0 and past // SEL_BLK >= LANES and past // SEL_BLK >= N_SEL
    n_pool = cache_cmp.shape[0]
    cos1, sin1 = _rope_tables(jnp.full((1,), past, dtype=jnp.int32))
    convb, sz, qbd, kvn, gate, u = _sample_in(
        x.reshape(nb, D_MODEL), p["ng"], p["w_all"], p["cw"], p["cb"], p["qg"], p["kg"], cos1, sin1,
        conv_buf[:, 0], conv_buf[:, 1])
    qbd = jnp.transpose(qbd, (1, 0, 2))
    feature_major = lambda a: jnp.transpose(a, (0, 2, 3, 4, 1)).reshape(a.shape[0], KV_LANES, a.shape[1])
    ocmp, imp = _sample_cmp(page_table, qbd, feature_major(cache_cmp), p["pe_t"], p["perm"], p["wbd"], past)
    idx = _sample_select(imp[:, :N_KV].reshape(nb * N_KV, past // CMP_BLK), past)
    sel = idx[:, :N_SEL - 1].reshape(nb, N_KV * (N_SEL - 1))
    attn = _sample_attn(page_table, sel, qbd, feature_major(cache_slc), kvn.reshape(nb, 1, N_BRANCH * KV_LANES),
                        feature_major(win_buf), ocmp, gate.reshape(nb, 1, LANES), past)
    y = _sample_out(x.reshape(nb, D_MODEL), convb, attn.reshape(nb, ATTN_W), sz, p["w_out"]).reshape(nb, 1, D_MODEL)
    kv_new = _kv_rows(kvn.reshape(nb, 1, N_BRANCH, KV_LANES))
    win_new = jnp.concatenate([win_buf, kv_new[:, :, 2]], axis=1)[:, t:]
    conv_new = jnp.concatenate([conv_buf, u.reshape(nb, 1, CONV_CH)], axis=1)[:, t:]
    return y, kv_new[:, :, 0], kv_new[:, :, 1], win_new, conv_new


def kernel(x_prompt, x_sample, cache_cmp_kv, cache_slc_kv, state_win_kv, state_conv, page_table, norm_g, w_in,
           conv_w, conv_b, q_gain, k_gain, cmp_pe, cmp_w, w_out):
    yp, ys = x_prompt, x_sample
    outs = [[] for _ in range(8)]
    for layer in range(norm_g.shape[0]):
        p = _layer_params(norm_g[layer], w_in[layer], conv_w[layer], conv_b[layer], q_gain[layer], k_gain[layer],
                          cmp_pe[layer], cmp_w[layer], w_out[layer])
        yp, *prompt_state = _prompt_layer(yp, p)
        ys, *sample_state = _sample_layer(ys, cache_cmp_kv[layer], cache_slc_kv[layer], state_win_kv[layer],
                                          state_conv[layer], page_table, p)
        for acc, a in zip(outs, prompt_state + sample_state):
            acc.append(a)
    return (yp, ys) + tuple(jnp.stack(a) for a in outs)
```

```python
import functools

import jax
import jax.numpy as jnp
from jax import lax
from jax.experimental import pallas as pl
from jax.experimental.pallas import tpu as pltpu

F32 = jnp.float32
BF16 = jnp.bfloat16

D_MODEL = 1024
CONV_CH = 512
CONV_W = 3
N_HEADS = 8
HEAD_DIM = 64
ATTN_W = N_HEADS * HEAD_DIM
N_KV = 2
GROUP = N_HEADS // N_KV
N_BRANCH = 3
CMP_BLK = 32
SEL_BLK = 64
N_SEL = 16
WINDOW = 512
PAGE = 128
ROPE_THETA = 10000.0
NORM_EPS = 1e-6
FORCE_BONUS = 1e4
KV_LANES = 2 * N_KV * HEAD_DIM
LANES = 128
SEL_CHUNK = 64
NEG_BIAS = -1e9
NEG_MASK = -1e30
Q_SCALE = HEAD_DIM ** -0.5 * 1.4426950408889634

C_CONV = 0
C_Q = 4 * CONV_CH
C_KV = C_Q + ATTN_W
C_Z = C_KV + N_BRANCH * KV_LANES
C_G = C_Z + ATTN_W
W_COLS = C_G + LANES

VMEM_LIMIT = 48 * 1024 * 1024
PAGES_PER_CHUNK = 32
CHUNK_SLOTS = 3
PAIR_UNROLL = 16


def _dot(a, b):
    return jnp.dot(a, b, preferred_element_type=F32)


def _dot_nt(a, b):
    return lax.dot_general(a, b, (((1,), (1,)), ((), ())), preferred_element_type=F32)


def _iota(shape, dim):
    return lax.broadcasted_iota(jnp.int32, shape, dim)


def _masked_softmax(s, mask):
    s = jnp.where(mask, s, -jnp.inf)
    m = jnp.max(s, axis=-1, keepdims=True)
    m = jnp.where(m > -jnp.inf, m, 0.0)
    p = jnp.exp2(s - m)
    return p * (1.0 / jnp.maximum(jnp.sum(p, axis=-1, keepdims=True), 1e-30))


def _group_mean_sq(x):
    x2 = x * x
    hi = x2.astype(BF16)
    lo = (x2 - hi.astype(F32)).astype(BF16)
    same = ((_iota((2 * LANES, LANES), 0) >> 6) & 1) == (_iota((2 * LANES, LANES), 1) >> 6)
    ones = jnp.where(same, 1.0, 0.0).astype(BF16)
    return _dot(jnp.concatenate([hi, lo], axis=1), ones) * (1.0 / HEAD_DIM)


def _head_norm_rope(x, gain, cos2, sin2):
    xn = x * lax.rsqrt(_group_mean_sq(x) + NORM_EPS) * gain
    lane = _iota(x.shape, 1)
    swapped = jnp.where((lane & (HEAD_DIM - 1)) < HEAD_DIM // 2,
                        pltpu.roll(xn, LANES - HEAD_DIM // 2, 1), pltpu.roll(xn, HEAD_DIM // 2, 1))
    return xn * cos2 + swapped * sin2


def _low_half(x, other=0.0):
    lane = _iota(x.shape, 1)
    return jnp.where(lane < HEAD_DIM, x, other)


def _swap_halves(x):
    return pltpu.roll(x, HEAD_DIM, 1)


def _normed_input(x, norm_g):
    ms = jnp.mean(x * x, axis=-1, keepdims=True)
    return (x * lax.rsqrt(ms + NORM_EPS) * norm_g).astype(BF16)


def _silu(z):
    return z * jax.nn.sigmoid(z)


def _prompt_in_kernel(x_ref, ng_ref, w_ref, cw_ref, cb_ref, qg_ref, kg_ref, cos_ref, sin_ref,
                      convb_ref, sz_ref, q_ref, kvc_ref, kvs_ref, kvw_ref, ksa_ref, vsa_ref, kw_ref, vw_ref,
                      gate_ref, utail_ref, carry_ref, *, tm):
    ti = pl.program_id(1)
    h = _normed_input(x_ref[...], ng_ref[...])
    cos2 = cos_ref[...]
    sin2 = sin_ref[...]

    u = _dot(h, w_ref[:, CONV_CH:2 * CONV_CH]) * _dot(h, w_ref[:, 2 * CONV_CH:3 * CONV_CH])

    @pl.when(ti == 0)
    def _():
        carry_ref[...] = jnp.zeros_like(carry_ref)

    prev1 = carry_ref[7:8, :]
    prev2 = carry_ref[6:7, :]
    row = _iota(u.shape, 0)
    u1 = jnp.where(row == 0, prev1, pltpu.roll(u, 1, 0))
    u2 = jnp.where(row == 0, prev2, jnp.where(row == 1, prev1, pltpu.roll(u, 2, 0)))
    conv_y = cw_ref[0:1, :] * u2 + cw_ref[1:2, :] * u1 + cw_ref[2:3, :] * u + cb_ref[...]
    b_gate = _dot(h, w_ref[:, 0:CONV_CH])
    z_conv = _dot(h, w_ref[:, 3 * CONV_CH:4 * CONV_CH])
    convb_ref[...] = (b_gate * conv_y * _silu(z_conv)).astype(BF16)
    carry_ref[...] = u[tm - 8:tm, :]
    utail_ref[...] = u[tm - 8:tm, :]

    pq = _dot(h, w_ref[:, C_Q:C_KV])
    for c in range(N_HEADS // 2):
        qc = _head_norm_rope(pq[:, c * LANES:(c + 1) * LANES], qg_ref[...], cos2, sin2) * Q_SCALE
        for half in range(2):
            head = 2 * c + half
            src = qc if half == 0 else _swap_halves(qc)
            q_ref[head // GROUP, head % GROUP] = _low_half(src).astype(BF16)

    pk = _dot(h, w_ref[:, C_KV:C_Z])
    lane = _iota((tm, LANES), 1)
    tpos = ti * tm + _iota((tm, LANES), 0)
    onehot = jnp.where(((tpos >> 6) & (SEL_CHUNK - 1)) == lane - HEAD_DIM, 1.0, 0.0)
    ones_col = jnp.where(lane == HEAD_DIM, 1.0, 0.0)
    for br, out_ref in enumerate((kvc_ref, kvs_ref, kvw_ref)):
        kp = _head_norm_rope(pk[:, br * KV_LANES:br * KV_LANES + LANES], kg_ref[br:br + 1, :], cos2, sin2)
        vv = pk[:, br * KV_LANES + LANES:(br + 1) * KV_LANES]
        out_ref[:, 0:LANES] = kp
        out_ref[:, LANES:KV_LANES] = vv
        if br == 1:
            for g in range(N_KV):
                ksa_ref[g] = _low_half(kp if g == 0 else _swap_halves(kp), onehot).astype(BF16)
                vsa_ref[g] = _low_half(vv if g == 0 else _swap_halves(vv), ones_col).astype(BF16)
        if br == 2:
            for g in range(N_KV):
                kw_ref[g] = _low_half(kp if g == 0 else _swap_halves(kp)).astype(BF16)
                vw_ref[g] = _low_half(vv if g == 0 else _swap_halves(vv), ones_col).astype(BF16)

    sz_ref[...] = _silu(_dot(h, w_ref[:, C_Z:C_G]))
    sg = jax.nn.sigmoid(_dot(h, w_ref[:, C_G:W_COLS]))
    gate_ref[0] = sg
    gate_ref[1] = pltpu.roll(sg, LANES - GROUP * N_BRANCH, 1)


def _prompt_in(x, ng, w_all, cw, cb, qg, kg, cos2, sin2, tm):
    b, t, _ = x.shape
    grid = (b, t // tm)
    row_blk = lambda w: pl.BlockSpec((None, tm, w), lambda bi, ti: (bi, ti, 0))
    full = lambda a: pl.BlockSpec(a.shape, lambda bi, ti: (0,) * a.ndim)
    head_blk = pl.BlockSpec((None, N_KV, tm, LANES), lambda bi, ti: (bi, 0, ti, 0))
    out_shape = (
        jax.ShapeDtypeStruct((b, t, CONV_CH), BF16),
        jax.ShapeDtypeStruct((b, t, ATTN_W), F32),
        jax.ShapeDtypeStruct((b, N_KV, GROUP, t, LANES), BF16),
        jax.ShapeDtypeStruct((b, t, KV_LANES), F32),
        jax.ShapeDtypeStruct((b, t, KV_LANES), F32),
        jax.ShapeDtypeStruct((b, t, KV_LANES), F32),
        jax.ShapeDtypeStruct((b, N_KV, t, LANES), BF16),
        jax.ShapeDtypeStruct((b, N_KV, t, LANES), BF16),
        jax.ShapeDtypeStruct((b, N_KV, t, LANES), BF16),
        jax.ShapeDtypeStruct((b, N_KV, t, LANES), BF16),
        jax.ShapeDtypeStruct((b, N_KV, t, LANES), F32),
        jax.ShapeDtypeStruct((b, 8, CONV_CH), F32),
    )
    out_specs = (
        row_blk(CONV_CH), row_blk(ATTN_W),
        pl.BlockSpec((None, N_KV, GROUP, tm, LANES), lambda bi, ti: (bi, 0, 0, ti, 0)),
        row_blk(KV_LANES), row_blk(KV_LANES), row_blk(KV_LANES),
        head_blk, head_blk, head_blk, head_blk, head_blk,
        pl.BlockSpec((None, 8, CONV_CH), lambda bi, ti: (bi, 0, 0)),
    )
    tab = pl.BlockSpec((tm, LANES), lambda bi, ti: (ti, 0))
    return pl.pallas_call(
        functools.partial(_prompt_in_kernel, tm=tm),
        grid=grid,
        in_specs=[row_blk(D_MODEL), full(ng), full(w_all), full(cw), full(cb), full(qg), full(kg), tab, tab],
        out_specs=out_specs,
        out_shape=out_shape,
        scratch_shapes=[pltpu.VMEM((8, CONV_CH), F32)],
        compiler_params=pltpu.CompilerParams(dimension_semantics=("arbitrary", "arbitrary"),
                                             vmem_limit_bytes=VMEM_LIMIT),
        name="prompt_in_proj",
    )(x, ng, w_all, cw, cb, qg, kg, cos2, sin2)


def _compress_rows(xk_ref, xv_ref, pe_ref, wbd_ref, acc_ref, n_even):
    acc_ref[...] = jnp.zeros_like(acc_ref)

    def rows(ref, r):
        even = ref[pl.ds(r, n_even, stride=2 * CMP_BLK), :]
        odd = ref[pl.ds(CMP_BLK + r, n_even, stride=2 * CMP_BLK), :]
        return jnp.concatenate([even, odd], axis=0)

    def body(r, carry):
        xr = jnp.concatenate([rows(xk_ref, r), rows(xv_ref, r)], axis=1) + pe_ref[pl.ds(r, 1), :]
        acc_ref[...] += _dot(xr.astype(BF16), wbd_ref[r])
        return carry

    lax.fori_loop(0, CMP_BLK, body, 0)


def _prompt_compress_kernel(xk_ref, xv_ref, pe_ref, wbd_ref, kc_ref, vc_ref, acc_ref, *, n_even):
    _compress_rows(xk_ref, xv_ref, pe_ref, wbd_ref, acc_ref, n_even)
    kk = acc_ref[:, 0:LANES]
    vv = acc_ref[:, LANES:KV_LANES]
    for g in range(N_KV):
        kc_ref[g] = _low_half(kk if g == 0 else _swap_halves(kk)).astype(BF16)
        vc_ref[g] = _low_half(vv if g == 0 else _swap_halves(vv)).astype(BF16)


def _prompt_compress(kvc, pe_row, wbd):
    b, t, _ = kvc.shape
    nc = t // CMP_BLK
    out = jax.ShapeDtypeStruct((b, N_KV, nc, LANES), BF16)
    blk = pl.BlockSpec((None, N_KV, nc, LANES), lambda bi: (bi, 0, 0, 0))
    return pl.pallas_call(
        functools.partial(_prompt_compress_kernel, n_even=nc // 2),
        grid=(b,),
        in_specs=[pl.BlockSpec((None, t, LANES), lambda bi: (bi, 0, 0)),
                  pl.BlockSpec((None, t, LANES), lambda bi: (bi, 0, 1)),
                  pl.BlockSpec(pe_row.shape, lambda bi: (0, 0)),
                  pl.BlockSpec(wbd.shape, lambda bi: (0, 0, 0))],
        out_specs=(blk, blk),
        out_shape=(out, out),
        scratch_shapes=[pltpu.VMEM((nc, KV_LANES), F32)],
        compiler_params=pltpu.CompilerParams(dimension_semantics=("arbitrary",), vmem_limit_bytes=VMEM_LIMIT),
        name="prompt_compress",
    )(kvc, kvc, pe_row, wbd)


def _select_blocks(score, blk_f, n_rounds):
    sel = jnp.zeros(score.shape, F32)
    picks = []
    for _ in range(n_rounds):
        m = jnp.max(score, axis=-1, keepdims=True)
        first = jnp.min(jnp.where(score == m, blk_f, 1e9), axis=-1, keepdims=True)
        hit = blk_f == first
        sel = jnp.where(hit, 1.0, sel)
        score = jnp.where(hit, -jnp.inf, score)
        picks.append(first)
    return sel, picks


def _selection_bias(imp, qs):
    tq, ns = imp.shape
    n_chunks = tq // LANES
    imp_t = jnp.concatenate([imp[r * LANES:(r + 1) * LANES, :].T for r in range(n_chunks)], axis=1)
    blk = _iota((ns, tq), 0)
    qp = qs + _iota((1, tq), 1)
    qb = qp >> 6
    valid = blk * SEL_BLK <= qp
    forced = ((blk == 0) | (blk == qb) | (blk == qb - 1)) & valid
    blk_f = blk.astype(F32)
    candidate = valid & jnp.logical_not(forced)
    score = jnp.where(candidate, imp_t, -jnp.inf)
    for _ in range(min(N_SEL, ns) - 3):
        m = jnp.max(score, axis=0, keepdims=True)
        first = jnp.min(jnp.where(score == m, blk_f, 1e9), axis=0, keepdims=True)
        score = jnp.where(blk_f == first, -jnp.inf, score)
    bias_t = jnp.where(forced | (candidate & (score == -jnp.inf)), 0.0, NEG_BIAS)
    return jnp.concatenate([bias_t[:, r * LANES:(r + 1) * LANES].T for r in range(n_chunks)], axis=0)


def _prepare_query_tile(q_ref, kc_ref, vc_ref, kw_ref, vw_ref, qa_ref, ocmp_ref, owin_ref, tile, dst, *, tq, t):
    qs = tile * tq
    rows = GROUP * tq
    nc = t // CMP_BLK
    ns = nc // 2
    n_variants = ns // SEL_CHUNK
    q = q_ref[...].reshape(rows, LANES)
    qpos = qs + (_iota((rows, 1), 0) & (tq - 1))

    col = _iota((1, nc), 1)
    cblk = jnp.where(col < ns, 2 * col, 2 * (col - ns) + 1)
    s_cmp = jnp.where((cblk + 1) * CMP_BLK - 1 <= qpos, _dot_nt(q, kc_ref[...]), -jnp.inf)
    m_cmp = jnp.max(s_cmp, axis=-1, keepdims=True)
    e_cmp = jnp.exp2(s_cmp - jnp.where(m_cmp > -jnp.inf, m_cmp, 0.0))
    inv_cmp = 1.0 / jnp.maximum(jnp.sum(e_cmp, axis=-1, keepdims=True), 1e-30)
    ocmp_ref[dst] = _dot(e_cmp.astype(BF16), vc_ref[...]) * inv_cmp

    pair = (e_cmp[:, 0:ns] + e_cmp[:, ns:nc]) * inv_cmp
    imp = pair[0:tq]
    for n in range(1, GROUP):
        imp = imp + pair[n * tq:(n + 1) * tq]
    bias = _selection_bias(imp, qs)

    for c in range(ns // SEL_CHUNK):
        bc = bias[:, (c // 2) * LANES:(c // 2 + 1) * LANES]
        if c % 2 == 0:
            bc = _swap_halves(bc)
        bc = jnp.concatenate([bc] * GROUP, axis=0).astype(BF16)
        qa_ref[dst * n_variants + c] = _low_half(q, bc)

    q_win = jnp.where(_iota((rows, LANES), 1) == HEAD_DIM, NEG_BIAS, q.astype(F32)).astype(BF16)
    r0 = _iota((GROUP * LANES, 1), 0) & (LANES - 1)
    upper = _iota((1, LANES), 1) > r0
    n_mid = WINDOW // LANES - 1
    o_win_blocks = []
    for a in range(tq // LANES):
        q_a = jnp.concatenate([q_win[n * tq + a * LANES:n * tq + (a + 1) * LANES] for n in range(GROUP)], axis=0)
        base = pl.multiple_of(qs + a * LANES, LANES)
        s_a = _dot_nt(q_a, kw_ref[pl.ds(base, WINDOW + LANES), :])
        s_a = jnp.concatenate([jnp.where(upper, s_a[:, 0:LANES], NEG_MASK), s_a[:, LANES:(n_mid + 1) * LANES],
                               jnp.where(upper, NEG_MASK, s_a[:, (n_mid + 1) * LANES:])], axis=1)
        e_a = jnp.exp2(s_a - jnp.max(s_a, axis=-1, keepdims=True))
        acc_a = _dot(e_a.astype(BF16), vw_ref[pl.ds(base, WINDOW + LANES), :])
        o_win_blocks.append(_low_half(acc_a * (1.0 / acc_a[:, HEAD_DIM:HEAD_DIM + 1])))
    owin_ref[dst] = jnp.concatenate(
        [blk_a[n * LANES:(n + 1) * LANES] for n in range(GROUP) for blk_a in o_win_blocks], axis=0)


def _prompt_attn_kernel(q_ref, kc_ref, vc_ref, ksa_ref, vsa_ref, kw_ref, vw_ref, gate_ref, sz_ref, out_ref,
                        qa_ref, ocmp_ref, owin_ref, sa_ref, sb_ref, m_ref, acc_ref, *, tq, tk, t):
    i = pl.program_id(2)
    qs = i * tq
    rows = GROUP * tq
    n_variants = t // (SEL_CHUNK * SEL_BLK)
    cur = 0
    _prepare_query_tile(q_ref, kc_ref, vc_ref, kw_ref, vw_ref, qa_ref, ocmp_ref, owin_ref, i, cur, tq=tq, t=t)
    qpos = qs + (_iota((rows, 1), 0) & (tq - 1))

    m_ref[...] = jnp.full_like(m_ref, NEG_MASK)
    acc_ref[...] = jnp.zeros_like(acc_ref)

    def scores(kt):
        k0 = pl.multiple_of(kt * tk, tk)
        return _dot_nt(qa_ref[cur * n_variants + kt // (SEL_CHUNK * SEL_BLK // tk)], ksa_ref[pl.ds(k0, tk), :])

    def update(s, kt, causal):
        k0 = pl.multiple_of(kt * tk, tk)
        if causal:
            s = jnp.where(k0 + _iota((1, tk), 1) <= qpos, s, NEG_MASK)
        m_old = m_ref[...]
        m_new = jnp.maximum(m_old, jnp.broadcast_to(jnp.max(s, axis=-1, keepdims=True), m_old.shape))
        p = jnp.exp2(s - jnp.concatenate([m_new] * (tk // LANES), axis=1))
        acc_ref[...] = jnp.exp2(m_old - m_new) * acc_ref[...] + _dot(p.astype(BF16), vsa_ref[pl.ds(k0, tk), :])
        m_ref[...] = m_new

    n_kt = (qs + tq + tk - 1) // tk
    n_pairs = (n_kt - 1) // 2
    sa_ref[...] = scores(0)

    def pair_step(j, carry):
        sb_ref[...] = scores(2 * j + 1)
        update(sa_ref[...], 2 * j, False)
        sa_ref[...] = scores(2 * j + 2)
        update(sb_ref[...], 2 * j + 1, False)
        return carry

    lax.fori_loop(0, n_pairs, pair_step, 0)
    last = n_kt - 1

    @pl.when(n_kt - 2 * n_pairs == 2)
    def _():
        update(sa_ref[...], last - 1, False)
        sa_ref[...] = scores(last)

    o_cmp = ocmp_ref[cur]
    o_win = owin_ref[cur]
    update(sa_ref[...], last, True)

    acc = acc_ref[...]
    o_slc = _low_half(acc * (1.0 / acc[:, HEAD_DIM:HEAD_DIM + 1]))

    gate = gate_ref[...]
    mixed = []
    for n in range(GROUP):
        r = slice(n * tq, (n + 1) * tq)
        mixed.append(gate[:, 3 * n:3 * n + 1] * o_cmp[r] + gate[:, 3 * n + 1:3 * n + 2] * o_slc[r]
                     + gate[:, 3 * n + 2:3 * n + 3] * o_win[r])
    for c in range(GROUP // 2):
        o_pair = mixed[2 * c] + _swap_halves(mixed[2 * c + 1])
        out_ref[:, c * LANES:(c + 1) * LANES] = (o_pair * sz_ref[:, c * LANES:(c + 1) * LANES]).astype(BF16)


def _prompt_attn(q, kc, vc, ksa, vsa, kw, vw, gate, sz, tq, tk):
    b, _, _, t, _ = q.shape
    assert t % (2 * SEL_CHUNK * SEL_BLK) == 0 and t >= WINDOW + tq and (SEL_CHUNK * SEL_BLK) % tk == 0
    assert tq % LANES == 0 and N_SEL > 3
    nc = t // CMP_BLK
    rows = GROUP * tq
    seq = lambda n: pl.BlockSpec((None, None, n, LANES), lambda bi, g, i: (bi, g, 0, 0))
    return pl.pallas_call(
        functools.partial(_prompt_attn_kernel, tq=tq, tk=tk, t=t),
        grid=(b, N_KV, t // tq),
        in_specs=[pl.BlockSpec((None, None, GROUP, tq, LANES), lambda bi, g, i: (bi, g, 0, i, 0)),
                  seq(nc), seq(nc), seq(t), seq(t), seq(t + WINDOW), seq(t + WINDOW),
                  pl.BlockSpec((None, None, tq, LANES), lambda bi, g, i: (bi, g, i, 0)),
                  pl.BlockSpec((None, tq, GROUP * HEAD_DIM), lambda bi, g, i: (bi, i, g))],
        out_specs=pl.BlockSpec((None, tq, GROUP * HEAD_DIM), lambda bi, g, i: (bi, i, g)),
        out_shape=jax.ShapeDtypeStruct((b, t, ATTN_W), BF16),
        scratch_shapes=[pltpu.VMEM((t // (SEL_CHUNK * SEL_BLK), rows, LANES), BF16),
                        pltpu.VMEM((1, rows, LANES), F32), pltpu.VMEM((1, rows, LANES), F32),
                        pltpu.VMEM((rows, tk), F32), pltpu.VMEM((rows, tk), F32),
                        pltpu.VMEM((rows, LANES), F32), pltpu.VMEM((rows, LANES), F32)],
        compiler_params=pltpu.CompilerParams(dimension_semantics=("arbitrary", "arbitrary", "arbitrary"),
                                             vmem_limit_bytes=VMEM_LIMIT),
        name="prompt_attention",
    )(q, kc, vc, ksa, vsa, kw, vw, gate, sz)


def _out_proj_kernel(x_ref, convb_ref, attnb_ref, w_ref, y_ref):
    mix = jnp.concatenate([convb_ref[...], attnb_ref[...]], axis=1)
    y_ref[...] = x_ref[...] + _dot(mix, w_ref[...])


def _out_proj(x2d, convb, attnb, w_out, tm):
    m = x2d.shape[0]
    blk = lambda w: pl.BlockSpec((tm, w), lambda i: (i, 0))
    return pl.pallas_call(
        _out_proj_kernel,
        grid=(m // tm,),
        in_specs=[blk(D_MODEL), blk(CONV_CH), blk(ATTN_W), pl.BlockSpec(w_out.shape, lambda i: (0, 0))],
        out_specs=blk(D_MODEL),
        out_shape=jax.ShapeDtypeStruct((m, D_MODEL), F32),
        compiler_params=pltpu.CompilerParams(dimension_semantics=("arbitrary",), vmem_limit_bytes=VMEM_LIMIT),
        name="out_proj",
    )(x2d, convb, attnb, w_out)


def _sample_in_kernel(x_ref, ng_ref, w_ref, cw_ref, cb_ref, qg_ref, kg_ref, cos_ref, sin_ref, c0_ref, c1_ref,
                      convb_ref, sz_ref, qbd_ref, kvn_ref, gate_ref, u_ref):
    h = _normed_input(x_ref[...], ng_ref[...])
    cos2 = cos_ref[...]
    sin2 = sin_ref[...]
    pc = _dot(h, w_ref[:, C_CONV:C_Q])
    b_gate = pc[:, 0:CONV_CH]
    u = pc[:, CONV_CH:2 * CONV_CH] * pc[:, 2 * CONV_CH:3 * CONV_CH]
    z_conv = pc[:, 3 * CONV_CH:4 * CONV_CH]
    conv_y = cw_ref[0:1, :] * c0_ref[...] + cw_ref[1:2, :] * c1_ref[...] + cw_ref[2:3, :] * u + cb_ref[...]
    convb_ref[...] = (b_gate * conv_y * _silu(z_conv)).astype(BF16)
    u_ref[...] = u

    pq = _dot(h, w_ref[:, C_Q:C_KV])
    zeros = jnp.zeros((x_ref.shape[0], LANES), BF16)
    for c in range(N_HEADS // 2):
        qc = _head_norm_rope(pq[:, c * LANES:(c + 1) * LANES], qg_ref[...], cos2, sin2) * Q_SCALE
        lane = _iota(qc.shape, 1)
        for half in range(2):
            head = 2 * c + half
            g = head // GROUP
            src = qc if half == g else _swap_halves(qc)
            keep = (lane >= g * HEAD_DIM) & (lane < (g + 1) * HEAD_DIM)
            qbd_ref[head] = jnp.concatenate([jnp.where(keep, src, 0.0).astype(BF16), zeros], axis=1)

    pk = _dot(h, w_ref[:, C_KV:C_Z])
    for br in range(N_BRANCH):
        kvn_ref[:, br * KV_LANES:br * KV_LANES + LANES] = _head_norm_rope(
            pk[:, br * KV_LANES:br * KV_LANES + LANES], kg_ref[br:br + 1, :], cos2, sin2)
        kvn_ref[:, br * KV_LANES + LANES:(br + 1) * KV_LANES] = pk[:, br * KV_LANES + LANES:(br + 1) * KV_LANES]

    sz_ref[...] = _silu(_dot(h, w_ref[:, C_Z:C_G]))
    gate_ref[...] = jax.nn.sigmoid(_dot(h, w_ref[:, C_G:W_COLS]))


def _sample_in(x, ng, w_all, cw, cb, qg, kg, cos1, sin1, c0, c1):
    nb = x.shape[0]
    out_shape = (
        jax.ShapeDtypeStruct((nb, CONV_CH), BF16),
        jax.ShapeDtypeStruct((nb, ATTN_W), F32),
        jax.ShapeDtypeStruct((N_HEADS, nb, 2 * LANES), BF16),
        jax.ShapeDtypeStruct((nb, N_BRANCH * KV_LANES), F32),
        jax.ShapeDtypeStruct((nb, LANES), F32),
        jax.ShapeDtypeStruct((nb, CONV_CH), F32),
    )
    return pl.pallas_call(
        _sample_in_kernel,
        out_shape=out_shape,
        compiler_params=pltpu.CompilerParams(vmem_limit_bytes=VMEM_LIMIT),
        name="sample_in_proj",
    )(x, ng, w_all, cw, cb, qg, kg, cos1, sin1, c0, c1)


def _sample_cmp_kernel(pt_ref, qbd_ref, cache_ref, pet_ref, perm_ref, wbd_ref, ocmp_ref, imp_ref,
                       buf_ref, rows_ref, acc_ref, sem, *, n_batch, n_pages, past):
    b = pl.program_id(0)
    n_chunks = n_pages // PAGES_PER_CHUNK
    blocks_per_pair = 2 * PAGE // CMP_BLK

    def page_copy(bb, ch, p, slot):
        return pltpu.make_async_copy(cache_ref.at[pt_ref[bb, ch * PAGES_PER_CHUNK + p]], buf_ref.at[slot, p],
                                     sem.at[slot])

    def start_chunk(bb, ch, slot):
        def body(p, carry):
            page_copy(bb, ch, p, slot).start()
            return carry
        lax.fori_loop(0, PAGES_PER_CHUNK, body, 0)

    def wait_chunk(ch, slot):
        def body(p, carry):
            page_copy(b, ch, p, slot).wait()
            return carry
        lax.fori_loop(0, PAGES_PER_CHUNK, body, 0)

    def stream_slot(offset):
        return (b * n_chunks + offset) % CHUNK_SLOTS

    def start_ahead(offset):
        bb_off, ch = divmod(offset, n_chunks)
        if bb_off == 0:
            start_chunk(b, ch, stream_slot(offset))
        else:
            @pl.when(b + bb_off < n_batch)
            def _():
                start_chunk(b + bb_off, ch, stream_slot(offset))

    @pl.when(b == 0)
    def _():
        for offset in range(CHUNK_SLOTS - 1):
            start_ahead(offset)

    for ch in range(n_chunks):
        slot = stream_slot(ch)
        start_ahead(ch + CHUNK_SLOTS - 1)
        wait_chunk(ch, slot)

        def pairs_body(it, carry):
            for k in range(PAIR_UNROLL):
                pr = it * PAIR_UNROLL + k
                xt = jnp.concatenate([buf_ref[slot, 2 * pr], buf_ref[slot, 2 * pr + 1]], axis=1) + pet_ref[...]
                x_perm = _dot_nt(perm_ref[...], xt.astype(BF16))
                base = pl.multiple_of((ch * (PAGES_PER_CHUNK // 2) + pr) * blocks_per_pair, blocks_per_pair)
                for r in range(CMP_BLK):
                    rows_ref[r, pl.ds(base, blocks_per_pair), :] = (
                        x_perm[r * blocks_per_pair:(r + 1) * blocks_per_pair, :])
            return carry

        lax.fori_loop(0, PAGES_PER_CHUNK // 2 // PAIR_UNROLL, pairs_body, 0)

    acc_ref[...] = jnp.zeros_like(acc_ref)

    def compress_body(it, carry):
        acc = acc_ref[...]
        for k in range(PAIR_UNROLL):
            r = it * PAIR_UNROLL + k
            acc = acc + _dot(rows_ref[r].astype(BF16), wbd_ref[r])
        acc_ref[...] = acc
        return carry

    lax.fori_loop(0, CMP_BLK // PAIR_UNROLL, compress_body, 0)

    nc = past // CMP_BLK
    kv = acc_ref[...].astype(BF16)
    qbd = qbd_ref[...]
    p_cmp = _masked_softmax(_dot_nt(qbd, kv), (_iota((1, nc), 1) + 1) * CMP_BLK - 1 <= past)
    ocmp_ref[...] = _dot(p_cmp.astype(BF16), kv)

    pair = p_cmp + pltpu.roll(p_cmp, nc - 1, 1)
    row = _iota((N_HEADS, nc), 0)
    imp_ref[...] = jnp.where(row == 0, jnp.sum(jnp.where(row < GROUP, pair, 0.0), axis=0, keepdims=True),
                             jnp.sum(jnp.where(row >= GROUP, pair, 0.0), axis=0, keepdims=True))


def _sample_cmp(page_table, qbd, cache_t, pe_t, perm, wbd, past):
    nb, n_pages = page_table.shape
    assert n_pages % PAGES_PER_CHUNK == 0 and n_pages // PAGES_PER_CHUNK >= CHUNK_SLOTS - 1
    nc = past // CMP_BLK
    const = lambda a: pl.BlockSpec(a.shape, lambda b, pt: (0,) * a.ndim)
    grid_spec = pltpu.PrefetchScalarGridSpec(
        num_scalar_prefetch=1,
        grid=(nb,),
        in_specs=[pl.BlockSpec((None, N_HEADS, 2 * LANES), lambda b, pt: (b, 0, 0)),
                  pl.BlockSpec(memory_space=pl.ANY), const(pe_t), const(perm), const(wbd)],
        out_specs=(pl.BlockSpec((None, N_HEADS, KV_LANES), lambda b, pt: (b, 0, 0)),
                   pl.BlockSpec((None, N_HEADS, nc), lambda b, pt: (b, 0, 0))),
        scratch_shapes=[pltpu.VMEM((CHUNK_SLOTS, PAGES_PER_CHUNK, KV_LANES, PAGE), F32),
                        pltpu.VMEM((CMP_BLK, nc, KV_LANES), F32),
                        pltpu.VMEM((nc, KV_LANES), F32), pltpu.SemaphoreType.DMA((CHUNK_SLOTS,))],
    )
    return pl.pallas_call(
        functools.partial(_sample_cmp_kernel, n_batch=nb, n_pages=n_pages, past=past),
        grid_spec=grid_spec,
        out_shape=(jax.ShapeDtypeStruct((nb, N_HEADS, KV_LANES), F32),
                   jax.ShapeDtypeStruct((nb, N_HEADS, nc), F32)),
        compiler_params=pltpu.CompilerParams(dimension_semantics=("arbitrary",), vmem_limit_bytes=VMEM_LIMIT),
        name="sample_compress",
    )(page_table, qbd, cache_t, pe_t, perm, wbd)


def _sample_select_kernel(imp_ref, idx_ref, *, past):
    imp = imp_ref[...]
    lane = _iota(imp.shape, 1)
    blk = lane >> 1
    qb = past // SEL_BLK
    forced = (blk == 0) | (blk == qb) | (blk == qb - 1)
    valid = ((lane & 1) == 0) & (blk * SEL_BLK <= past)
    score = jnp.where(valid, imp + jnp.where(forced, FORCE_BONUS, 0.0), -jnp.inf)
    _, picks = _select_blocks(score, lane.astype(F32), N_SEL - 1)
    out_lane = _iota(idx_ref.shape, 1)
    idx = jnp.zeros(idx_ref.shape, F32)
    for k, pick in enumerate(picks):
        idx = jnp.where(out_lane == k, pick * 0.5, idx)
    idx_ref[...] = idx.astype(jnp.int32)


def _sample_select(imp, past):
    return pl.pallas_call(
        functools.partial(_sample_select_kernel, past=past),
        out_shape=jax.ShapeDtypeStruct((imp.shape[0], LANES), jnp.int32),
        compiler_params=pltpu.CompilerParams(vmem_limit_bytes=VMEM_LIMIT),
        name="sample_select",
    )(imp)


def _sample_attn_kernel(pt_ref, sel_ref, qbd_ref, cache_ref, kvn_ref, win_ref, ocmp_ref, gate_ref, out_ref,
                        buf_ref, sem, *, n_batch, past):
    b = pl.program_id(0)
    n_pick = N_SEL - 1
    blocks_per_page = PAGE // SEL_BLK
    slot = b % 2

    def page_copy(bb, j, s):
        return pltpu.make_async_copy(cache_ref.at[pt_ref[bb, sel_ref[bb, j] >> 1]], buf_ref.at[s, j], sem.at[s])

    @pl.when(b == 0)
    def _():
        for j in range(N_KV * n_pick):
            page_copy(0, j, 0).start()

    @pl.when(b + 1 < n_batch)
    def _():
        for j in range(N_KV * n_pick):
            page_copy(b + 1, j, 1 - slot).start()

    qbd = qbd_ref[...]
    qf = qbd.astype(F32)
    row = _iota((N_HEADS, 1), 0)

    def new_token(offset):
        kv_new = kvn_ref[:, offset:offset + KV_LANES].astype(BF16).astype(F32)
        return jnp.sum(qf * kv_new, axis=-1, keepdims=True), kv_new

    def attend(keys_t, mask, s_new, kv_new):
        s = jnp.where(mask, _dot(qbd, keys_t), -jnp.inf)
        m = jnp.maximum(jnp.max(s, axis=-1, keepdims=True), s_new)
        p = jnp.exp2(s - m)
        p_new = jnp.exp2(s_new - m)
        norm = 1.0 / (jnp.sum(p, axis=-1, keepdims=True) + p_new)
        return (_dot_nt(p.astype(BF16), keys_t) + p_new.astype(BF16).astype(F32) * kv_new) * norm

    w_buf = win_ref.shape[1]
    s_new, kv_new = new_token(2 * KV_LANES)
    x_win = attend(win_ref[...].astype(BF16), _iota((1, w_buf), 1) > w_buf - WINDOW, s_new, kv_new)

    for j in range(N_KV * n_pick):
        page_copy(b, j, slot).wait()

    s_new, kv_new = new_token(KV_LANES)
    lane = _iota((1, n_pick * PAGE), 1)
    x_slc = []
    for g in range(N_KV):
        slots = range(g * n_pick, (g + 1) * n_pick)
        keys_t = jnp.concatenate([buf_ref[slot, j] for j in slots], axis=1).astype(BF16)
        half = jnp.concatenate([jnp.full((1, PAGE), sel_ref[b, j] & (blocks_per_page - 1), jnp.int32) for j in slots],
                               axis=1)
        x_slc.append(attend(keys_t, ((lane & (PAGE - 1)) >> (SEL_BLK.bit_length() - 1)) == half, s_new, kv_new))
    x_slc = jnp.where(row < GROUP, x_slc[0], x_slc[1])

    gate = jnp.broadcast_to(gate_ref[...], (N_HEADS, LANES))
    lane = _iota((N_HEADS, LANES), 1)
    mixed = jnp.zeros((N_HEADS, KV_LANES), F32)
    for br, x_br in enumerate((ocmp_ref[...], x_slc, x_win)):
        g_col = jnp.sum(jnp.where(lane == N_BRANCH * _iota((N_HEADS, LANES), 0) + br, gate, 0.0), axis=-1, keepdims=True)
        mixed = mixed + g_col * x_br
    v_lo = mixed[:, 2 * HEAD_DIM:3 * HEAD_DIM]
    v_hi = mixed[:, 3 * HEAD_DIM:4 * HEAD_DIM]
    out_ref[...] = jnp.where(row < GROUP, v_lo, v_hi)


def _sample_attn(page_table, sel, qbd, cache, kvn, win, ocmp, gate, past):
    nb = page_table.shape[0]
    n_pick = N_SEL - 1
    per_b = lambda *shape: pl.BlockSpec((None,) + shape, lambda b, pt, s: (b,) + (0,) * len(shape))
    grid_spec = pltpu.PrefetchScalarGridSpec(
        num_scalar_prefetch=2,
        grid=(nb,),
        in_specs=[per_b(N_HEADS, 2 * LANES),
                  pl.BlockSpec(memory_space=pl.ANY),
                  per_b(1, N_BRANCH * KV_LANES),
                  per_b(KV_LANES, win.shape[2]),
                  per_b(N_HEADS, KV_LANES),
                  per_b(1, LANES)],
        out_specs=per_b(N_HEADS, HEAD_DIM),
        scratch_shapes=[pltpu.VMEM((2, N_KV * n_pick, KV_LANES, PAGE), F32), pltpu.SemaphoreType.DMA((2,))],
    )
    return pl.pallas_call(
        functools.partial(_sample_attn_kernel, n_batch=nb, past=past),
        grid_spec=grid_spec,
        out_shape=jax.ShapeDtypeStruct((nb, N_HEADS, HEAD_DIM), F32),
        compiler_params=pltpu.CompilerParams(dimension_semantics=("arbitrary",), vmem_limit_bytes=VMEM_LIMIT),
        name="sample_attention",
    )(page_table, sel, qbd, cache, kvn, win, ocmp, gate)


def _sample_out_kernel(x_ref, convb_ref, attn_ref, sz_ref, w_ref, y_ref):
    mix = jnp.concatenate([convb_ref[...], (attn_ref[...] * sz_ref[...]).astype(BF16)], axis=1)
    y_ref[...] = x_ref[...] + _dot(mix, w_ref[...])


def _sample_out(x, convb, attn, sz, w_out):
    return pl.pallas_call(
        _sample_out_kernel,
        out_shape=jax.ShapeDtypeStruct(x.shape, F32),
        compiler_params=pltpu.CompilerParams(vmem_limit_bytes=VMEM_LIMIT),
        name="sample_out_proj",
    )(x, convb, attn, sz, w_out)


def _rope_tables(pos):
    half = HEAD_DIM // 2
    inv = ROPE_THETA ** (-jnp.arange(half, dtype=F32) / half)
    ang = pos.astype(F32)[:, None] * inv[None, :]
    cos, sin = lax.optimization_barrier((jnp.cos(ang), jnp.sin(ang)))
    return jnp.concatenate([cos, cos, cos, cos], axis=1), jnp.concatenate([-sin, sin, -sin, sin], axis=1)


def _layer_params(norm_g, w_in, conv_w, conv_b, q_gain, k_gain, cmp_pe, cmp_w, w_out):
    c_gates = C_KV + N_BRANCH * KV_LANES
    n_gates = N_HEADS * N_BRANCH
    w_all = jnp.concatenate([w_in[:, :c_gates], w_in[:, c_gates + n_gates:], w_in[:, c_gates:c_gates + n_gates],
                             jnp.zeros((D_MODEL, LANES - n_gates), w_in.dtype)], axis=1).astype(BF16)
    w16 = cmp_w.astype(BF16)
    wbd = jnp.concatenate(
        [jnp.pad(w16[:, j], ((0, 0), (0, 0), (o, KV_LANES - HEAD_DIM - o)))
         for j in range(2) for o in ((j * N_KV + g) * HEAD_DIM for g in range(N_KV))], axis=1)
    pe_row = jnp.broadcast_to(cmp_pe[:, :, None, :], (CMP_BLK, 2, N_KV, HEAD_DIM)).reshape(CMP_BLK, KV_LANES)
    blocks_per_pair = 2 * PAGE // CMP_BLK
    pe_t = jnp.tile(pe_row.T, (1, blocks_per_pair))
    m = jnp.arange(2 * PAGE)
    perm = (m[None, :] == (CMP_BLK * (m % blocks_per_pair) + m // blocks_per_pair)[:, None]).astype(BF16)
    return dict(
        ng=norm_g.reshape(1, D_MODEL), w_all=w_all, cw=conv_w, cb=conv_b.reshape(1, CONV_CH),
        qg=jnp.tile(q_gain, 2).reshape(1, LANES), kg=jnp.tile(k_gain, (1, 2)),
        pe_row=pe_row, pe_t=pe_t, perm=perm, wbd=wbd, w_out=w_out.astype(BF16))


def _kv_rows(a):
    return a.reshape(a.shape[:-1] + (2, N_KV, HEAD_DIM))


def _prompt_layer(x, p):
    b, t, _ = x.shape
    cos2, sin2 = _rope_tables(jnp.arange(t, dtype=jnp.int32))
    (convb, sz, q, kvc, kvs, kvw, ksa, vsa, kw, vw, gate, utail) = _prompt_in(
        x, p["ng"], p["w_all"], p["cw"], p["cb"], p["qg"], p["kg"], cos2, sin2, tm=512)
    kc, vc = _prompt_compress(kvc, p["pe_row"], p["wbd"])
    pad_k = jnp.zeros((b, N_KV, WINDOW, LANES), BF16).at[..., HEAD_DIM].set(1.0)
    kw = jnp.concatenate([pad_k, kw], axis=2)
    vw = jnp.concatenate([jnp.zeros((b, N_KV, WINDOW, LANES), BF16), vw], axis=2)
    attnb = _prompt_attn(q, kc, vc, ksa, vsa, kw, vw, gate, sz, tq=256, tk=1024)
    y = _out_proj(x.reshape(b * t, D_MODEL), convb.reshape(b * t, CONV_CH), attnb.reshape(b * t, ATTN_W),
                  p["w_out"], tm=1024).reshape(b, t, D_MODEL)
    w_keep = min(WINDOW, t)
    return (y, _kv_rows(kvc), _kv_rows(kvs), _kv_rows(kvw[:, t - w_keep:]), utail[:, 8 - (CONV_W - 1):])


def _sample_layer(x, cache_cmp, cache_slc, win_buf, conv_buf, page_table, p):
    nb, t, _ = x.shape
    assert t == 1
    n_pages = page_table.shape[1]
    past = n_pages * PAGE
    assert past % (2 * SEL_BLK) == 0 and past // SEL_BLK >= LANES and past // SEL_BLK >= N_SEL
    n_pool = cache_cmp.shape[0]
    cos1, sin1 = _rope_tables(jnp.full((1,), past, dtype=jnp.int32))
    convb, sz, qbd, kvn, gate, u = _sample_in(
        x.reshape(nb, D_MODEL), p["ng"], p["w_all"], p["cw"], p["cb"], p["qg"], p["kg"], cos1, sin1,
        conv_buf[:, 0], conv_buf[:, 1])
    qbd = jnp.transpose(qbd, (1, 0, 2))
    feature_major = lambda a: jnp.transpose(a, (0, 2, 3, 4, 1)).reshape(a.shape[0], KV_LANES, a.shape[1])
    ocmp, imp = _sample_cmp(page_table, qbd, feature_major(cache_cmp), p["pe_t"], p["perm"], p["wbd"], past)
    idx = _sample_select(imp[:, :N_KV].reshape(nb * N_KV, past // CMP_BLK), past)
    sel = idx[:, :N_SEL - 1].reshape(nb, N_KV * (N_SEL - 1))
    attn = _sample_attn(page_table, sel, qbd, feature_major(cache_slc), kvn.reshape(nb, 1, N_BRANCH * KV_LANES),
                        feature_major(win_buf), ocmp, gate.reshape(nb, 1, LANES), past)
    y = _sample_out(x.reshape(nb, D_MODEL), convb, attn.reshape(nb, ATTN_W), sz, p["w_out"]).reshape(nb, 1, D_MODEL)
    kv_new = _kv_rows(kvn.reshape(nb, 1, N_BRANCH, KV_LANES))
    win_new = jnp.concatenate([win_buf, kv_new[:, :, 2]], axis=1)[:, t:]
    conv_new = jnp.concatenate([conv_buf, u.reshape(nb, 1, CONV_CH)], axis=1)[:, t:]
    return y, kv_new[:, :, 0], kv_new[:, :, 1], win_new, conv_new


def kernel(x_prompt, x_sample, cache_cmp_kv, cache_slc_kv, state_win_kv, state_conv, page_table, norm_g, w_in,
           conv_w, conv_b, q_gain, k_gain, cmp_pe, cmp_w, w_out):
    yp, ys = x_prompt, x_sample
    outs = [[] for _ in range(8)]
    for layer in range(norm_g.shape[0]):
        p = _layer_params(norm_g[layer], w_in[layer], conv_w[layer], conv_b[layer], q_gain[layer], k_gain[layer],
                          cmp_pe[layer], cmp_w[layer], w_out[layer])
        yp, *prompt_state = _prompt_layer(yp, p)
        ys, *sample_state = _sample_layer(ys, cache_cmp_kv[layer], cache_slc_kv[layer], state_win_kv[layer],
                                          state_conv[layer], page_table, p)
        for acc, a in zip(outs, prompt_state + sample_state):
            acc.append(a)
    return (yp, ys) + tuple(jnp.stack(a) for a in outs)
```

```python
import functools

import jax
import jax.numpy as jnp
from jax import lax
from jax.experimental import pallas as pl
from jax.experimental.pallas import tpu as pltpu

F32 = jnp.float32
BF16 = jnp.bfloat16

D_MODEL = 1024
CONV_CH = 512
CONV_W = 3
N_HEADS = 8
HEAD_DIM = 64
ATTN_W = N_HEADS * HEAD_DIM
N_KV = 2
GROUP = N_HEADS // N_KV
N_BRANCH = 3
CMP_BLK = 32
SEL_BLK = 64
N_SEL = 16
WINDOW = 512
PAGE = 128
ROPE_THETA = 10000.0
NORM_EPS = 1e-6
FORCE_BONUS = 1e4
KV_LANES = 2 * N_KV * HEAD_DIM
LANES = 128
SEL_CHUNK = 64
NEG_BIAS = -1e9
NEG_MASK = -1e30
Q_SCALE = HEAD_DIM ** -0.5 * 1.4426950408889634

C_CONV = 0
C_Q = 4 * CONV_CH
C_KV = C_Q + ATTN_W
C_Z = C_KV + N_BRANCH * KV_LANES
C_G = C_Z + ATTN_W
W_COLS = C_G + LANES

VMEM_LIMIT = 48 * 1024 * 1024
PAGES_PER_CHUNK = 32
CHUNK_SLOTS = 3
PAIR_UNROLL = 16
CHUNK_BLOCKS = 16
CHUNK_UNROLL = 4


def _dot(a, b):
    return jnp.dot(a, b, preferred_element_type=F32)


def _dot_nt(a, b):
    return lax.dot_general(a, b, (((1,), (1,)), ((), ())), preferred_element_type=F32)


def _iota(shape, dim):
    return lax.broadcasted_iota(jnp.int32, shape, dim)


def _masked_softmax(s, mask):
    s = jnp.where(mask, s, -jnp.inf)
    m = jnp.max(s, axis=-1, keepdims=True)
    m = jnp.where(m > -jnp.inf, m, 0.0)
    p = jnp.exp2(s - m)
    return p * (1.0 / jnp.maximum(jnp.sum(p, axis=-1, keepdims=True), 1e-30))


def _group_mean_sq(x):
    x2 = x * x
    hi = x2.astype(BF16)
    lo = (x2 - hi.astype(F32)).astype(BF16)
    same = ((_iota((2 * LANES, LANES), 0) >> 6) & 1) == (_iota((2 * LANES, LANES), 1) >> 6)
    ones = jnp.where(same, 1.0, 0.0).astype(BF16)
    return _dot(jnp.concatenate([hi, lo], axis=1), ones) * (1.0 / HEAD_DIM)


def _head_norm_rope(x, gain, cos2, sin2):
    xn = x * lax.rsqrt(_group_mean_sq(x) + NORM_EPS) * gain
    lane = _iota(x.shape, 1)
    swapped = jnp.where((lane & (HEAD_DIM - 1)) < HEAD_DIM // 2,
                        pltpu.roll(xn, LANES - HEAD_DIM // 2, 1), pltpu.roll(xn, HEAD_DIM // 2, 1))
    return xn * cos2 + swapped * sin2


def _low_half(x, other=0.0):
    lane = _iota(x.shape, 1)
    return jnp.where(lane < HEAD_DIM, x, other)


def _swap_halves(x):
    return pltpu.roll(x, HEAD_DIM, 1)


def _normed_input(x, norm_g):
    ms = jnp.mean(x * x, axis=-1, keepdims=True)
    return (x * lax.rsqrt(ms + NORM_EPS) * norm_g).astype(BF16)


def _silu(z):
    return z * jax.nn.sigmoid(z)


def _prompt_in_kernel(x_ref, ng_ref, w_ref, cw_ref, cb_ref, qg_ref, kg_ref, cos_ref, sin_ref,
                      convb_ref, sz_ref, q_ref, kvc_ref, kvs_ref, kvw_ref, ksa_ref, vsa_ref, kw_ref, vw_ref,
                      gate_ref, utail_ref, carry_ref, *, tm):
    ti = pl.program_id(1)
    h = _normed_input(x_ref[...], ng_ref[...])
    cos2 = cos_ref[...]
    sin2 = sin_ref[...]

    u = _dot(h, w_ref[:, CONV_CH:2 * CONV_CH]) * _dot(h, w_ref[:, 2 * CONV_CH:3 * CONV_CH])

    @pl.when(ti == 0)
    def _():
        carry_ref[...] = jnp.zeros_like(carry_ref)

    prev1 = carry_ref[7:8, :]
    prev2 = carry_ref[6:7, :]
    row = _iota(u.shape, 0)
    u1 = jnp.where(row == 0, prev1, pltpu.roll(u, 1, 0))
    u2 = jnp.where(row == 0, prev2, jnp.where(row == 1, prev1, pltpu.roll(u, 2, 0)))
    conv_y = cw_ref[0:1, :] * u2 + cw_ref[1:2, :] * u1 + cw_ref[2:3, :] * u + cb_ref[...]
    b_gate = _dot(h, w_ref[:, 0:CONV_CH])
    z_conv = _dot(h, w_ref[:, 3 * CONV_CH:4 * CONV_CH])
    convb_ref[...] = (b_gate * conv_y * _silu(z_conv)).astype(BF16)
    carry_ref[...] = u[tm - 8:tm, :]
    utail_ref[...] = u[tm - 8:tm, :]

    pq = _dot(h, w_ref[:, C_Q:C_KV])
    for c in range(N_HEADS // 2):
        qc = _head_norm_rope(pq[:, c * LANES:(c + 1) * LANES], qg_ref[...], cos2, sin2) * Q_SCALE
        for half in range(2):
            head = 2 * c + half
            src = qc if half == 0 else _swap_halves(qc)
            q_ref[head // GROUP, head % GROUP] = _low_half(src).astype(BF16)

    pk = _dot(h, w_ref[:, C_KV:C_Z])
    lane = _iota((tm, LANES), 1)
    tpos = ti * tm + _iota((tm, LANES), 0)
    onehot = jnp.where(((tpos >> 6) & (SEL_CHUNK - 1)) == lane - HEAD_DIM, 1.0, 0.0)
    ones_col = jnp.where(lane == HEAD_DIM, 1.0, 0.0)
    for br, out_ref in enumerate((kvc_ref, kvs_ref, kvw_ref)):
        kp = _head_norm_rope(pk[:, br * KV_LANES:br * KV_LANES + LANES], kg_ref[br:br + 1, :], cos2, sin2)
        vv = pk[:, br * KV_LANES + LANES:(br + 1) * KV_LANES]
        out_ref[:, 0:LANES] = kp
        out_ref[:, LANES:KV_LANES] = vv
        if br == 1:
            for g in range(N_KV):
                ksa_ref[g] = _low_half(kp if g == 0 else _swap_halves(kp), onehot).astype(BF16)
                vsa_ref[g] = _low_half(vv if g == 0 else _swap_halves(vv), ones_col).astype(BF16)
        if br == 2:
            for g in range(N_KV):
                kw_ref[g] = _low_half(kp if g == 0 else _swap_halves(kp)).astype(BF16)
                vw_ref[g] = _low_half(vv if g == 0 else _swap_halves(vv), ones_col).astype(BF16)

    sz_ref[...] = _silu(_dot(h, w_ref[:, C_Z:C_G]))
    sg = jax.nn.sigmoid(_dot(h, w_ref[:, C_G:W_COLS]))
    gate_ref[0] = sg
    gate_ref[1] = pltpu.roll(sg, LANES - GROUP * N_BRANCH, 1)


def _prompt_in(x, ng, w_all, cw, cb, qg, kg, cos2, sin2, tm):
    b, t, _ = x.shape
    grid = (b, t // tm)
    row_blk = lambda w: pl.BlockSpec((None, tm, w), lambda bi, ti: (bi, ti, 0))
    full = lambda a: pl.BlockSpec(a.shape, lambda bi, ti: (0,) * a.ndim)
    head_blk = pl.BlockSpec((None, N_KV, tm, LANES), lambda bi, ti: (bi, 0, ti, 0))
    out_shape = (
        jax.ShapeDtypeStruct((b, t, CONV_CH), BF16),
        jax.ShapeDtypeStruct((b, t, ATTN_W), F32),
        jax.ShapeDtypeStruct((b, N_KV, GROUP, t, LANES), BF16),
        jax.ShapeDtypeStruct((b, t, KV_LANES), F32),
        jax.ShapeDtypeStruct((b, t, KV_LANES), F32),
        jax.ShapeDtypeStruct((b, t, KV_LANES), F32),
        jax.ShapeDtypeStruct((b, N_KV, t, LANES), BF16),
        jax.ShapeDtypeStruct((b, N_KV, t, LANES), BF16),
        jax.ShapeDtypeStruct((b, N_KV, t, LANES), BF16),
        jax.ShapeDtypeStruct((b, N_KV, t, LANES), BF16),
        jax.ShapeDtypeStruct((b, N_KV, t, LANES), F32),
        jax.ShapeDtypeStruct((b, 8, CONV_CH), F32),
    )
    out_specs = (
        row_blk(CONV_CH), row_blk(ATTN_W),
        pl.BlockSpec((None, N_KV, GROUP, tm, LANES), lambda bi, ti: (bi, 0, 0, ti, 0)),
        row_blk(KV_LANES), row_blk(KV_LANES), row_blk(KV_LANES),
        head_blk, head_blk, head_blk, head_blk, head_blk,
        pl.BlockSpec((None, 8, CONV_CH), lambda bi, ti: (bi, 0, 0)),
    )
    tab = pl.BlockSpec((tm, LANES), lambda bi, ti: (ti, 0))
    return pl.pallas_call(
        functools.partial(_prompt_in_kernel, tm=tm),
        grid=grid,
        in_specs=[row_blk(D_MODEL), full(ng), full(w_all), full(cw), full(cb), full(qg), full(kg), tab, tab],
        out_specs=out_specs,
        out_shape=out_shape,
        scratch_shapes=[pltpu.VMEM((8, CONV_CH), F32)],
        compiler_params=pltpu.CompilerParams(dimension_semantics=("arbitrary", "arbitrary"),
                                             vmem_limit_bytes=VMEM_LIMIT),
        name="prompt_in_proj",
    )(x, ng, w_all, cw, cb, qg, kg, cos2, sin2)


def _prompt_compress_kernel(x_ref, pe_ref, perm_ref, wbd_ref, kc_ref, vc_ref, rows_ref, acc_ref, *, n_even):
    half = CHUNK_BLOCKS // 2
    chunk_rows = CHUNK_BLOCKS * CMP_BLK
    n_chunks = 2 * n_even // CHUNK_BLOCKS

    def chunk_body(it, carry):
        for k in range(CHUNK_UNROLL):
            ch = it * CHUNK_UNROLL + k
            x = x_ref[pl.ds(pl.multiple_of(ch * chunk_rows, chunk_rows), chunk_rows), :] + pe_ref[...]
            x_perm = _dot(perm_ref[...], x.astype(BF16))
            base = pl.multiple_of(ch * half, half)
            for r in range(CMP_BLK):
                rows_ref[r, pl.ds(base, half), :] = x_perm[r * CHUNK_BLOCKS:r * CHUNK_BLOCKS + half, :]
                rows_ref[r, pl.ds(n_even + base, half), :] = x_perm[r * CHUNK_BLOCKS + half:(r + 1) * CHUNK_BLOCKS, :]
        return carry

    lax.fori_loop(0, n_chunks // CHUNK_UNROLL, chunk_body, 0)
    acc_ref[...] = jnp.zeros_like(acc_ref)

    def compress_body(it, carry):
        acc = acc_ref[...]
        for k in range(CHUNK_UNROLL):
            r = it * CHUNK_UNROLL + k
            acc = acc + _dot(rows_ref[r].astype(BF16), wbd_ref[r])
        acc_ref[...] = acc
        return carry

    lax.fori_loop(0, CMP_BLK // CHUNK_UNROLL, compress_body, 0)
    kk = acc_ref[:, 0:LANES]
    vv = acc_ref[:, LANES:KV_LANES]
    for g in range(N_KV):
        kc_ref[g] = _low_half(kk if g == 0 else _swap_halves(kk)).astype(BF16)
        vc_ref[g] = _low_half(vv if g == 0 else _swap_halves(vv)).astype(BF16)


def _prompt_compress(kvc, pe_chunk, perm_chunk, wbd):
    b, t, _ = kvc.shape
    nc = t // CMP_BLK
    assert nc % (CHUNK_BLOCKS * CHUNK_UNROLL) == 0 and CMP_BLK % CHUNK_UNROLL == 0
    out = jax.ShapeDtypeStruct((b, N_KV, nc, LANES), BF16)
    blk = pl.BlockSpec((None, N_KV, nc, LANES), lambda bi: (bi, 0, 0, 0))
    const = lambda a: pl.BlockSpec(a.shape, lambda bi: (0,) * a.ndim)
    return pl.pallas_call(
        functools.partial(_prompt_compress_kernel, n_even=nc // 2),
        grid=(b,),
        in_specs=[pl.BlockSpec((None, t, KV_LANES), lambda bi: (bi, 0, 0)),
                  const(pe_chunk), const(perm_chunk), const(wbd)],
        out_specs=(blk, blk),
        out_shape=(out, out),
        scratch_shapes=[pltpu.VMEM((CMP_BLK, nc, KV_LANES), F32), pltpu.VMEM((nc, KV_LANES), F32)],
        compiler_params=pltpu.CompilerParams(dimension_semantics=("arbitrary",), vmem_limit_bytes=VMEM_LIMIT),
        name="prompt_compress",
    )(kvc, pe_chunk, perm_chunk, wbd)


def _select_blocks(score, blk_f, n_rounds):
    sel = jnp.zeros(score.shape, F32)
    picks = []
    for _ in range(n_rounds):
        m = jnp.max(score, axis=-1, keepdims=True)
        first = jnp.min(jnp.where(score == m, blk_f, 1e9), axis=-1, keepdims=True)
        hit = blk_f == first
        sel = jnp.where(hit, 1.0, sel)
        score = jnp.where(hit, -jnp.inf, score)
        picks.append(first)
    return sel, picks


def _selection_bias(imp, qs):
    tq, ns = imp.shape
    n_chunks = tq // LANES
    imp_t = jnp.concatenate([imp[r * LANES:(r + 1) * LANES, :].T for r in range(n_chunks)], axis=1)
    blk = _iota((ns, tq), 0)
    qp = qs + _iota((1, tq), 1)
    qb = qp >> 6
    valid = blk * SEL_BLK <= qp
    forced = ((blk == 0) | (blk == qb) | (blk == qb - 1)) & valid
    blk_f = blk.astype(F32)
    candidate = valid & jnp.logical_not(forced)
    score = jnp.where(candidate, imp_t, -jnp.inf)
    for _ in range(min(N_SEL, ns) - 3):
        m = jnp.max(score, axis=0, keepdims=True)
        first = jnp.min(jnp.where(score == m, blk_f, 1e9), axis=0, keepdims=True)
        score = jnp.where(blk_f == first, -jnp.inf, score)
    bias_t = jnp.where(forced | (candidate & (score == -jnp.inf)), 0.0, NEG_BIAS)
    return jnp.concatenate([bias_t[:, r * LANES:(r + 1) * LANES].T for r in range(n_chunks)], axis=0)


def _prepare_query_tile(q_ref, kc_ref, vc_ref, kw_ref, vw_ref, qa_ref, ocmp_ref, owin_ref, tile, dst, *, tq, t):
    qs = tile * tq
    rows = GROUP * tq
    nc = t // CMP_BLK
    ns = nc // 2
    n_variants = ns // SEL_CHUNK
    q = q_ref[...].reshape(rows, LANES)
    qpos = qs + (_iota((rows, 1), 0) & (tq - 1))

    col = _iota((1, nc), 1)
    cblk = jnp.where(col < ns, 2 * col, 2 * (col - ns) + 1)
    s_cmp = jnp.where((cblk + 1) * CMP_BLK - 1 <= qpos, _dot_nt(q, kc_ref[...]), -jnp.inf)
    m_cmp = jnp.max(s_cmp, axis=-1, keepdims=True)
    e_cmp = jnp.exp2(s_cmp - jnp.where(m_cmp > -jnp.inf, m_cmp, 0.0))
    inv_cmp = 1.0 / jnp.maximum(jnp.sum(e_cmp, axis=-1, keepdims=True), 1e-30)
    ocmp_ref[dst] = _dot(e_cmp.astype(BF16), vc_ref[...]) * inv_cmp

    pair = (e_cmp[:, 0:ns] + e_cmp[:, ns:nc]) * inv_cmp
    imp = pair[0:tq]
    for n in range(1, GROUP):
        imp = imp + pair[n * tq:(n + 1) * tq]
    bias = _selection_bias(imp, qs)

    for c in range(ns // SEL_CHUNK):
        bc = bias[:, (c // 2) * LANES:(c // 2 + 1) * LANES]
        if c % 2 == 0:
            bc = _swap_halves(bc)
        bc = jnp.concatenate([bc] * GROUP, axis=0).astype(BF16)
        qa_ref[dst * n_variants + c] = _low_half(q, bc)

    q_win = jnp.where(_iota((rows, LANES), 1) == HEAD_DIM, NEG_BIAS, q.astype(F32)).astype(BF16)
    r0 = _iota((GROUP * LANES, 1), 0) & (LANES - 1)
    upper = _iota((1, LANES), 1) > r0
    n_mid = WINDOW // LANES - 1
    o_win_blocks = []
    for a in range(tq // LANES):
        q_a = jnp.concatenate([q_win[n * tq + a * LANES:n * tq + (a + 1) * LANES] for n in range(GROUP)], axis=0)
        base = pl.multiple_of(qs + a * LANES, LANES)
        s_a = _dot_nt(q_a, kw_ref[pl.ds(base, WINDOW + LANES), :])
        s_a = jnp.concatenate([jnp.where(upper, s_a[:, 0:LANES], NEG_MASK), s_a[:, LANES:(n_mid + 1) * LANES],
                               jnp.where(upper, NEG_MASK, s_a[:, (n_mid + 1) * LANES:])], axis=1)
        e_a = jnp.exp2(s_a - jnp.max(s_a, axis=-1, keepdims=True))
        acc_a = _dot(e_a.astype(BF16), vw_ref[pl.ds(base, WINDOW + LANES), :])
        o_win_blocks.append(_low_half(acc_a * (1.0 / acc_a[:, HEAD_DIM:HEAD_DIM + 1])))
    owin_ref[dst] = jnp.concatenate(
        [blk_a[n * LANES:(n + 1) * LANES] for n in range(GROUP) for blk_a in o_win_blocks], axis=0)


def _prompt_attn_kernel(q_ref, kc_ref, vc_ref, ksa_ref, vsa_ref, kw_ref, vw_ref, gate_ref, sz_ref, out_ref,
                        qa_ref, ocmp_ref, owin_ref, sa_ref, sb_ref, m_ref, acc_ref, *, tq, tk, t):
    i = pl.program_id(2)
    qs = i * tq
    rows = GROUP * tq
    n_variants = t // (SEL_CHUNK * SEL_BLK)
    cur = 0
    _prepare_query_tile(q_ref, kc_ref, vc_ref, kw_ref, vw_ref, qa_ref, ocmp_ref, owin_ref, i, cur, tq=tq, t=t)
    qpos = qs + (_iota((rows, 1), 0) & (tq - 1))

    m_ref[...] = jnp.full_like(m_ref, NEG_MASK)
    acc_ref[...] = jnp.zeros_like(acc_ref)

    def scores(kt):
        k0 = pl.multiple_of(kt * tk, tk)
        return _dot_nt(qa_ref[cur * n_variants + kt // (SEL_CHUNK * SEL_BLK // tk)], ksa_ref[pl.ds(k0, tk), :])

    def update(s, kt, causal):
        k0 = pl.multiple_of(kt * tk, tk)
        if causal:
            s = jnp.where(k0 + _iota((1, tk), 1) <= qpos, s, NEG_MASK)
        m_old = m_ref[...]
        m_new = jnp.maximum(m_old, jnp.broadcast_to(jnp.max(s, axis=-1, keepdims=True), m_old.shape))
        p = jnp.exp2(s - jnp.concatenate([m_new] * (tk // LANES), axis=1))
        acc_ref[...] = jnp.exp2(m_old - m_new) * acc_ref[...] + _dot(p.astype(BF16), vsa_ref[pl.ds(k0, tk), :])
        m_ref[...] = m_new

    n_kt = (qs + tq + tk - 1) // tk
    n_pairs = (n_kt - 1) // 2
    sa_ref[...] = scores(0)

    def pair_step(j, carry):
        sb_ref[...] = scores(2 * j + 1)
        update(sa_ref[...], 2 * j, False)
        sa_ref[...] = scores(2 * j + 2)
        update(sb_ref[...], 2 * j + 1, False)
        return carry

    lax.fori_loop(0, n_pairs, pair_step, 0)
    last = n_kt - 1

    @pl.when(n_kt - 2 * n_pairs == 2)
    def _():
        update(sa_ref[...], last - 1, False)
        sa_ref[...] = scores(last)

    o_cmp = ocmp_ref[cur]
    o_win = owin_ref[cur]
    update(sa_ref[...], last, True)

    acc = acc_ref[...]
    o_slc = _low_half(acc * (1.0 / acc[:, HEAD_DIM:HEAD_DIM + 1]))

    gate = gate_ref[...]
    mixed = []
    for n in range(GROUP):
        r = slice(n * tq, (n + 1) * tq)
        mixed.append(gate[:, 3 * n:3 * n + 1] * o_cmp[r] + gate[:, 3 * n + 1:3 * n + 2] * o_slc[r]
                     + gate[:, 3 * n + 2:3 * n + 3] * o_win[r])
    for c in range(GROUP // 2):
        o_pair = mixed[2 * c] + _swap_halves(mixed[2 * c + 1])
        out_ref[:, c * LANES:(c + 1) * LANES] = (o_pair * sz_ref[:, c * LANES:(c + 1) * LANES]).astype(BF16)


def _prompt_attn(q, kc, vc, ksa, vsa, kw, vw, gate, sz, tq, tk):
    b, _, _, t, _ = q.shape
    assert t % (2 * SEL_CHUNK * SEL_BLK) == 0 and t >= WINDOW + tq and (SEL_CHUNK * SEL_BLK) % tk == 0
    assert tq % LANES == 0 and N_SEL > 3
    nc = t // CMP_BLK
    rows = GROUP * tq
    seq = lambda n: pl.BlockSpec((None, None, n, LANES), lambda bi, g, i: (bi, g, 0, 0))
    return pl.pallas_call(
        functools.partial(_prompt_attn_kernel, tq=tq, tk=tk, t=t),
        grid=(b, N_KV, t // tq),
        in_specs=[pl.BlockSpec((None, None, GROUP, tq, LANES), lambda bi, g, i: (bi, g, 0, i, 0)),
                  seq(nc), seq(nc), seq(t), seq(t), seq(t + WINDOW), seq(t + WINDOW),
                  pl.BlockSpec((None, None, tq, LANES), lambda bi, g, i: (bi, g, i, 0)),
                  pl.BlockSpec((None, tq, GROUP * HEAD_DIM), lambda bi, g, i: (bi, i, g))],
        out_specs=pl.BlockSpec((None, tq, GROUP * HEAD_DIM), lambda bi, g, i: (bi, i, g)),
        out_shape=jax.ShapeDtypeStruct((b, t, ATTN_W), BF16),
        scratch_shapes=[pltpu.VMEM((t // (SEL_CHUNK * SEL_BLK), rows, LANES), BF16),
                        pltpu.VMEM((1, rows, LANES), F32), pltpu.VMEM((1, rows, LANES), F32),
                        pltpu.VMEM((rows, tk), F32), pltpu.VMEM((rows, tk), F32),
                        pltpu.VMEM((rows, LANES), F32), pltpu.VMEM((rows, LANES), F32)],
        compiler_params=pltpu.CompilerParams(dimension_semantics=("arbitrary", "arbitrary", "arbitrary"),
                                             vmem_limit_bytes=VMEM_LIMIT),
        name="prompt_attention",
    )(q, kc, vc, ksa, vsa, kw, vw, gate, sz)


def _out_proj_kernel(x_ref, convb_ref, attnb_ref, w_ref, y_ref):
    mix = jnp.concatenate([convb_ref[...], attnb_ref[...]], axis=1)
    y_ref[...] = x_ref[...] + _dot(mix, w_ref[...])


def _out_proj(x2d, convb, attnb, w_out, tm):
    m = x2d.shape[0]
    blk = lambda w: pl.BlockSpec((tm, w), lambda i: (i, 0))
    return pl.pallas_call(
        _out_proj_kernel,
        grid=(m // tm,),
        in_specs=[blk(D_MODEL), blk(CONV_CH), blk(ATTN_W), pl.BlockSpec(w_out.shape, lambda i: (0, 0))],
        out_specs=blk(D_MODEL),
        out_shape=jax.ShapeDtypeStruct((m, D_MODEL), F32),
        compiler_params=pltpu.CompilerParams(dimension_semantics=("arbitrary",), vmem_limit_bytes=VMEM_LIMIT),
        name="out_proj",
    )(x2d, convb, attnb, w_out)


def _sample_in_kernel(x_ref, ng_ref, w_ref, cw_ref, cb_ref, qg_ref, kg_ref, cos_ref, sin_ref, c0_ref, c1_ref,
                      convb_ref, sz_ref, qbd_ref, kvn_ref, gate_ref, u_ref):
    h = _normed_input(x_ref[...], ng_ref[...])
    cos2 = cos_ref[...]
    sin2 = sin_ref[...]
    pc = _dot(h, w_ref[:, C_CONV:C_Q])
    b_gate = pc[:, 0:CONV_CH]
    u = pc[:, CONV_CH:2 * CONV_CH] * pc[:, 2 * CONV_CH:3 * CONV_CH]
    z_conv = pc[:, 3 * CONV_CH:4 * CONV_CH]
    conv_y = cw_ref[0:1, :] * c0_ref[...] + cw_ref[1:2, :] * c1_ref[...] + cw_ref[2:3, :] * u + cb_ref[...]
    convb_ref[...] = (b_gate * conv_y * _silu(z_conv)).astype(BF16)
    u_ref[...] = u

    pq = _dot(h, w_ref[:, C_Q:C_KV])
    zeros = jnp.zeros((x_ref.shape[0], LANES), BF16)
    for c in range(N_HEADS // 2):
        qc = _head_norm_rope(pq[:, c * LANES:(c + 1) * LANES], qg_ref[...], cos2, sin2) * Q_SCALE
        lane = _iota(qc.shape, 1)
        for half in range(2):
            head = 2 * c + half
            g = head // GROUP
            src = qc if half == g else _swap_halves(qc)
            keep = (lane >= g * HEAD_DIM) & (lane < (g + 1) * HEAD_DIM)
            qbd_ref[head] = jnp.concatenate([jnp.where(keep, src, 0.0).astype(BF16), zeros], axis=1)

    pk = _dot(h, w_ref[:, C_KV:C_Z])
    for br in range(N_BRANCH):
        kvn_ref[:, br * KV_LANES:br * KV_LANES + LANES] = _head_norm_rope(
            pk[:, br * KV_LANES:br * KV_LANES + LANES], kg_ref[br:br + 1, :], cos2, sin2)
        kvn_ref[:, br * KV_LANES + LANES:(br + 1) * KV_LANES] = pk[:, br * KV_LANES + LANES:(br + 1) * KV_LANES]

    sz_ref[...] = _silu(_dot(h, w_ref[:, C_Z:C_G]))
    gate_ref[...] = jax.nn.sigmoid(_dot(h, w_ref[:, C_G:W_COLS]))


def _sample_in(x, ng, w_all, cw, cb, qg, kg, cos1, sin1, c0, c1):
    nb = x.shape[0]
    out_shape = (
        jax.ShapeDtypeStruct((nb, CONV_CH), BF16),
        jax.ShapeDtypeStruct((nb, ATTN_W), F32),
        jax.ShapeDtypeStruct((N_HEADS, nb, 2 * LANES), BF16),
        jax.ShapeDtypeStruct((nb, N_BRANCH * KV_LANES), F32),
        jax.ShapeDtypeStruct((nb, LANES), F32),
        jax.ShapeDtypeStruct((nb, CONV_CH), F32),
    )
    return pl.pallas_call(
        _sample_in_kernel,
        out_shape=out_shape,
        compiler_params=pltpu.CompilerParams(vmem_limit_bytes=VMEM_LIMIT),
        name="sample_in_proj",
    )(x, ng, w_all, cw, cb, qg, kg, cos1, sin1, c0, c1)


def _sample_cmp_kernel(pt_ref, qbd_ref, cache_ref, pet_ref, perm_ref, wbd_ref, ocmp_ref, imp_ref,
                       buf_ref, rows_ref, acc_ref, sem, *, n_batch, n_pages, past):
    b = pl.program_id(0)
    n_chunks = n_pages // PAGES_PER_CHUNK
    blocks_per_pair = 2 * PAGE // CMP_BLK

    def page_copy(bb, ch, p, slot):
        return pltpu.make_async_copy(cache_ref.at[pt_ref[bb, ch * PAGES_PER_CHUNK + p]], buf_ref.at[slot, p],
                                     sem.at[slot])

    def start_chunk(bb, ch, slot):
        def body(p, carry):
            page_copy(bb, ch, p, slot).start()
            return carry
        lax.fori_loop(0, PAGES_PER_CHUNK, body, 0)

    def wait_chunk(ch, slot):
        def body(p, carry):
            page_copy(b, ch, p, slot).wait()
            return carry
        lax.fori_loop(0, PAGES_PER_CHUNK, body, 0)

    def stream_slot(offset):
        return (b * n_chunks + offset) % CHUNK_SLOTS

    def start_ahead(offset):
        bb_off, ch = divmod(offset, n_chunks)
        if bb_off == 0:
            start_chunk(b, ch, stream_slot(offset))
        else:
            @pl.when(b + bb_off < n_batch)
            def _():
                start_chunk(b + bb_off, ch, stream_slot(offset))

    @pl.when(b == 0)
    def _():
        for offset in range(CHUNK_SLOTS - 1):
            start_ahead(offset)

    for ch in range(n_chunks):
        slot = stream_slot(ch)
        start_ahead(ch + CHUNK_SLOTS - 1)
        wait_chunk(ch, slot)

        def pairs_body(it, carry):
            for k in range(PAIR_UNROLL):
                pr = it * PAIR_UNROLL + k
                xt = jnp.concatenate([buf_ref[slot, 2 * pr], buf_ref[slot, 2 * pr + 1]], axis=1) + pet_ref[...]
                x_perm = _dot_nt(perm_ref[...], xt.astype(BF16))
                base = pl.multiple_of((ch * (PAGES_PER_CHUNK // 2) + pr) * blocks_per_pair, blocks_per_pair)
                for r in range(CMP_BLK):
                    rows_ref[r, pl.ds(base, blocks_per_pair), :] = (
                        x_perm[r * blocks_per_pair:(r + 1) * blocks_per_pair, :])
            return carry

        lax.fori_loop(0, PAGES_PER_CHUNK // 2 // PAIR_UNROLL, pairs_body, 0)

    acc_ref[...] = jnp.zeros_like(acc_ref)

    def compress_body(it, carry):
        acc = acc_ref[...]
        for k in range(PAIR_UNROLL):
            r = it * PAIR_UNROLL + k
            acc = acc + _dot(rows_ref[r].astype(BF16), wbd_ref[r])
        acc_ref[...] = acc
        return carry

    lax.fori_loop(0, CMP_BLK // PAIR_UNROLL, compress_body, 0)

    nc = past // CMP_BLK
    kv = acc_ref[...].astype(BF16)
    qbd = qbd_ref[...]
    p_cmp = _masked_softmax(_dot_nt(qbd, kv), (_iota((1, nc), 1) + 1) * CMP_BLK - 1 <= past)
    ocmp_ref[...] = _dot(p_cmp.astype(BF16), kv)

    pair = p_cmp + pltpu.roll(p_cmp, nc - 1, 1)
    row = _iota((N_HEADS, nc), 0)
    imp_ref[...] = jnp.where(row == 0, jnp.sum(jnp.where(row < GROUP, pair, 0.0), axis=0, keepdims=True),
                             jnp.sum(jnp.where(row >= GROUP, pair, 0.0), axis=0, keepdims=True))


def _sample_cmp(page_table, qbd, cache_t, pe_t, perm, wbd, past):
    nb, n_pages = page_table.shape
    assert n_pages % PAGES_PER_CHUNK == 0 and n_pages // PAGES_PER_CHUNK >= CHUNK_SLOTS - 1
    nc = past // CMP_BLK
    const = lambda a: pl.BlockSpec(a.shape, lambda b, pt: (0,) * a.ndim)
    grid_spec = pltpu.PrefetchScalarGridSpec(
        num_scalar_prefetch=1,
        grid=(nb,),
        in_specs=[pl.BlockSpec((None, N_HEADS, 2 * LANES), lambda b, pt: (b, 0, 0)),
                  pl.BlockSpec(memory_space=pl.ANY), const(pe_t), const(perm), const(wbd)],
        out_specs=(pl.BlockSpec((None, N_HEADS, KV_LANES), lambda b, pt: (b, 0, 0)),
                   pl.BlockSpec((None, N_HEADS, nc), lambda b, pt: (b, 0, 0))),
        scratch_shapes=[pltpu.VMEM((CHUNK_SLOTS, PAGES_PER_CHUNK, KV_LANES, PAGE), F32),
                        pltpu.VMEM((CMP_BLK, nc, KV_LANES), F32),
                        pltpu.VMEM((nc, KV_LANES), F32), pltpu.SemaphoreType.DMA((CHUNK_SLOTS,))],
    )
    return pl.pallas_call(
        functools.partial(_sample_cmp_kernel, n_batch=nb, n_pages=n_pages, past=past),
        grid_spec=grid_spec,
        out_shape=(jax.ShapeDtypeStruct((nb, N_HEADS, KV_LANES), F32),
                   jax.ShapeDtypeStruct((nb, N_HEADS, nc), F32)),
        compiler_params=pltpu.CompilerParams(dimension_semantics=("arbitrary",), vmem_limit_bytes=VMEM_LIMIT),
        name="sample_compress",
    )(page_table, qbd, cache_t, pe_t, perm, wbd)


def _sample_select_kernel(imp_ref, idx_ref, *, past):
    imp = imp_ref[...]
    lane = _iota(imp.shape, 1)
    blk = lane >> 1
    qb = past // SEL_BLK
    forced = (blk == 0) | (blk == qb) | (blk == qb - 1)
    valid = ((lane & 1) == 0) & (blk * SEL_BLK <= past)
    score = jnp.where(valid, imp + jnp.where(forced, FORCE_BONUS, 0.0), -jnp.inf)
    _, picks = _select_blocks(score, lane.astype(F32), N_SEL - 1)
    out_lane = _iota(idx_ref.shape, 1)
    idx = jnp.zeros(idx_ref.shape, F32)
    for k, pick in enumerate(picks):
        idx = jnp.where(out_lane == k, pick * 0.5, idx)
    idx_ref[...] = idx.astype(jnp.int32)


def _sample_select(imp, past):
    return pl.pallas_call(
        functools.partial(_sample_select_kernel, past=past),
        out_shape=jax.ShapeDtypeStruct((imp.shape[0], LANES), jnp.int32),
        compiler_params=pltpu.CompilerParams(vmem_limit_bytes=VMEM_LIMIT),
        name="sample_select",
    )(imp)


def _sample_attn_kernel(pt_ref, sel_ref, qbd_ref, cache_ref, kvn_ref, win_ref, ocmp_ref, gate_ref, out_ref,
                        buf_ref, sem, *, n_batch, past):
    b = pl.program_id(0)
    n_pick = N_SEL - 1
    blocks_per_page = PAGE // SEL_BLK
    slot = b % 2

    def page_copy(bb, j, s):
        return pltpu.make_async_copy(cache_ref.at[pt_ref[bb, sel_ref[bb, j] >> 1]], buf_ref.at[s, j], sem.at[s])

    @pl.when(b == 0)
    def _():
        for j in range(N_KV * n_pick):
            page_copy(0, j, 0).start()

    @pl.when(b + 1 < n_batch)
    def _():
        for j in range(N_KV * n_pick):
            page_copy(b + 1, j, 1 - slot).start()

    qbd = qbd_ref[...]
    qf = qbd.astype(F32)
    row = _iota((N_HEADS, 1), 0)

    def new_token(offset):
        kv_new = kvn_ref[:, offset:offset + KV_LANES].astype(BF16).astype(F32)
        return jnp.sum(qf * kv_new, axis=-1, keepdims=True), kv_new

    def attend(keys_t, mask, s_new, kv_new):
        s = jnp.where(mask, _dot(qbd, keys_t), -jnp.inf)
        m = jnp.maximum(jnp.max(s, axis=-1, keepdims=True), s_new)
        p = jnp.exp2(s - m)
        p_new = jnp.exp2(s_new - m)
        norm = 1.0 / (jnp.sum(p, axis=-1, keepdims=True) + p_new)
        return (_dot_nt(p.astype(BF16), keys_t) + p_new.astype(BF16).astype(F32) * kv_new) * norm

    w_buf = win_ref.shape[1]
    s_new, kv_new = new_token(2 * KV_LANES)
    x_win = attend(win_ref[...].astype(BF16), _iota((1, w_buf), 1) > w_buf - WINDOW, s_new, kv_new)

    for j in range(N_KV * n_pick):
        page_copy(b, j, slot).wait()

    s_new, kv_new = new_token(KV_LANES)
    lane = _iota((1, n_pick * PAGE), 1)
    x_slc = []
    for g in range(N_KV):
        slots = range(g * n_pick, (g + 1) * n_pick)
        keys_t = jnp.concatenate([buf_ref[slot, j] for j in slots], axis=1).astype(BF16)
        half = jnp.concatenate([jnp.full((1, PAGE), sel_ref[b, j] & (blocks_per_page - 1), jnp.int32) for j in slots],
                               axis=1)
        x_slc.append(attend(keys_t, ((lane & (PAGE - 1)) >> (SEL_BLK.bit_length() - 1)) == half, s_new, kv_new))
    x_slc = jnp.where(row < GROUP, x_slc[0], x_slc[1])

    gate = jnp.broadcast_to(gate_ref[...], (N_HEADS, LANES))
    lane = _iota((N_HEADS, LANES), 1)
    mixed = jnp.zeros((N_HEADS, KV_LANES), F32)
    for br, x_br in enumerate((ocmp_ref[...], x_slc, x_win)):
        g_col = jnp.sum(jnp.where(lane == N_BRANCH * _iota((N_HEADS, LANES), 0) + br, gate, 0.0), axis=-1, keepdims=True)
        mixed = mixed + g_col * x_br
    v_lo = mixed[:, 2 * HEAD_DIM:3 * HEAD_DIM]
    v_hi = mixed[:, 3 * HEAD_DIM:4 * HEAD_DIM]
    out_ref[...] = jnp.where(row < GROUP, v_lo, v_hi)


def _sample_attn(page_table, sel, qbd, cache, kvn, win, ocmp, gate, past):
    nb = page_table.shape[0]
    n_pick = N_SEL - 1
    per_b = lambda *shape: pl.BlockSpec((None,) + shape, lambda b, pt, s: (b,) + (0,) * len(shape))
    grid_spec = pltpu.PrefetchScalarGridSpec(
        num_scalar_prefetch=2,
        grid=(nb,),
        in_specs=[per_b(N_HEADS, 2 * LANES),
                  pl.BlockSpec(memory_space=pl.ANY),
                  per_b(1, N_BRANCH * KV_LANES),
                  per_b(KV_LANES, win.shape[2]),
                  per_b(N_HEADS, KV_LANES),
                  per_b(1, LANES)],
        out_specs=per_b(N_HEADS, HEAD_DIM),
        scratch_shapes=[pltpu.VMEM((2, N_KV * n_pick, KV_LANES, PAGE), F32), pltpu.SemaphoreType.DMA((2,))],
    )
    return pl.pallas_call(
        functools.partial(_sample_attn_kernel, n_batch=nb, past=past),
        grid_spec=grid_spec,
        out_shape=jax.ShapeDtypeStruct((nb, N_HEADS, HEAD_DIM), F32),
        compiler_params=pltpu.CompilerParams(dimension_semantics=("arbitrary",), vmem_limit_bytes=VMEM_LIMIT),
        name="sample_attention",
    )(page_table, sel, qbd, cache, kvn, win, ocmp, gate)


def _sample_out_kernel(x_ref, convb_ref, attn_ref, sz_ref, w_ref, y_ref):
    mix = jnp.concatenate([convb_ref[...], (attn_ref[...] * sz_ref[...]).astype(BF16)], axis=1)
    y_ref[...] = x_ref[...] + _dot(mix, w_ref[...])


def _sample_out(x, convb, attn, sz, w_out):
    return pl.pallas_call(
        _sample_out_kernel,
        out_shape=jax.ShapeDtypeStruct(x.shape, F32),
        compiler_params=pltpu.CompilerParams(vmem_limit_bytes=VMEM_LIMIT),
        name="sample_out_proj",
    )(x, convb, attn, sz, w_out)


def _rope_tables(pos):
    half = HEAD_DIM // 2
    inv = ROPE_THETA ** (-jnp.arange(half, dtype=F32) / half)
    ang = pos.astype(F32)[:, None] * inv[None, :]
    cos, sin = lax.optimization_barrier((jnp.cos(ang), jnp.sin(ang)))
    return jnp.concatenate([cos, cos, cos, cos], axis=1), jnp.concatenate([-sin, sin, -sin, sin], axis=1)


def _layer_params(norm_g, w_in, conv_w, conv_b, q_gain, k_gain, cmp_pe, cmp_w, w_out):
    c_gates = C_KV + N_BRANCH * KV_LANES
    n_gates = N_HEADS * N_BRANCH
    w_all = jnp.concatenate([w_in[:, :c_gates], w_in[:, c_gates + n_gates:], w_in[:, c_gates:c_gates + n_gates],
                             jnp.zeros((D_MODEL, LANES - n_gates), w_in.dtype)], axis=1).astype(BF16)
    w16 = cmp_w.astype(BF16)
    wbd = jnp.concatenate(
        [jnp.pad(w16[:, j], ((0, 0), (0, 0), (o, KV_LANES - HEAD_DIM - o)))
         for j in range(2) for o in ((j * N_KV + g) * HEAD_DIM for g in range(N_KV))], axis=1)
    pe_row = jnp.broadcast_to(cmp_pe[:, :, None, :], (CMP_BLK, 2, N_KV, HEAD_DIM)).reshape(CMP_BLK, KV_LANES)
    blocks_per_pair = 2 * PAGE // CMP_BLK
    pe_t = jnp.tile(pe_row.T, (1, blocks_per_pair))
    m = jnp.arange(2 * PAGE)
    perm = (m[None, :] == (CMP_BLK * (m % blocks_per_pair) + m // blocks_per_pair)[:, None]).astype(BF16)
    pe_chunk = jnp.tile(pe_row, (CHUNK_BLOCKS, 1))
    mc = jnp.arange(CHUNK_BLOCKS * CMP_BLK)
    e = mc % CHUNK_BLOCKS
    blk_of_e = jnp.where(e < CHUNK_BLOCKS // 2, 2 * e, 2 * (e - CHUNK_BLOCKS // 2) + 1)
    perm_chunk = (mc[None, :] == (CMP_BLK * blk_of_e + mc // CHUNK_BLOCKS)[:, None]).astype(BF16)
    return dict(
        ng=norm_g.reshape(1, D_MODEL), w_all=w_all, cw=conv_w, cb=conv_b.reshape(1, CONV_CH),
        qg=jnp.tile(q_gain, 2).reshape(1, LANES), kg=jnp.tile(k_gain, (1, 2)),
        pe_chunk=pe_chunk, perm_chunk=perm_chunk, pe_t=pe_t, perm=perm, wbd=wbd, w_out=w_out.astype(BF16))


def _kv_rows(a):
    return a.reshape(a.shape[:-1] + (2, N_KV, HEAD_DIM))


def _prompt_layer(x, p):
    b, t, _ = x.shape
    cos2, sin2 = _rope_tables(jnp.arange(t, dtype=jnp.int32))
    (convb, sz, q, kvc, kvs, kvw, ksa, vsa, kw, vw, gate, utail) = _prompt_in(
        x, p["ng"], p["w_all"], p["cw"], p["cb"], p["qg"], p["kg"], cos2, sin2, tm=512)
    kc, vc = _prompt_compress(kvc, p["pe_chunk"], p["perm_chunk"], p["wbd"])
    pad_k = jnp.zeros((b, N_KV, WINDOW, LANES), BF16).at[..., HEAD_DIM].set(1.0)
    kw = jnp.concatenate([pad_k, kw], axis=2)
    vw = jnp.concatenate([jnp.zeros((b, N_KV, WINDOW, LANES), BF16), vw], axis=2)
    attnb = _prompt_attn(q, kc, vc, ksa, vsa, kw, vw, gate, sz, tq=256, tk=1024)
    y = _out_proj(x.reshape(b * t, D_MODEL), convb.reshape(b * t, CONV_CH), attnb.reshape(b * t, ATTN_W),
                  p["w_out"], tm=1024).reshape(b, t, D_MODEL)
    w_keep = min(WINDOW, t)
    return (y, _kv_rows(kvc), _kv_rows(kvs), _kv_rows(kvw[:, t - w_keep:]), utail[:, 8 - (CONV_W - 1):])


def _sample_layer(x, cache_cmp, cache_slc, win_buf, conv_buf, page_table, p):
    nb, t, _ = x.shape
    assert t == 1
    n_pages = page_table.shape[1]
    past = n_pages * PAGE
    assert past % (2 * SEL_BLK) == 0 and past // SEL_BLK >= LANES and past // SEL_BLK >= N_SEL
    n_pool = cache_cmp.shape[0]
    cos1, sin1 = _rope_tables(jnp.full((1,), past, dtype=jnp.int32))
    convb, sz, qbd, kvn, gate, u = _sample_in(
        x.reshape(nb, D_MODEL), p["ng"], p["w_all"], p["cw"], p["cb"], p["qg"], p["kg"], cos1, sin1,
        conv_buf[:, 0], conv_buf[:, 1])
    qbd = jnp.transpose(qbd, (1, 0, 2))
    feature_major = lambda a: jnp.transpose(a, (0, 2, 3, 4, 1)).reshape(a.shape[0], KV_LANES, a.shape[1])
    ocmp, imp = _sample_cmp(page_table, qbd, feature_major(cache_cmp), p["pe_t"], p["perm"], p["wbd"], past)
    idx = _sample_select(imp[:, :N_KV].reshape(nb * N_KV, past // CMP_BLK), past)
    sel = idx[:, :N_SEL - 1].reshape(nb, N_KV * (N_SEL - 1))
    attn = _sample_attn(page_table, sel, qbd, feature_major(cache_slc), kvn.reshape(nb, 1, N_BRANCH * KV_LANES),
                        feature_major(win_buf), ocmp, gate.reshape(nb, 1, LANES), past)
    y = _sample_out(x.reshape(nb, D_MODEL), convb, attn.reshape(nb, ATTN_W), sz, p["w_out"]).reshape(nb, 1, D_MODEL)
    kv_new = _kv_rows(kvn.reshape(nb, 1, N_BRANCH, KV_LANES))
    win_new = jnp.concatenate([win_buf, kv_new[:, :, 2]], axis=1)[:, t:]
    conv_new = jnp.concatenate([conv_buf, u.reshape(nb, 1, CONV_CH)], axis=1)[:, t:]
    return y, kv_new[:, :, 0], kv_new[:, :, 1], win_new, conv_new


def kernel(x_prompt, x_sample, cache_cmp_kv, cache_slc_kv, state_win_kv, state_conv, page_table, norm_g, w_in,
           conv_w, conv_b, q_gain, k_gain, cmp_pe, cmp_w, w_out):
    yp, ys = x_prompt, x_sample
    outs = [[] for _ in range(8)]
    for layer in range(norm_g.shape[0]):
        p = _layer_params(norm_g[layer], w_in[layer], conv_w[layer], conv_b[layer], q_gain[layer], k_gain[layer],
                          cmp_pe[layer], cmp_w[layer], w_out[layer])
        yp, *prompt_state = _prompt_layer(yp, p)
        ys, *sample_state = _sample_layer(ys, cache_cmp_kv[layer], cache_slc_kv[layer], state_win_kv[layer],
                                          state_conv[layer], page_table, p)
        for acc, a in zip(outs, prompt_state + sample_state):
            acc.append(a)
    return (yp, ys) + tuple(jnp.stack(a) for a in outs)
```

```python
import functools

import jax
import jax.numpy as jnp
from jax import lax
from jax.experimental import pallas as pl
from jax.experimental.pallas import tpu as pltpu

F32 = jnp.float32
BF16 = jnp.bfloat16

D_MODEL = 1024
CONV_CH = 512
CONV_W = 3
N_HEADS = 8
HEAD_DIM = 64
ATTN_W = N_HEADS * HEAD_DIM
N_KV = 2
GROUP = N_HEADS // N_KV
N_BRANCH = 3
CMP_BLK = 32
SEL_BLK = 64
N_SEL = 16
WINDOW = 512
PAGE = 128
ROPE_THETA = 10000.0
NORM_EPS = 1e-6
FORCE_BONUS = 1e4
KV_LANES = 2 * N_KV * HEAD_DIM
LANES = 128
SEL_CHUNK = 64
NEG_BIAS = -1e9
NEG_MASK = -1e30
Q_SCALE = HEAD_DIM ** -0.5 * 1.4426950408889634

C_CONV = 0
C_Q = 4 * CONV_CH
C_KV = C_Q + ATTN_W
C_Z = C_KV + N_BRANCH * KV_LANES
C_G = C_Z + ATTN_W
W_COLS = C_G + LANES

VMEM_LIMIT = 48 * 1024 * 1024
PAGES_PER_CHUNK = 32
CHUNK_SLOTS = 3
PAIR_UNROLL = 16
CHUNK_BLOCKS = 16
CHUNK_UNROLL = 4


def _dot(a, b):
    return jnp.dot(a, b, preferred_element_type=F32)


def _dot_nt(a, b):
    return lax.dot_general(a, b, (((1,), (1,)), ((), ())), preferred_element_type=F32)


def _iota(shape, dim):
    return lax.broadcasted_iota(jnp.int32, shape, dim)


def _masked_softmax(s, mask):
    s = jnp.where(mask, s, -jnp.inf)
    m = jnp.max(s, axis=-1, keepdims=True)
    m = jnp.where(m > -jnp.inf, m, 0.0)
    p = jnp.exp2(s - m)
    return p * (1.0 / jnp.maximum(jnp.sum(p, axis=-1, keepdims=True), 1e-30))


def _group_mean_sq(x):
    x2 = x * x
    hi = x2.astype(BF16)
    lo = (x2 - hi.astype(F32)).astype(BF16)
    same = ((_iota((2 * LANES, LANES), 0) >> 6) & 1) == (_iota((2 * LANES, LANES), 1) >> 6)
    ones = jnp.where(same, 1.0, 0.0).astype(BF16)
    return _dot(jnp.concatenate([hi, lo], axis=1), ones) * (1.0 / HEAD_DIM)


def _head_norm_rope(x, gain, cos2, sin2):
    xn = x * lax.rsqrt(_group_mean_sq(x) + NORM_EPS) * gain
    lane = _iota(x.shape, 1)
    swapped = jnp.where((lane & (HEAD_DIM - 1)) < HEAD_DIM // 2,
                        pltpu.roll(xn, LANES - HEAD_DIM // 2, 1), pltpu.roll(xn, HEAD_DIM // 2, 1))
    return xn * cos2 + swapped * sin2


def _low_half(x, other=0.0):
    lane = _iota(x.shape, 1)
    return jnp.where(lane < HEAD_DIM, x, other)


def _swap_halves(x):
    return pltpu.roll(x, HEAD_DIM, 1)


def _normed_input(x, norm_g):
    ms = jnp.mean(x * x, axis=-1, keepdims=True)
    return (x * lax.rsqrt(ms + NORM_EPS) * norm_g).astype(BF16)


def _silu(z):
    return z * jax.nn.sigmoid(z)


def _prompt_in_kernel(x_ref, ng_ref, w_ref, cw_ref, cb_ref, qg_ref, kg_ref, cos_ref, sin_ref,
                      convb_ref, sz_ref, q_ref, kvc_ref, kvs_ref, kvw_ref, ksa_ref, vsa_ref, kw_ref, vw_ref,
                      gate_ref, utail_ref, carry_ref, *, tm):
    ti = pl.program_id(1)
    h = _normed_input(x_ref[...], ng_ref[...])
    cos2 = cos_ref[...]
    sin2 = sin_ref[...]

    u = _dot(h, w_ref[:, CONV_CH:2 * CONV_CH]) * _dot(h, w_ref[:, 2 * CONV_CH:3 * CONV_CH])

    @pl.when(ti == 0)
    def _():
        carry_ref[...] = jnp.zeros_like(carry_ref)

    prev1 = carry_ref[7:8, :]
    prev2 = carry_ref[6:7, :]
    row = _iota(u.shape, 0)
    u1 = jnp.where(row == 0, prev1, pltpu.roll(u, 1, 0))
    u2 = jnp.where(row == 0, prev2, jnp.where(row == 1, prev1, pltpu.roll(u, 2, 0)))
    conv_y = cw_ref[0:1, :] * u2 + cw_ref[1:2, :] * u1 + cw_ref[2:3, :] * u + cb_ref[...]
    b_gate = _dot(h, w_ref[:, 0:CONV_CH])
    z_conv = _dot(h, w_ref[:, 3 * CONV_CH:4 * CONV_CH])
    convb_ref[...] = (b_gate * conv_y * _silu(z_conv)).astype(BF16)
    carry_ref[...] = u[tm - 8:tm, :]
    utail_ref[...] = u[tm - 8:tm, :]

    pq = _dot(h, w_ref[:, C_Q:C_KV])
    for c in range(N_HEADS // 2):
        qc = _head_norm_rope(pq[:, c * LANES:(c + 1) * LANES], qg_ref[...], cos2, sin2) * Q_SCALE
        for half in range(2):
            head = 2 * c + half
            src = qc if half == 0 else _swap_halves(qc)
            q_ref[head // GROUP, head % GROUP] = _low_half(src).astype(BF16)

    pk = _dot(h, w_ref[:, C_KV:C_Z])
    lane = _iota((tm, LANES), 1)
    tpos = ti * tm + _iota((tm, LANES), 0)
    onehot = jnp.where(((tpos >> 6) & (SEL_CHUNK - 1)) == lane - HEAD_DIM, 1.0, 0.0)
    ones_col = jnp.where(lane == HEAD_DIM, 1.0, 0.0)
    for br, out_ref in enumerate((kvc_ref, kvs_ref, kvw_ref)):
        kp = _head_norm_rope(pk[:, br * KV_LANES:br * KV_LANES + LANES], kg_ref[br:br + 1, :], cos2, sin2)
        vv = pk[:, br * KV_LANES + LANES:(br + 1) * KV_LANES]
        out_ref[:, 0:LANES] = kp
        out_ref[:, LANES:KV_LANES] = vv
        if br == 1:
            for g in range(N_KV):
                ksa_ref[g] = _low_half(kp if g == 0 else _swap_halves(kp), onehot).astype(BF16)
                vsa_ref[g] = _low_half(vv if g == 0 else _swap_halves(vv), ones_col).astype(BF16)
        if br == 2:
            for g in range(N_KV):
                kw_ref[g] = _low_half(kp if g == 0 else _swap_halves(kp)).astype(BF16)
                vw_ref[g] = _low_half(vv if g == 0 else _swap_halves(vv), ones_col).astype(BF16)

    sz_ref[...] = _silu(_dot(h, w_ref[:, C_Z:C_G]))
    sg = jax.nn.sigmoid(_dot(h, w_ref[:, C_G:W_COLS]))
    gate_ref[0] = sg
    gate_ref[1] = pltpu.roll(sg, LANES - GROUP * N_BRANCH, 1)


def _prompt_in(x, ng, w_all, cw, cb, qg, kg, cos2, sin2, tm):
    b, t, _ = x.shape
    grid = (b, t // tm)
    row_blk = lambda w: pl.BlockSpec((None, tm, w), lambda bi, ti: (bi, ti, 0))
    full = lambda a: pl.BlockSpec(a.shape, lambda bi, ti: (0,) * a.ndim)
    head_blk = pl.BlockSpec((None, N_KV, tm, LANES), lambda bi, ti: (bi, 0, ti, 0))
    out_shape = (
        jax.ShapeDtypeStruct((b, t, CONV_CH), BF16),
        jax.ShapeDtypeStruct((b, t, ATTN_W), F32),
        jax.ShapeDtypeStruct((b, N_KV, GROUP, t, LANES), BF16),
        jax.ShapeDtypeStruct((b, t, KV_LANES), F32),
        jax.ShapeDtypeStruct((b, t, KV_LANES), F32),
        jax.ShapeDtypeStruct((b, t, KV_LANES), F32),
        jax.ShapeDtypeStruct((b, N_KV, t, LANES), BF16),
        jax.ShapeDtypeStruct((b, N_KV, t, LANES), BF16),
        jax.ShapeDtypeStruct((b, N_KV, t, LANES), BF16),
        jax.ShapeDtypeStruct((b, N_KV, t, LANES), BF16),
        jax.ShapeDtypeStruct((b, N_KV, t, LANES), F32),
        jax.ShapeDtypeStruct((b, 8, CONV_CH), F32),
    )
    out_specs = (
        row_blk(CONV_CH), row_blk(ATTN_W),
        pl.BlockSpec((None, N_KV, GROUP, tm, LANES), lambda bi, ti: (bi, 0, 0, ti, 0)),
        row_blk(KV_LANES), row_blk(KV_LANES), row_blk(KV_LANES),
        head_blk, head_blk, head_blk, head_blk, head_blk,
        pl.BlockSpec((None, 8, CONV_CH), lambda bi, ti: (bi, 0, 0)),
    )
    tab = pl.BlockSpec((tm, LANES), lambda bi, ti: (ti, 0))
    return pl.pallas_call(
        functools.partial(_prompt_in_kernel, tm=tm),
        grid=grid,
        in_specs=[row_blk(D_MODEL), full(ng), full(w_all), full(cw), full(cb), full(qg), full(kg), tab, tab],
        out_specs=out_specs,
        out_shape=out_shape,
        scratch_shapes=[pltpu.VMEM((8, CONV_CH), F32)],
        compiler_params=pltpu.CompilerParams(dimension_semantics=("arbitrary", "arbitrary"),
                                             vmem_limit_bytes=VMEM_LIMIT),
        name="prompt_in_proj",
    )(x, ng, w_all, cw, cb, qg, kg, cos2, sin2)


def _prompt_compress_kernel(x_ref, pe_ref, perm_ref, wbd_ref, kc_ref, vc_ref, rows_ref, acc_ref, *, n_even):
    half = CHUNK_BLOCKS // 2
    chunk_rows = CHUNK_BLOCKS * CMP_BLK
    n_chunks = 2 * n_even // CHUNK_BLOCKS

    def chunk_body(it, carry):
        for k in range(CHUNK_UNROLL):
            ch = it * CHUNK_UNROLL + k
            x = x_ref[pl.ds(pl.multiple_of(ch * chunk_rows, chunk_rows), chunk_rows), :] + pe_ref[...]
            x_perm = _dot(perm_ref[...], x.astype(BF16))
            base = pl.multiple_of(ch * half, half)
            for r in range(CMP_BLK):
                rows_ref[r, pl.ds(base, half), :] = x_perm[r * CHUNK_BLOCKS:r * CHUNK_BLOCKS + half, :]
                rows_ref[r, pl.ds(n_even + base, half), :] = x_perm[r * CHUNK_BLOCKS + half:(r + 1) * CHUNK_BLOCKS, :]
        return carry

    lax.fori_loop(0, n_chunks // CHUNK_UNROLL, chunk_body, 0)
    acc_ref[...] = jnp.zeros_like(acc_ref)

    def compress_body(it, carry):
        acc = acc_ref[...]
        for k in range(CHUNK_UNROLL):
            r = it * CHUNK_UNROLL + k
            acc = acc + _dot(rows_ref[r].astype(BF16), wbd_ref[r])
        acc_ref[...] = acc
        return carry

    lax.fori_loop(0, CMP_BLK // CHUNK_UNROLL, compress_body, 0)
    kk = acc_ref[:, 0:LANES]
    vv = acc_ref[:, LANES:KV_LANES]
    for g in range(N_KV):
        kc_ref[g] = _low_half(kk if g == 0 else _swap_halves(kk)).astype(BF16)
        vc_ref[g] = _low_half(vv if g == 0 else _swap_halves(vv)).astype(BF16)


def _prompt_compress(kvc, pe_chunk, perm_chunk, wbd):
    b, t, _ = kvc.shape
    nc = t // CMP_BLK
    assert nc % (CHUNK_BLOCKS * CHUNK_UNROLL) == 0 and CMP_BLK % CHUNK_UNROLL == 0
    out = jax.ShapeDtypeStruct((b, N_KV, nc, LANES), BF16)
    blk = pl.BlockSpec((None, N_KV, nc, LANES), lambda bi: (bi, 0, 0, 0))
    const = lambda a: pl.BlockSpec(a.shape, lambda bi: (0,) * a.ndim)
    return pl.pallas_call(
        functools.partial(_prompt_compress_kernel, n_even=nc // 2),
        grid=(b,),
        in_specs=[pl.BlockSpec((None, t, KV_LANES), lambda bi: (bi, 0, 0)),
                  const(pe_chunk), const(perm_chunk), const(wbd)],
        out_specs=(blk, blk),
        out_shape=(out, out),
        scratch_shapes=[pltpu.VMEM((CMP_BLK, nc, KV_LANES), F32), pltpu.VMEM((nc, KV_LANES), F32)],
        compiler_params=pltpu.CompilerParams(dimension_semantics=("arbitrary",), vmem_limit_bytes=VMEM_LIMIT),
        name="prompt_compress",
    )(kvc, pe_chunk, perm_chunk, wbd)


def _select_blocks(score, blk_f, n_rounds):
    sel = jnp.zeros(score.shape, F32)
    picks = []
    for _ in range(n_rounds):
        m = jnp.max(score, axis=-1, keepdims=True)
        first = jnp.min(jnp.where(score == m, blk_f, 1e9), axis=-1, keepdims=True)
        hit = blk_f == first
        sel = jnp.where(hit, 1.0, sel)
        score = jnp.where(hit, -jnp.inf, score)
        picks.append(first)
    return sel, picks


def _selection_bias(imp, qs):
    tq, ns = imp.shape
    n_chunks = tq // LANES
    imp_t = jnp.concatenate([imp[r * LANES:(r + 1) * LANES, :].T for r in range(n_chunks)], axis=1)
    blk = _iota((ns, tq), 0)
    qp = qs + _iota((1, tq), 1)
    qb = qp >> 6
    valid = blk * SEL_BLK <= qp
    forced = ((blk == 0) | (blk == qb) | (blk == qb - 1)) & valid
    blk_f = blk.astype(F32)
    candidate = valid & jnp.logical_not(forced)
    score = jnp.where(candidate, imp_t, -jnp.inf)
    for _ in range(min(N_SEL, ns) - 3):
        m = jnp.max(score, axis=0, keepdims=True)
        first = jnp.min(jnp.where(score == m, blk_f, 1e9), axis=0, keepdims=True)
        score = jnp.where(blk_f == first, -jnp.inf, score)
    bias_t = jnp.where(forced | (candidate & (score == -jnp.inf)), 0.0, NEG_BIAS)
    return jnp.concatenate([bias_t[:, r * LANES:(r + 1) * LANES].T for r in range(n_chunks)], axis=0)


def _prepare_query_tile(q_ref, kc_ref, vc_ref, kw_ref, vw_ref, qa_ref, ocmp_ref, owin_ref, tile, dst, *, tq, t):
    qs = tile * tq
    rows = GROUP * tq
    nc = t // CMP_BLK
    ns = nc // 2
    n_variants = ns // SEL_CHUNK
    q = q_ref[...].reshape(rows, LANES)
    qpos = qs + (_iota((rows, 1), 0) & (tq - 1))

    col = _iota((1, nc), 1)
    cblk = jnp.where(col < ns, 2 * col, 2 * (col - ns) + 1)
    s_cmp = jnp.where((cblk + 1) * CMP_BLK - 1 <= qpos, _dot_nt(q, kc_ref[...]), -jnp.inf)
    m_cmp = jnp.max(s_cmp, axis=-1, keepdims=True)
    e_cmp = jnp.exp2(s_cmp - jnp.where(m_cmp > -jnp.inf, m_cmp, 0.0))
    inv_cmp = 1.0 / jnp.maximum(jnp.sum(e_cmp, axis=-1, keepdims=True), 1e-30)
    ocmp_ref[dst] = _dot(e_cmp.astype(BF16), vc_ref[...]) * inv_cmp

    pair = (e_cmp[:, 0:ns] + e_cmp[:, ns:nc]) * inv_cmp
    imp = pair[0:tq]
    for n in range(1, GROUP):
        imp = imp + pair[n * tq:(n + 1) * tq]
    bias = _selection_bias(imp, qs)

    for c in range(ns // SEL_CHUNK):
        bc = bias[:, (c // 2) * LANES:(c // 2 + 1) * LANES]
        if c % 2 == 0:
            bc = _swap_halves(bc)
        bc = jnp.concatenate([bc] * GROUP, axis=0).astype(BF16)
        qa_ref[dst * n_variants + c] = _low_half(q, bc)

    q_win = jnp.where(_iota((rows, LANES), 1) == HEAD_DIM, NEG_BIAS, q.astype(F32)).astype(BF16)
    r0 = _iota((GROUP * LANES, 1), 0) & (LANES - 1)
    upper = _iota((1, LANES), 1) > r0
    n_mid = WINDOW // LANES - 1
    o_win_blocks = []
    for a in range(tq // LANES):
        q_a = jnp.concatenate([q_win[n * tq + a * LANES:n * tq + (a + 1) * LANES] for n in range(GROUP)], axis=0)
        base = pl.multiple_of(qs + a * LANES, LANES)
        s_a = _dot_nt(q_a, kw_ref[pl.ds(base, WINDOW + LANES), :])
        s_a = jnp.concatenate([jnp.where(upper, s_a[:, 0:LANES], NEG_MASK), s_a[:, LANES:(n_mid + 1) * LANES],
                               jnp.where(upper, NEG_MASK, s_a[:, (n_mid + 1) * LANES:])], axis=1)
        e_a = jnp.exp2(s_a - jnp.max(s_a, axis=-1, keepdims=True))
        acc_a = _dot(e_a.astype(BF16), vw_ref[pl.ds(base, WINDOW + LANES), :])
        o_win_blocks.append(_low_half(acc_a * (1.0 / acc_a[:, HEAD_DIM:HEAD_DIM + 1])))
    owin_ref[dst] = jnp.concatenate(
        [blk_a[n * LANES:(n + 1) * LANES] for n in range(GROUP) for blk_a in o_win_blocks], axis=0)


def _prompt_attn_kernel(q_ref, kc_ref, vc_ref, ksa_ref, vsa_ref, kw_ref, vw_ref, gate_ref, sz_ref, out_ref,
                        qa_ref, ocmp_ref, owin_ref, sa_ref, sb_ref, m_ref, acc_ref, *, tq, tk, t):
    i = pl.program_id(2)
    qs = i * tq
    rows = GROUP * tq
    n_variants = t // (SEL_CHUNK * SEL_BLK)
    cur = 0
    _prepare_query_tile(q_ref, kc_ref, vc_ref, kw_ref, vw_ref, qa_ref, ocmp_ref, owin_ref, i, cur, tq=tq, t=t)
    qpos = qs + (_iota((rows, 1), 0) & (tq - 1))

    m_ref[...] = jnp.full_like(m_ref, NEG_MASK)
    acc_ref[...] = jnp.zeros_like(acc_ref)

    def scores(kt):
        k0 = pl.multiple_of(kt * tk, tk)
        return _dot_nt(qa_ref[cur * n_variants + kt // (SEL_CHUNK * SEL_BLK // tk)], ksa_ref[pl.ds(k0, tk), :])

    def update(s, kt, causal):
        k0 = pl.multiple_of(kt * tk, tk)
        if causal:
            s = jnp.where(k0 + _iota((1, tk), 1) <= qpos, s, NEG_MASK)
        m_old = m_ref[...]
        m_new = jnp.maximum(m_old, jnp.broadcast_to(jnp.max(s, axis=-1, keepdims=True), m_old.shape))
        p = jnp.exp2(s - jnp.concatenate([m_new] * (tk // LANES), axis=1))
        acc_ref[...] = jnp.exp2(m_old - m_new) * acc_ref[...] + _dot(p.astype(BF16), vsa_ref[pl.ds(k0, tk), :])
        m_ref[...] = m_new

    n_kt = (qs + tq + tk - 1) // tk
    n_pairs = (n_kt - 1) // 2
    sa_ref[...] = scores(0)

    def pair_step(j, carry):
        sb_ref[...] = scores(2 * j + 1)
        update(sa_ref[...], 2 * j, False)
        sa_ref[...] = scores(2 * j + 2)
        update(sb_ref[...], 2 * j + 1, False)
        return carry

    lax.fori_loop(0, n_pairs, pair_step, 0)
    last = n_kt - 1

    @pl.when(n_kt - 2 * n_pairs == 2)
    def _():
        update(sa_ref[...], last - 1, False)
        sa_ref[...] = scores(last)

    o_cmp = ocmp_ref[cur]
    o_win = owin_ref[cur]
    update(sa_ref[...], last, True)

    acc = acc_ref[...]
    o_slc = _low_half(acc * (1.0 / acc[:, HEAD_DIM:HEAD_DIM + 1]))

    gate = gate_ref[...]
    mixed = []
    for n in range(GROUP):
        r = slice(n * tq, (n + 1) * tq)
        mixed.append(gate[:, 3 * n:3 * n + 1] * o_cmp[r] + gate[:, 3 * n + 1:3 * n + 2] * o_slc[r]
                     + gate[:, 3 * n + 2:3 * n + 3] * o_win[r])
    for c in range(GROUP // 2):
        o_pair = mixed[2 * c] + _swap_halves(mixed[2 * c + 1])
        out_ref[:, c * LANES:(c + 1) * LANES] = (o_pair * sz_ref[:, c * LANES:(c + 1) * LANES]).astype(BF16)


def _prompt_attn(q, kc, vc, ksa, vsa, kw, vw, gate, sz, tq, tk):
    b, _, _, t, _ = q.shape
    assert t % (2 * SEL_CHUNK * SEL_BLK) == 0 and t >= WINDOW + tq and (SEL_CHUNK * SEL_BLK) % tk == 0
    assert tq % LANES == 0 and N_SEL > 3
    nc = t // CMP_BLK
    rows = GROUP * tq
    seq = lambda n: pl.BlockSpec((None, None, n, LANES), lambda bi, g, i: (bi, g, 0, 0))
    return pl.pallas_call(
        functools.partial(_prompt_attn_kernel, tq=tq, tk=tk, t=t),
        grid=(b, N_KV, t // tq),
        in_specs=[pl.BlockSpec((None, None, GROUP, tq, LANES), lambda bi, g, i: (bi, g, 0, i, 0)),
                  seq(nc), seq(nc), seq(t), seq(t), seq(t + WINDOW), seq(t + WINDOW),
                  pl.BlockSpec((None, None, tq, LANES), lambda bi, g, i: (bi, g, i, 0)),
                  pl.BlockSpec((None, tq, GROUP * HEAD_DIM), lambda bi, g, i: (bi, i, g))],
        out_specs=pl.BlockSpec((None, tq, GROUP * HEAD_DIM), lambda bi, g, i: (bi, i, g)),
        out_shape=jax.ShapeDtypeStruct((b, t, ATTN_W), BF16),
        scratch_shapes=[pltpu.VMEM((t // (SEL_CHUNK * SEL_BLK), rows, LANES), BF16),
                        pltpu.VMEM((1, rows, LANES), F32), pltpu.VMEM((1, rows, LANES), F32),
                        pltpu.VMEM((rows, tk), F32), pltpu.VMEM((rows, tk), F32),
                        pltpu.VMEM((rows, LANES), F32), pltpu.VMEM((rows, LANES), F32)],
        compiler_params=pltpu.CompilerParams(dimension_semantics=("arbitrary", "arbitrary", "arbitrary"),
                                             vmem_limit_bytes=VMEM_LIMIT),
        name="prompt_attention",
    )(q, kc, vc, ksa, vsa, kw, vw, gate, sz)


def _out_proj_kernel(x_ref, convb_ref, attnb_ref, w_ref, y_ref):
    mix = jnp.concatenate([convb_ref[...], attnb_ref[...]], axis=1)
    y_ref[...] = x_ref[...] + _dot(mix, w_ref[...])


def _out_proj(x2d, convb, attnb, w_out, tm):
    m = x2d.shape[0]
    blk = lambda w: pl.BlockSpec((tm, w), lambda i: (i, 0))
    return pl.pallas_call(
        _out_proj_kernel,
        grid=(m // tm,),
        in_specs=[blk(D_MODEL), blk(CONV_CH), blk(ATTN_W), pl.BlockSpec(w_out.shape, lambda i: (0, 0))],
        out_specs=blk(D_MODEL),
        out_shape=jax.ShapeDtypeStruct((m, D_MODEL), F32),
        compiler_params=pltpu.CompilerParams(dimension_semantics=("arbitrary",), vmem_limit_bytes=VMEM_LIMIT),
        name="out_proj",
    )(x2d, convb, attnb, w_out)


def _sample_in_kernel(x_ref, ng_ref, w_ref, cw_ref, cb_ref, qg_ref, kg_ref, cos_ref, sin_ref, c0_ref, c1_ref,
                      convb_ref, sz_ref, qbd_ref, kvn_ref, gate_ref, u_ref):
    h = _normed_input(x_ref[...], ng_ref[...])
    cos2 = cos_ref[...]
    sin2 = sin_ref[...]
    pc = _dot(h, w_ref[:, C_CONV:C_Q])
    b_gate = pc[:, 0:CONV_CH]
    u = pc[:, CONV_CH:2 * CONV_CH] * pc[:, 2 * CONV_CH:3 * CONV_CH]
    z_conv = pc[:, 3 * CONV_CH:4 * CONV_CH]
    conv_y = cw_ref[0:1, :] * c0_ref[...] + cw_ref[1:2, :] * c1_ref[...] + cw_ref[2:3, :] * u + cb_ref[...]
    convb_ref[...] = (b_gate * conv_y * _silu(z_conv)).astype(BF16)
    u_ref[...] = u

    pq = _dot(h, w_ref[:, C_Q:C_KV])
    zeros = jnp.zeros((x_ref.shape[0], LANES), BF16)
    for c in range(N_HEADS // 2):
        qc = _head_norm_rope(pq[:, c * LANES:(c + 1) * LANES], qg_ref[...], cos2, sin2) * Q_SCALE
        lane = _iota(qc.shape, 1)
        for half in range(2):
            head = 2 * c + half
            g = head // GROUP
            src = qc if half == g else _swap_halves(qc)
            keep = (lane >= g * HEAD_DIM) & (lane < (g + 1) * HEAD_DIM)
            qbd_ref[head] = jnp.concatenate([jnp.where(keep, src, 0.0).astype(BF16), zeros], axis=1)

    pk = _dot(h, w_ref[:, C_KV:C_Z])
    for br in range(N_BRANCH):
        kvn_ref[:, br * KV_LANES:br * KV_LANES + LANES] = _head_norm_rope(
            pk[:, br * KV_LANES:br * KV_LANES + LANES], kg_ref[br:br + 1, :], cos2, sin2)
        kvn_ref[:, br * KV_LANES + LANES:(br + 1) * KV_LANES] = pk[:, br * KV_LANES + LANES:(br + 1) * KV_LANES]

    sz_ref[...] = _silu(_dot(h, w_ref[:, C_Z:C_G]))
    gate_ref[...] = jax.nn.sigmoid(_dot(h, w_ref[:, C_G:W_COLS]))


def _sample_in(x, ng, w_all, cw, cb, qg, kg, cos1, sin1, c0, c1):
    nb = x.shape[0]
    out_shape = (
        jax.ShapeDtypeStruct((nb, CONV_CH), BF16),
        jax.ShapeDtypeStruct((nb, ATTN_W), F32),
        jax.ShapeDtypeStruct((N_HEADS, nb, 2 * LANES), BF16),
        jax.ShapeDtypeStruct((nb, N_BRANCH * KV_LANES), F32),
        jax.ShapeDtypeStruct((nb, LANES), F32),
        jax.ShapeDtypeStruct((nb, CONV_CH), F32),
    )
    return pl.pallas_call(
        _sample_in_kernel,
        out_shape=out_shape,
        compiler_params=pltpu.CompilerParams(vmem_limit_bytes=VMEM_LIMIT),
        name="sample_in_proj",
    )(x, ng, w_all, cw, cb, qg, kg, cos1, sin1, c0, c1)


def _sample_cmp_kernel(pt_ref, qbd_ref, cache_ref, pet_ref, perm_ref, wbd_ref, ocmp_ref, imp_ref,
                       buf_ref, rows_ref, acc_ref, sem, *, n_batch, n_pages, past):
    b = pl.program_id(0)
    n_chunks = n_pages // PAGES_PER_CHUNK
    blocks_per_pair = 2 * PAGE // CMP_BLK

    def page_copy(bb, ch, p, slot):
        return pltpu.make_async_copy(cache_ref.at[pt_ref[bb, ch * PAGES_PER_CHUNK + p]], buf_ref.at[slot, p],
                                     sem.at[slot])

    def start_chunk(bb, ch, slot):
        def body(p2, carry):
            for priority in range(2):
                page_copy(bb, ch, 2 * p2 + priority, slot).start(priority=priority)
            return carry
        lax.fori_loop(0, PAGES_PER_CHUNK // 2, body, 0)

    def wait_chunk(ch, slot):
        def body(p, carry):
            page_copy(b, ch, p, slot).wait()
            return carry
        lax.fori_loop(0, PAGES_PER_CHUNK, body, 0)

    def stream_slot(offset):
        return (b * n_chunks + offset) % CHUNK_SLOTS

    def start_ahead(offset):
        bb_off, ch = divmod(offset, n_chunks)
        if bb_off == 0:
            start_chunk(b, ch, stream_slot(offset))
        else:
            @pl.when(b + bb_off < n_batch)
            def _():
                start_chunk(b + bb_off, ch, stream_slot(offset))

    @pl.when(b == 0)
    def _():
        for offset in range(CHUNK_SLOTS - 1):
            start_ahead(offset)

    for ch in range(n_chunks):
        slot = stream_slot(ch)
        start_ahead(ch + CHUNK_SLOTS - 1)
        wait_chunk(ch, slot)

        def pairs_body(it, carry):
            for k in range(PAIR_UNROLL):
                pr = it * PAIR_UNROLL + k
                xt = jnp.concatenate([buf_ref[slot, 2 * pr], buf_ref[slot, 2 * pr + 1]], axis=1) + pet_ref[...]
                x_perm = _dot_nt(perm_ref[...], xt.astype(BF16))
                base = pl.multiple_of((ch * (PAGES_PER_CHUNK // 2) + pr) * blocks_per_pair, blocks_per_pair)
                for r in range(CMP_BLK):
                    rows_ref[r, pl.ds(base, blocks_per_pair), :] = (
                        x_perm[r * blocks_per_pair:(r + 1) * blocks_per_pair, :])
            return carry

        lax.fori_loop(0, PAGES_PER_CHUNK // 2 // PAIR_UNROLL, pairs_body, 0)

    acc_ref[...] = jnp.zeros_like(acc_ref)

    def compress_body(it, carry):
        acc = acc_ref[...]
        for k in range(PAIR_UNROLL):
            r = it * PAIR_UNROLL + k
            acc = acc + _dot(rows_ref[r].astype(BF16), wbd_ref[r])
        acc_ref[...] = acc
        return carry

    lax.fori_loop(0, CMP_BLK // PAIR_UNROLL, compress_body, 0)

    nc = past // CMP_BLK
    kv = acc_ref[...].astype(BF16)
    qbd = qbd_ref[...]
    p_cmp = _masked_softmax(_dot_nt(qbd, kv), (_iota((1, nc), 1) + 1) * CMP_BLK - 1 <= past)
    ocmp_ref[...] = _dot(p_cmp.astype(BF16), kv)

    pair = p_cmp + pltpu.roll(p_cmp, nc - 1, 1)
    row = _iota((N_HEADS, nc), 0)
    imp_ref[...] = jnp.where(row == 0, jnp.sum(jnp.where(row < GROUP, pair, 0.0), axis=0, keepdims=True),
                             jnp.sum(jnp.where(row >= GROUP, pair, 0.0), axis=0, keepdims=True))


def _sample_cmp(page_table, qbd, cache_t, pe_t, perm, wbd, past):
    nb, n_pages = page_table.shape
    assert n_pages % PAGES_PER_CHUNK == 0 and n_pages // PAGES_PER_CHUNK >= CHUNK_SLOTS - 1
    nc = past // CMP_BLK
    const = lambda a: pl.BlockSpec(a.shape, lambda b, pt: (0,) * a.ndim)
    grid_spec = pltpu.PrefetchScalarGridSpec(
        num_scalar_prefetch=1,
        grid=(nb,),
        in_specs=[pl.BlockSpec((None, N_HEADS, 2 * LANES), lambda b, pt: (b, 0, 0)),
                  pl.BlockSpec(memory_space=pl.ANY), const(pe_t), const(perm), const(wbd)],
        out_specs=(pl.BlockSpec((None, N_HEADS, KV_LANES), lambda b, pt: (b, 0, 0)),
                   pl.BlockSpec((None, N_HEADS, nc), lambda b, pt: (b, 0, 0))),
        scratch_shapes=[pltpu.VMEM((CHUNK_SLOTS, PAGES_PER_CHUNK, KV_LANES, PAGE), F32),
                        pltpu.VMEM((CMP_BLK, nc, KV_LANES), F32),
                        pltpu.VMEM((nc, KV_LANES), F32), pltpu.SemaphoreType.DMA((CHUNK_SLOTS,))],
    )
    return pl.pallas_call(
        functools.partial(_sample_cmp_kernel, n_batch=nb, n_pages=n_pages, past=past),
        grid_spec=grid_spec,
        out_shape=(jax.ShapeDtypeStruct((nb, N_HEADS, KV_LANES), F32),
                   jax.ShapeDtypeStruct((nb, N_HEADS, nc), F32)),
        compiler_params=pltpu.CompilerParams(dimension_semantics=("arbitrary",), vmem_limit_bytes=VMEM_LIMIT),
        name="sample_compress",
    )(page_table, qbd, cache_t, pe_t, perm, wbd)


def _sample_select_kernel(imp_ref, idx_ref, *, past):
    imp = imp_ref[...]
    lane = _iota(imp.shape, 1)
    blk = lane >> 1
    qb = past // SEL_BLK
    forced = (blk == 0) | (blk == qb) | (blk == qb - 1)
    valid = ((lane & 1) == 0) & (blk * SEL_BLK <= past)
    score = jnp.where(valid, imp + jnp.where(forced, FORCE_BONUS, 0.0), -jnp.inf)
    _, picks = _select_blocks(score, lane.astype(F32), N_SEL - 1)
    out_lane = _iota(idx_ref.shape, 1)
    idx = jnp.zeros(idx_ref.shape, F32)
    for k, pick in enumerate(picks):
        idx = jnp.where(out_lane == k, pick * 0.5, idx)
    idx_ref[...] = idx.astype(jnp.int32)


def _sample_select(imp, past):
    return pl.pallas_call(
        functools.partial(_sample_select_kernel, past=past),
        out_shape=jax.ShapeDtypeStruct((imp.shape[0], LANES), jnp.int32),
        compiler_params=pltpu.CompilerParams(vmem_limit_bytes=VMEM_LIMIT),
        name="sample_select",
    )(imp)


def _sample_attn_kernel(pt_ref, sel_ref, qbd_ref, cache_ref, kvn_ref, win_ref, ocmp_ref, gate_ref, out_ref,
                        buf_ref, sem, *, n_batch, past):
    b = pl.program_id(0)
    n_pick = N_SEL - 1
    blocks_per_page = PAGE // SEL_BLK
    slot = b % 2

    def page_copy(bb, j, s):
        return pltpu.make_async_copy(cache_ref.at[pt_ref[bb, sel_ref[bb, j] >> 1]], buf_ref.at[s, j], sem.at[s])

    @pl.when(b == 0)
    def _():
        for j in range(N_KV * n_pick):
            page_copy(0, j, 0).start(priority=j % 2)

    @pl.when(b + 1 < n_batch)
    def _():
        for j in range(N_KV * n_pick):
            page_copy(b + 1, j, 1 - slot).start(priority=j % 2)

    qbd = qbd_ref[...]
    qf = qbd.astype(F32)
    row = _iota((N_HEADS, 1), 0)

    def new_token(offset):
        kv_new = kvn_ref[:, offset:offset + KV_LANES].astype(BF16).astype(F32)
        return jnp.sum(qf * kv_new, axis=-1, keepdims=True), kv_new

    def attend(keys_t, mask, s_new, kv_new):
        s = jnp.where(mask, _dot(qbd, keys_t), -jnp.inf)
        m = jnp.maximum(jnp.max(s, axis=-1, keepdims=True), s_new)
        p = jnp.exp2(s - m)
        p_new = jnp.exp2(s_new - m)
        norm = 1.0 / (jnp.sum(p, axis=-1, keepdims=True) + p_new)
        return (_dot_nt(p.astype(BF16), keys_t) + p_new.astype(BF16).astype(F32) * kv_new) * norm

    w_buf = win_ref.shape[1]
    s_new, kv_new = new_token(2 * KV_LANES)
    x_win = attend(win_ref[...].astype(BF16), _iota((1, w_buf), 1) > w_buf - WINDOW, s_new, kv_new)

    for j in range(N_KV * n_pick):
        page_copy(b, j, slot).wait()

    s_new, kv_new = new_token(KV_LANES)
    lane = _iota((1, n_pick * PAGE), 1)
    x_slc = []
    for g in range(N_KV):
        slots = range(g * n_pick, (g + 1) * n_pick)
        keys_t = jnp.concatenate([buf_ref[slot, j] for j in slots], axis=1).astype(BF16)
        half = jnp.concatenate([jnp.full((1, PAGE), sel_ref[b, j] & (blocks_per_page - 1), jnp.int32) for j in slots],
                               axis=1)
        x_slc.append(attend(keys_t, ((lane & (PAGE - 1)) >> (SEL_BLK.bit_length() - 1)) == half, s_new, kv_new))
    x_slc = jnp.where(row < GROUP, x_slc[0], x_slc[1])

    gate = jnp.broadcast_to(gate_ref[...], (N_HEADS, LANES))
    lane = _iota((N_HEADS, LANES), 1)
    mixed = jnp.zeros((N_HEADS, KV_LANES), F32)
    for br, x_br in enumerate((ocmp_ref[...], x_slc, x_win)):
        g_col = jnp.sum(jnp.where(lane == N_BRANCH * _iota((N_HEADS, LANES), 0) + br, gate, 0.0), axis=-1, keepdims=True)
        mixed = mixed + g_col * x_br
    v_lo = mixed[:, 2 * HEAD_DIM:3 * HEAD_DIM]
    v_hi = mixed[:, 3 * HEAD_DIM:4 * HEAD_DIM]
    out_ref[...] = jnp.where(row < GROUP, v_lo, v_hi)


def _sample_attn(page_table, sel, qbd, cache, kvn, win, ocmp, gate, past):
    nb = page_table.shape[0]
    n_pick = N_SEL - 1
    per_b = lambda *shape: pl.BlockSpec((None,) + shape, lambda b, pt, s: (b,) + (0,) * len(shape))
    grid_spec = pltpu.PrefetchScalarGridSpec(
        num_scalar_prefetch=2,
        grid=(nb,),
        in_specs=[per_b(N_HEADS, 2 * LANES),
                  pl.BlockSpec(memory_space=pl.ANY),
                  per_b(1, N_BRANCH * KV_LANES),
                  per_b(KV_LANES, win.shape[2]),
                  per_b(N_HEADS, KV_LANES),
                  per_b(1, LANES)],
        out_specs=per_b(N_HEADS, HEAD_DIM),
        scratch_shapes=[pltpu.VMEM((2, N_KV * n_pick, KV_LANES, PAGE), F32), pltpu.SemaphoreType.DMA((2,))],
    )
    return pl.pallas_call(
        functools.partial(_sample_attn_kernel, n_batch=nb, past=past),
        grid_spec=grid_spec,
        out_shape=jax.ShapeDtypeStruct((nb, N_HEADS, HEAD_DIM), F32),
        compiler_params=pltpu.CompilerParams(dimension_semantics=("arbitrary",), vmem_limit_bytes=VMEM_LIMIT),
        name="sample_attention",
    )(page_table, sel, qbd, cache, kvn, win, ocmp, gate)


def _sample_out_kernel(x_ref, convb_ref, attn_ref, sz_ref, w_ref, y_ref):
    mix = jnp.concatenate([convb_ref[...], (attn_ref[...] * sz_ref[...]).astype(BF16)], axis=1)
    y_ref[...] = x_ref[...] + _dot(mix, w_ref[...])


def _sample_out(x, convb, attn, sz, w_out):
    return pl.pallas_call(
        _sample_out_kernel,
        out_shape=jax.ShapeDtypeStruct(x.shape, F32),
        compiler_params=pltpu.CompilerParams(vmem_limit_bytes=VMEM_LIMIT),
        name="sample_out_proj",
    )(x, convb, attn, sz, w_out)


def _rope_tables(pos):
    half = HEAD_DIM // 2
    inv = ROPE_THETA ** (-jnp.arange(half, dtype=F32) / half)
    ang = pos.astype(F32)[:, None] * inv[None, :]
    cos, sin = lax.optimization_barrier((jnp.cos(ang), jnp.sin(ang)))
    return jnp.concatenate([cos, cos, cos, cos], axis=1), jnp.concatenate([-sin, sin, -sin, sin], axis=1)


def _layer_params(norm_g, w_in, conv_w, conv_b, q_gain, k_gain, cmp_pe, cmp_w, w_out):
    c_gates = C_KV + N_BRANCH * KV_LANES
    n_gates = N_HEADS * N_BRANCH
    w_all = jnp.concatenate([w_in[:, :c_gates], w_in[:, c_gates + n_gates:], w_in[:, c_gates:c_gates + n_gates],
                             jnp.zeros((D_MODEL, LANES - n_gates), w_in.dtype)], axis=1).astype(BF16)
    w16 = cmp_w.astype(BF16)
    wbd = jnp.concatenate(
        [jnp.pad(w16[:, j], ((0, 0), (0, 0), (o, KV_LANES - HEAD_DIM - o)))
         for j in range(2) for o in ((j * N_KV + g) * HEAD_DIM for g in range(N_KV))], axis=1)
    pe_row = jnp.broadcast_to(cmp_pe[:, :, None, :], (CMP_BLK, 2, N_KV, HEAD_DIM)).reshape(CMP_BLK, KV_LANES)
    blocks_per_pair = 2 * PAGE // CMP_BLK
    pe_t = jnp.tile(pe_row.T, (1, blocks_per_pair))
    m = jnp.arange(2 * PAGE)
    perm = (m[None, :] == (CMP_BLK * (m % blocks_per_pair) + m // blocks_per_pair)[:, None]).astype(BF16)
    pe_chunk = jnp.tile(pe_row, (CHUNK_BLOCKS, 1))
    mc = jnp.arange(CHUNK_BLOCKS * CMP_BLK)
    e = mc % CHUNK_BLOCKS
    blk_of_e = jnp.where(e < CHUNK_BLOCKS // 2, 2 * e, 2 * (e - CHUNK_BLOCKS // 2) + 1)
    perm_chunk = (mc[None, :] == (CMP_BLK * blk_of_e + mc // CHUNK_BLOCKS)[:, None]).astype(BF16)
    return dict(
        ng=norm_g.reshape(1, D_MODEL), w_all=w_all, cw=conv_w, cb=conv_b.reshape(1, CONV_CH),
        qg=jnp.tile(q_gain, 2).reshape(1, LANES), kg=jnp.tile(k_gain, (1, 2)),
        pe_chunk=pe_chunk, perm_chunk=perm_chunk, pe_t=pe_t, perm=perm, wbd=wbd, w_out=w_out.astype(BF16))


def _kv_rows(a):
    return a.reshape(a.shape[:-1] + (2, N_KV, HEAD_DIM))


def _prompt_layer(x, p):
    b, t, _ = x.shape
    cos2, sin2 = _rope_tables(jnp.arange(t, dtype=jnp.int32))
    (convb, sz, q, kvc, kvs, kvw, ksa, vsa, kw, vw, gate, utail) = _prompt_in(
        x, p["ng"], p["w_all"], p["cw"], p["cb"], p["qg"], p["kg"], cos2, sin2, tm=512)
    kc, vc = _prompt_compress(kvc, p["pe_chunk"], p["perm_chunk"], p["wbd"])
    pad_k = jnp.zeros((b, N_KV, WINDOW, LANES), BF16).at[..., HEAD_DIM].set(1.0)
    kw = jnp.concatenate([pad_k, kw], axis=2)
    vw = jnp.concatenate([jnp.zeros((b, N_KV, WINDOW, LANES), BF16), vw], axis=2)
    attnb = _prompt_attn(q, kc, vc, ksa, vsa, kw, vw, gate, sz, tq=256, tk=1024)
    y = _out_proj(x.reshape(b * t, D_MODEL), convb.reshape(b * t, CONV_CH), attnb.reshape(b * t, ATTN_W),
                  p["w_out"], tm=1024).reshape(b, t, D_MODEL)
    w_keep = min(WINDOW, t)
    return (y, _kv_rows(kvc), _kv_rows(kvs), _kv_rows(kvw[:, t - w_keep:]), utail[:, 8 - (CONV_W - 1):])


def _sample_layer(x, cache_cmp, cache_slc, win_buf, conv_buf, page_table, p):
    nb, t, _ = x.shape
    assert t == 1
    n_pages = page_table.shape[1]
    past = n_pages * PAGE
    assert past % (2 * SEL_BLK) == 0 and past // SEL_BLK >= LANES and past // SEL_BLK >= N_SEL
    n_pool = cache_cmp.shape[0]
    cos1, sin1 = _rope_tables(jnp.full((1,), past, dtype=jnp.int32))
    convb, sz, qbd, kvn, gate, u = _sample_in(
        x.reshape(nb, D_MODEL), p["ng"], p["w_all"], p["cw"], p["cb"], p["qg"], p["kg"], cos1, sin1,
        conv_buf[:, 0], conv_buf[:, 1])
    qbd = jnp.transpose(qbd, (1, 0, 2))
    feature_major = lambda a: jnp.transpose(a, (0, 2, 3, 4, 1)).reshape(a.shape[0], KV_LANES, a.shape[1])
    ocmp, imp = _sample_cmp(page_table, qbd, feature_major(cache_cmp), p["pe_t"], p["perm"], p["wbd"], past)
    idx = _sample_select(imp[:, :N_KV].reshape(nb * N_KV, past // CMP_BLK), past)
    sel = idx[:, :N_SEL - 1].reshape(nb, N_KV * (N_SEL - 1))
    attn = _sample_attn(page_table, sel, qbd, feature_major(cache_slc), kvn.reshape(nb, 1, N_BRANCH * KV_LANES),
                        feature_major(win_buf), ocmp, gate.reshape(nb, 1, LANES), past)
    y = _sample_out(x.reshape(nb, D_MODEL), convb, attn.reshape(nb, ATTN_W), sz, p["w_out"]).reshape(nb, 1, D_MODEL)
    kv_new = _kv_rows(kvn.reshape(nb, 1, N_BRANCH, KV_LANES))
    win_new = jnp.concatenate([win_buf, kv_new[:, :, 2]], axis=1)[:, t:]
    conv_new = jnp.concatenate([conv_buf, u.reshape(nb, 1, CONV_CH)], axis=1)[:, t:]
    return y, kv_new[:, :, 0], kv_new[:, :, 1], win_new, conv_new


def kernel(x_prompt, x_sample, cache_cmp_kv, cache_slc_kv, state_win_kv, state_conv, page_table, norm_g, w_in,
           conv_w, conv_b, q_gain, k_gain, cmp_pe, cmp_w, w_out):
    yp, ys = x_prompt, x_sample
    outs = [[] for _ in range(8)]
    for layer in range(norm_g.shape[0]):
        p = _layer_params(norm_g[layer], w_in[layer], conv_w[layer], conv_b[layer], q_gain[layer], k_gain[layer],
                          cmp_pe[layer], cmp_w[layer], w_out[layer])
        yp, *prompt_state = _prompt_layer(yp, p)
        ys, *sample_state = _sample_layer(ys, cache_cmp_kv[layer], cache_slc_kv[layer], state_win_kv[layer],
                                          state_conv[layer], page_table, p)
        for acc, a in zip(outs, prompt_state + sample_state):
            acc.append(a)
    return (yp, ys) + tuple(jnp.stack(a) for a in outs)
```
